```python
import math
import jax, jax.numpy as jnp
from jax import lax
import numpy as np

D_MODEL = 1024
BATCH = 32
SEQ = 256
DEPTH = 2
DEC_BATCH = 4
DEC_SEQ = 1024
PAST_LEN = 512

GRID_W = 64
QUERY_BLOCK = 128
ROPE_THETA = 10000.0
EPS = 1e-6
DIFF_HEADS = 4
DIFF_QK = 32
DIFF_V = 64
GQA_HEADS = 4
GQA_KV_HEADS = 2
GQA_HEAD_DIM = 64
SSM_WIDTH = 256
SSM_GROUP = 16
SSM_GROUPS = SSM_WIDTH // SSM_GROUP
SSM_STATE = 64
MLA_HEADS = 4
MLA_Q_RANK = 192
MLA_KV_RANK = 128
MLA_NOPE = 64
MLA_ROPE = 32
MLA_V = 64
MIX_WIDTH = DIFF_HEADS * DIFF_V + GQA_HEADS * GQA_HEAD_DIM + SSM_WIDTH + MLA_HEADS * MLA_V
IN_SIZES = (DIFF_HEADS * 2 * DIFF_QK, DIFF_HEADS * 2 * DIFF_QK, DIFF_HEADS * DIFF_V,
            GQA_HEADS * GQA_HEAD_DIM, GQA_KV_HEADS * GQA_HEAD_DIM, GQA_KV_HEADS * GQA_HEAD_DIM,
            SSM_WIDTH, MLA_Q_RANK, MLA_KV_RANK, MLA_ROPE)
IN_COLS = sum(IN_SIZES)
D_FF = 4 * D_MODEL
N_MOD = 6

kernel_name = 'hybrid_prefix_diffusion_step'

F32 = jnp.float32


def rmsnorm(x, g):
    xf = x.astype(F32)
    y = xf * lax.rsqrt(jnp.mean(xf * xf, axis=-1, keepdims=True) + EPS)
    return (y * g.astype(F32)).astype(x.dtype)


def rope_angles(t, rot_dim):
    rows = t // GRID_W
    row = jnp.repeat(jnp.arange(rows), GRID_W).astype(F32)
    col = jnp.tile(jnp.arange(GRID_W), rows).astype(F32)
    n = rot_dim // 4
    freq = ROPE_THETA ** (-jnp.arange(n, dtype=F32) / n)
    return row[:, None] * freq, col[:, None] * freq


def apply_rope2d(x, ang_row, ang_col):
    shape = (x.shape[1],) + (1,) * (x.ndim - 3) + (ang_row.shape[-1],)
    half = x.shape[-1] // 2
    xf = x.astype(F32)

    def rot(part, ang):
        cos = jnp.cos(ang).reshape(shape)
        sin = jnp.sin(ang).reshape(shape)
        x1, x2 = jnp.split(part, 2, axis=-1)
        return jnp.concatenate([x1 * cos - x2 * sin, x1 * sin + x2 * cos], axis=-1)

    out = jnp.concatenate([rot(xf[..., :half], ang_row), rot(xf[..., half:], ang_col)], axis=-1)
    return out.astype(x.dtype)


def attend(q, k, v, scale):
    b, tq, hq, dk = q.shape
    hkv = k.shape[2]
    g = hq // hkv
    dv = v.shape[-1]
    qb = min(QUERY_BLOCK, tq)
    nb = tq // qb
    qs = (q.astype(F32) * scale).reshape(b, nb, qb, hkv, g, dk).transpose(1, 0, 2, 3, 4, 5)
    kf = k.astype(F32)
    vf = v.astype(F32)

    def block(qblk):
        s = jnp.einsum('bqhgd,bkhd->bhgqk', qblk, kf)
        p = jax.nn.softmax(s, axis=-1)
        return jnp.einsum('bhgqk,bkhd->bqhgd', p, vf)

    o = lax.map(block, qs)
    return o.transpose(1, 0, 2, 3, 4, 5).reshape(b, tq, hq, dv)


def diff_attention(q, k, v, p, lam_init):
    lam = (jnp.exp(jnp.sum(p['diff_lq1'].astype(F32) * p['diff_lk1'].astype(F32)))
           - jnp.exp(jnp.sum(p['diff_lq2'].astype(F32) * p['diff_lk2'].astype(F32))) + lam_init)
    scale = DIFF_QK ** -0.5
    o1 = attend(q[..., 0, :], k[..., 0, :], v, scale)
    o2 = attend(q[..., 1, :], k[..., 1, :], v, scale)
    o = rmsnorm(o1 - lam * o2, p['diff_subln_g']) * (1.0 - lam_init)
    return o.reshape(o.shape[0], o.shape[1], -1)


def mla_expand(ckv_n, k_rope, w_ukv):
    b, t, _ = ckv_n.shape
    kv = (ckv_n @ w_ukv).reshape(b, t, MLA_HEADS, MLA_NOPE + MLA_V)
    k_nope, v = kv[..., :MLA_NOPE], kv[..., MLA_NOPE:]
    k_r = jnp.broadcast_to(k_rope[:, :, None, :], (b, t, MLA_HEADS, MLA_ROPE)).astype(k_nope.dtype)
    return jnp.concatenate([k_nope, k_r], axis=-1), v


def _linrec(e1, e2):
    a1, b1 = e1
    a2, b2 = e2
    return a1 * a2, a2 * b1 + b2


def ssm_scan(u, a_re, a_im, log_dt, b_re, b_im, c_re, c_im, h0):
    lam = lax.complex(a_re.astype(F32), a_im.astype(F32))
    dt = jnp.exp(log_dt.astype(F32))[:, None]
    a_bar = jnp.exp(lam * dt)
    b_bar = ((a_bar - 1.0) / lam)[..., None] * lax.complex(b_re.astype(F32), b_im.astype(F32))
    c_mat = lax.complex(c_re.astype(F32), c_im.astype(F32))
    bu = jnp.einsum('gpc,btgc->btgp', b_bar, u.astype(jnp.complex64))
    bu = bu.at[:, 0].add(a_bar * h0)
    a_seq = jnp.broadcast_to(a_bar, bu.shape)
    _, h = lax.associative_scan(_linrec, (a_seq, bu), axis=1)
    y = jnp.einsum('gcp,btgp->btgc', c_mat, h).real
    return y, h[:, -1]


def ssm_mixer(u, h0_re, h0_im, p):
    b, t, _ = u.shape
    uf = u.astype(F32).reshape(b, t, SSM_GROUPS, SSM_GROUP)
    h0 = lax.complex(h0_re.astype(F32), h0_im.astype(F32))
    ys, finals = [], []
    for d in range(2):
        ud = uf if d == 0 else jnp.flip(uf, axis=1)
        y, h_last = ssm_scan(ud, p['ssm_a_re'][d], p['ssm_a_im'][d], p['ssm_log_dt'][d],
                             p['ssm_b_re'][d], p['ssm_b_im'][d], p['ssm_c_re'][d], p['ssm_c_im'][d],
                             h0[:, d])
        ys.append(y if d == 0 else jnp.flip(y, axis=1))
        finals.append(h_last)
    y = ys[0] + ys[1] + uf * p['ssm_d'].astype(F32).reshape(SSM_GROUPS, SSM_GROUP)
    g = jax.nn.gelu(y.reshape(b, t, SSM_WIDTH))
    z = g @ p['ssm_w_glu'].astype(F32)
    out = z[..., :SSM_WIDTH] * jax.nn.sigmoid(z[..., SSM_WIDTH:])
    h_fin = jnp.stack(finals, axis=1)
    return out, h_fin.real, h_fin.imag


def token_mixers(h, p, lam_init, ctx, angs):
    b, t, _ = h.shape
    idx = np.cumsum(IN_SIZES)[:-1].tolist()
    dq, dk, dv, gq, gk, gv, u, cq, ckv, kr = jnp.split(h @ p['w_in'], idx, axis=-1)
    dq = dq.reshape(b, t, DIFF_HEADS, 2, DIFF_QK)
    dk = dk.reshape(b, t, DIFF_HEADS, 2, DIFF_QK)
    dv = dv.reshape(b, t, DIFF_HEADS, DIFF_V)
    gq = rmsnorm(gq.reshape(b, t, GQA_HEADS, GQA_HEAD_DIM), p['gqa_qn_g'])
    gk = rmsnorm(gk.reshape(b, t, GQA_KV_HEADS, GQA_HEAD_DIM), p['gqa_kn_g'])
    gv = gv.reshape(b, t, GQA_KV_HEADS, GQA_HEAD_DIM)
    ckv_n = rmsnorm(ckv, p['mla_kvn_g'])
    mq = (rmsnorm(cq, p['mla_qn_g']) @ p['mla_w_uq']).reshape(b, t, MLA_HEADS, MLA_NOPE + MLA_ROPE)
    own = (dk.reshape(b, t, DIFF_HEADS, 2 * DIFF_QK), dv, gk, gv, ckv_n, kr)
    if angs is not None:
        a_diff, a_gqa, a_mla = angs
        dq = apply_rope2d(dq, *a_diff)
        dk = apply_rope2d(dk, *a_diff)
        gq = apply_rope2d(gq, *a_gqa)
        gk = apply_rope2d(gk, *a_gqa)
        mq = jnp.concatenate([mq[..., :MLA_NOPE], apply_rope2d(mq[..., MLA_NOPE:], *a_mla)], axis=-1)
        kr = apply_rope2d(kr, *a_mla)
    mk, mv = mla_expand(ckv_n, kr, p['mla_w_ukv'])
    if ctx is None:
        h0_re = jnp.zeros((b, 2, SSM_GROUPS, SSM_STATE), F32)
        h0_im = jnp.zeros((b, 2, SSM_GROUPS, SSM_STATE), F32)
    else:
        c_dk, c_dv, c_gk, c_gv, c_ckv, c_kr, h0_re, h0_im = ctx
        dk = jnp.concatenate([c_dk.reshape(b, -1, DIFF_HEADS, 2, DIFF_QK), dk], axis=1)
        dv = jnp.concatenate([c_dv, dv], axis=1)
        gk = jnp.concatenate([c_gk, gk], axis=1)
        gv = jnp.concatenate([c_gv, gv], axis=1)
        c_mk, c_mv = mla_expand(c_ckv, c_kr, p['mla_w_ukv'])
        mk = jnp.concatenate([c_mk, mk], axis=1)
        mv = jnp.concatenate([c_mv, mv], axis=1)
    o_diff = diff_attention(dq, dk, dv, p, lam_init)
    o_gqa = attend(gq, gk, gv, GQA_HEAD_DIM ** -0.5).reshape(b, t, -1)
    o_ssm, hT_re, hT_im = ssm_mixer(u, h0_re, h0_im, p)
    o_mla = attend(mq, mk, mv, (MLA_NOPE + MLA_ROPE) ** -0.5).reshape(b, t, -1)
    mixed = jnp.concatenate([o_diff, o_gqa, o_ssm, o_mla], axis=-1).astype(h.dtype)
    return mixed @ p['w_out'], own + (hT_re, hT_im)


def sq_relu_mlp(h, p):
    a = jax.nn.relu(h @ p['mlp_w1'])
    return (a * a) @ p['mlp_w2']


def adaln(cond, p):
    return jax.nn.silu(cond.astype(F32)) @ p['w_ada'] + p['b_ada']


def layer(x, mod, p, lam_init, ctx, angs):
    sh1, sc1, g1, sh2, sc2, g2 = jnp.split(mod.astype(x.dtype), N_MOD, axis=-1)
    h = rmsnorm(x, p['norm1_g']) * (1.0 + sc1) + sh1
    mix, st = token_mixers(h, p, lam_init, ctx, angs)
    x = x + g1 * mix
    h = rmsnorm(x, p['norm2_g']) * (1.0 + sc2) + sh2
    x = x + g2 * sq_relu_mlp(h, p)
    return x, st


def setup_inputs(seed: int = 0) -> dict:
    key = jax.random.key(seed)
    ks = iter(jax.random.split(key, 64))
    L, G, P, C = DEPTH, SSM_GROUPS, SSM_STATE, SSM_GROUP

    def nrm(shape, scale=1.0):
        return jax.random.normal(next(ks), shape, F32) * scale

    def gain(shape):
        return 1.0 + nrm(shape, 0.02)

    return {
        'x_prompt': nrm((BATCH, SEQ, D_MODEL)),
        'x_sample': nrm((DEC_BATCH, DEC_SEQ, D_MODEL)),
        'cache_diff_k': nrm((DEC_BATCH, L, PAST_LEN, DIFF_HEADS, 2 * DIFF_QK)),
        'cache_diff_v': nrm((DEC_BATCH, L, PAST_LEN, DIFF_HEADS, DIFF_V)),
        'cache_gqa_k': nrm((DEC_BATCH, L, PAST_LEN, GQA_KV_HEADS, GQA_HEAD_DIM)),
        'cache_gqa_v': nrm((DEC_BATCH, L, PAST_LEN, GQA_KV_HEADS, GQA_HEAD_DIM)),
        'cache_mla_ckv': nrm((DEC_BATCH, L, PAST_LEN, MLA_KV_RANK)),
        'cache_mla_krope': nrm((DEC_BATCH, L, PAST_LEN, MLA_ROPE)),
        'state_ssm_re': nrm((DEC_BATCH, L, 2, G, P), 0.3),
        'state_ssm_im': nrm((DEC_BATCH, L, 2, G, P), 0.3),
        'c': nrm((DEC_BATCH, D_MODEL)),
        'c_ctx': nrm((D_MODEL,)),
        'norm1_g': gain((L, D_MODEL)),
        'norm2_g': gain((L, D_MODEL)),
        'w_ada': nrm((L, D_MODEL, N_MOD * D_MODEL), 0.5 * D_MODEL ** -0.5),
        'b_ada': nrm((L, N_MOD * D_MODEL), 0.02),
        'w_in': nrm((L, D_MODEL, IN_COLS), D_MODEL ** -0.5),
        'w_out': nrm((L, MIX_WIDTH, D_MODEL), MIX_WIDTH ** -0.5),
        'diff_lq1': nrm((L, DIFF_QK), 0.1),
        'diff_lk1': nrm((L, DIFF_QK), 0.1),
        'diff_lq2': nrm((L, DIFF_QK), 0.1),
        'diff_lk2': nrm((L, DIFF_QK), 0.1),
        'diff_subln_g': gain((L, DIFF_V)),
        'gqa_qn_g': gain((L, GQA_HEAD_DIM)),
        'gqa_kn_g': gain((L, GQA_HEAD_DIM)),
        'ssm_a_re': -0.5 + nrm((L, 2, G, P), 0.01),
        'ssm_a_im': math.pi * jnp.arange(P, dtype=F32) + nrm((L, 2, G, P), 0.01),
        'ssm_log_dt': jax.random.uniform(next(ks), (L, 2, G), F32, math.log(1e-3), math.log(1e-1)),
        'ssm_b_re': nrm((L, 2, G, P, C), (2 * C) ** -0.5),
        'ssm_b_im': nrm((L, 2, G, P, C), (2 * C) ** -0.5),
        'ssm_c_re': nrm((L, 2, G, C, P), (2 * P) ** -0.5),
        'ssm_c_im': nrm((L, 2, G, C, P), (2 * P) ** -0.5),
        'ssm_d': nrm((L, SSM_WIDTH)),
        'ssm_w_glu': nrm((L, SSM_WIDTH, 2 * SSM_WIDTH), SSM_WIDTH ** -0.5),
        'mla_qn_g': gain((L, MLA_Q_RANK)),
        'mla_kvn_g': gain((L, MLA_KV_RANK)),
        'mla_w_uq': nrm((L, MLA_Q_RANK, MLA_HEADS * (MLA_NOPE + MLA_ROPE)), MLA_Q_RANK ** -0.5),
        'mla_w_ukv': nrm((L, MLA_KV_RANK, MLA_HEADS * (MLA_NOPE + MLA_V)), MLA_KV_RANK ** -0.5),
        'mlp_w1': nrm((L, D_MODEL, D_FF), D_MODEL ** -0.5),
        'mlp_w2': nrm((L, D_FF, D_MODEL), D_FF ** -0.5),
        'final_norm_g': gain((D_MODEL,)),
    }


def reference(x_prompt, x_sample, cache_diff_k, cache_diff_v, cache_gqa_k, cache_gqa_v, cache_mla_ckv,
              cache_mla_krope, state_ssm_re, state_ssm_im, c, c_ctx, norm1_g, norm2_g, w_ada, b_ada, w_in,
              w_out, diff_lq1, diff_lk1, diff_lq2, diff_lk2, diff_subln_g, gqa_qn_g, gqa_kn_g, ssm_a_re,
              ssm_a_im, ssm_log_dt, ssm_b_re, ssm_b_im, ssm_c_re, ssm_c_im, ssm_d, ssm_w_glu, mla_qn_g,
              mla_kvn_g, mla_w_uq, mla_w_ukv, mlp_w1, mlp_w2, final_norm_g):
    layers = [dict(norm1_g=norm1_g[l], norm2_g=norm2_g[l], w_ada=w_ada[l], b_ada=b_ada[l], w_in=w_in[l],
                   w_out=w_out[l], diff_lq1=diff_lq1[l], diff_lk1=diff_lk1[l], diff_lq2=diff_lq2[l],
                   diff_lk2=diff_lk2[l], diff_subln_g=diff_subln_g[l], gqa_qn_g=gqa_qn_g[l],
                   gqa_kn_g=gqa_kn_g[l], ssm_a_re=ssm_a_re[l], ssm_a_im=ssm_a_im[l],
                   ssm_log_dt=ssm_log_dt[l], ssm_b_re=ssm_b_re[l], ssm_b_im=ssm_b_im[l],
                   ssm_c_re=ssm_c_re[l], ssm_c_im=ssm_c_im[l], ssm_d=ssm_d[l], ssm_w_glu=ssm_w_glu[l],
                   mla_qn_g=mla_qn_g[l], mla_kvn_g=mla_kvn_g[l], mla_w_uq=mla_w_uq[l],
                   mla_w_ukv=mla_w_ukv[l], mlp_w1=mlp_w1[l], mlp_w2=mlp_w2[l])
              for l in range(DEPTH)]
    lam_inits = [0.8 - 0.6 * math.exp(-0.3 * l) for l in range(DEPTH)]

    x = x_prompt
    ctx_states = []
    for l in range(DEPTH):
        p = layers[l]
        x, st = layer(x, adaln(c_ctx, p)[None, None], p, lam_inits[l], None, None)
        ctx_states.append(st)
    y_prompt = rmsnorm(x, final_norm_g)

    t_lat = x_sample.shape[1]
    angs = (rope_angles(t_lat, DIFF_QK), rope_angles(t_lat, GQA_HEAD_DIM), rope_angles(t_lat, MLA_ROPE))
    x = x_sample
    for l in range(DEPTH):
        p = layers[l]
        ctx = (cache_diff_k[:, l], cache_diff_v[:, l], cache_gqa_k[:, l], cache_gqa_v[:, l],
               cache_mla_ckv[:, l], cache_mla_krope[:, l], state_ssm_re[:, l], state_ssm_im[:, l])
        x, _ = layer(x, adaln(c, p)[:, None], p, lam_inits[l], ctx, angs)
    y_sample = rmsnorm(x, final_norm_g)

    new_diff_k = jnp.stack([s[0] for s in ctx_states], axis=1)
    new_diff_v = jnp.stack([s[1] for s in ctx_states], axis=1)
    new_gqa_k = jnp.stack([s[2] for s in ctx_states], axis=1)
    new_gqa_v = jnp.stack([s[3] for s in ctx_states], axis=1)
    new_mla_ckv = jnp.stack([s[4] for s in ctx_states], axis=1)
    new_mla_krope = jnp.stack([s[5] for s in ctx_states], axis=1)
    new_ssm_re = jnp.stack([s[6] for s in ctx_states], axis=1)
    new_ssm_im = jnp.stack([s[7] for s in ctx_states], axis=1)
    return (y_prompt, y_sample, new_diff_k, new_diff_v, new_gqa_k, new_gqa_v, new_mla_ckv, new_mla_krope,
            new_ssm_re, new_ssm_im)
```

```python
import functools
import math

import numpy as np
import jax
import jax.numpy as jnp
from jax import lax
from jax.experimental import pallas as pl
from jax.experimental.pallas import tpu as pltpu

F32 = jnp.float32
BF16 = jnp.bfloat16

D_MODEL = 1024
GRID_W = 64
ROPE_THETA = 10000.0
EPS = 1e-6
DIFF_HEADS, DIFF_QK, DIFF_V = 4, 32, 64
GQA_HEADS, GQA_KV_HEADS, GQA_HEAD_DIM = 4, 2, 64
SSM_WIDTH, SSM_GROUP, SSM_STATE = 256, 16, 64
SSM_GROUPS = SSM_WIDTH // SSM_GROUP
SSM_LANES = SSM_GROUPS * SSM_STATE
MLA_HEADS, MLA_Q_RANK, MLA_KV_RANK, MLA_NOPE, MLA_ROPE, MLA_V = 4, 192, 128, 64, 32, 64
D_FF = 4 * D_MODEL
N_MOD = 6
IN_COLS = 1888
MOD_ROWS = 8

P_DQ, P_DK, P_DV, P_GQ, P_GK, P_GV, P_U, P_CKV, P_CQ, P_KR, P_END = (
    0, 256, 512, 768, 1024, 1152, 1280, 1536, 1664, 1920, 2048)
CQ_PAD = 256
MLA_BLK = 128
Q_COLS = 256 + 256 + MLA_HEADS * MLA_BLK
K_COLS = 256 + 128 + MLA_HEADS * MLA_BLK
V_COLS = 256 + 128 + 256
O_COLS = 768

TOKEN_TILE = 256
VMEM_LIMIT = 48 * 1024 * 1024


def _dot(a, b):
    return jnp.dot(a, b, preferred_element_type=F32)


def _dot_nt(a, b):
    return lax.dot_general(a, b, (((1,), (1,)), ((), ())), preferred_element_type=F32)


def _block_ones(width, seg):
    shift = seg.bit_length() - 1
    r = jnp.right_shift(lax.broadcasted_iota(jnp.int32, (width, width), 0), shift)
    c = jnp.right_shift(lax.broadcasted_iota(jnp.int32, (width, width), 1), shift)
    return jnp.where(r == c, 1.0, 0.0).astype(BF16)


def _seg_sum_sq(y, seg):
    ones = _block_ones(y.shape[-1], seg)
    sq = y * y
    hi = sq.astype(BF16)
    lo = (sq - hi.astype(F32)).astype(BF16)
    return _dot(hi, ones) + _dot(lo, ones)


def _rope(x, tab_ref, quarter):
    outs = []
    for j in range(x.shape[-1] // 128):
        sl = slice(j * 128, (j + 1) * 128)
        xs = x[:, sl]
        up = pltpu.roll(xs, 128 - quarter, axis=1)
        dn = pltpu.roll(xs, quarter, axis=1)
        outs.append(xs * tab_ref[0, :, sl] + up * tab_ref[1, :, sl] + dn * tab_ref[2, :, sl])
    return outs[0] if len(outs) == 1 else jnp.concatenate(outs, axis=-1)


def _sigmoid(x):
    return 1.0 / (1.0 + jnp.exp(-x))


def _adaln_kernel(cond_ref, w_ref, b_ref, o_ref):
    c = cond_ref[...]
    s = c * _sigmoid(c)
    o_ref[...] = _dot(s.astype(BF16), w_ref[...].astype(BF16)) + b_ref[...]


def _adaln(cond, w_ada, b_ada):
    n_layers = w_ada.shape[0]
    tn = 1024
    return pl.pallas_call(
        _adaln_kernel,
        grid=(n_layers, N_MOD * D_MODEL // tn),
        in_specs=[
            pl.BlockSpec((MOD_ROWS, D_MODEL), lambda l, j: (0, 0)),
            pl.BlockSpec((None, D_MODEL, tn), lambda l, j: (l, 0, j)),
            pl.BlockSpec((None, 1, tn), lambda l, j: (l, 0, j)),
        ],
        out_specs=pl.BlockSpec((None, MOD_ROWS, tn), lambda l, j: (l, 0, j)),
        out_shape=jax.ShapeDtypeStruct((n_layers, MOD_ROWS, N_MOD * D_MODEL), F32),
        compiler_params=pltpu.CompilerParams(vmem_limit_bytes=VMEM_LIMIT),
        name="adaln",
    )(cond, w_ada, b_ada.reshape(n_layers, 1, N_MOD * D_MODEL))


def _in_proj_kernel(*refs, latent):
    (x_ref, mod_ref, n1g_ref, win_ref, qng_ref, kng_ref, mqng_ref, kvng_ref, wuq_ref, wukv_ref) = refs[:10]
    if latent:
        rope_d_ref, rope_g_ref, rope_m_ref = refs[10:13]
        q_ref, k_ref, v_ref, u_ref = refs[13:]
    else:
        q_ref, k_ref, v_ref, u_ref, cdk_ref, cdv_ref, cgk_ref, cgv_ref, cckv_ref, ckr_ref = refs[10:]

    x = x_ref[...]
    sh1 = mod_ref[:, 0:D_MODEL]
    sc1 = mod_ref[:, D_MODEL:2 * D_MODEL]
    ms = jnp.mean(x * x, axis=-1, keepdims=True)
    h = x * lax.rsqrt(ms + EPS) * n1g_ref[...]
    h = h * (1.0 + sc1) + sh1
    proj = _dot(h.astype(BF16), win_ref[...])

    dq = proj[:, P_DQ:P_DK]
    dk = proj[:, P_DK:P_DV]
    dv = proj[:, P_DV:P_GQ]
    gq = proj[:, P_GQ:P_GK]
    gk = proj[:, P_GK:P_GV]
    gv = proj[:, P_GV:P_U]
    u_ref[...] = proj[:, P_U:P_CKV]
    ckv = proj[:, P_CKV:P_CQ]
    cq = proj[:, P_CQ:P_KR]
    krb = proj[:, P_KR:P_END]

    gq = gq * lax.rsqrt(_seg_sum_sq(gq, GQA_HEAD_DIM) * (1.0 / GQA_HEAD_DIM) + EPS) * qng_ref[...]
    gk = gk * lax.rsqrt(_seg_sum_sq(gk, GQA_HEAD_DIM) * (1.0 / GQA_HEAD_DIM) + EPS) * kng_ref[...]
    ckv_n = ckv * lax.rsqrt(jnp.mean(ckv * ckv, axis=-1, keepdims=True) + EPS) * kvng_ref[...]
    cq_ms = jnp.sum(cq * cq, axis=-1, keepdims=True) * (1.0 / MLA_Q_RANK)
    cq_n = cq * lax.rsqrt(cq_ms + EPS) * mqng_ref[...]
    mq = _dot(cq_n.astype(BF16), wuq_ref[...])
    kv = _dot(ckv_n.astype(BF16), wukv_ref[...])

    if not latent:
        cdk_ref[...] = dk
        cdv_ref[...] = dv
        cgk_ref[...] = gk
        cgv_ref[...] = gv
        cckv_ref[...] = ckv_n
        ckr_ref[...] = krb[:, MLA_NOPE:MLA_NOPE + MLA_ROPE]
    else:
        dq = _rope(dq, rope_d_ref, DIFF_QK // 4)
        dk = _rope(dk, rope_d_ref, DIFF_QK // 4)
        gq = _rope(gq, rope_g_ref, GQA_HEAD_DIM // 4)
        gk = _rope(gk, rope_g_ref, GQA_HEAD_DIM // 4)
        mq = jnp.concatenate(
            [_rope(mq[:, hd * MLA_BLK:(hd + 1) * MLA_BLK], rope_m_ref, MLA_ROPE // 4)
             for hd in range(MLA_HEADS)], axis=-1)
        krb = _rope(krb, rope_m_ref, MLA_ROPE // 4)

    q_ref[:, 0:256] = (dq * (DIFF_QK ** -0.5)).astype(BF16)
    q_ref[:, 256:512] = (gq * (GQA_HEAD_DIM ** -0.5)).astype(BF16)
    q_ref[:, 512:Q_COLS] = (mq * ((MLA_NOPE + MLA_ROPE) ** -0.5)).astype(BF16)
    k_ref[:, 0:256] = dk.astype(BF16)
    k_ref[:, 256:384] = gk.astype(BF16)
    for hd in range(MLA_HEADS):
        k_ref[:, 384 + hd * MLA_BLK:384 + (hd + 1) * MLA_BLK] = (
            kv[:, hd * MLA_BLK:(hd + 1) * MLA_BLK] + krb).astype(BF16)
    v_ref[:, 0:256] = dv.astype(BF16)
    v_ref[:, 256:384] = gv.astype(BF16)
    v_ref[:, 384:V_COLS] = kv[:, MLA_HEADS * MLA_BLK:].astype(BF16)


def _const_spec(shape):
    nd = len(shape)
    return pl.BlockSpec(shape, lambda i: (0,) * nd)


def _in_proj(x2d, mod3, layer, weights, n_batch, seq, latent, rope_tabs=None):
    tm = TOKEN_TILE
    tpb = seq // tm
    n_tok = n_batch * seq
    if latent:
        mod_row = lambda i: (layer * MOD_ROWS + 1 + i // tpb, 0, 0)
    else:
        mod_row = lambda i: (layer * MOD_ROWS, 0, 0)
    tile = lambda i: (i, 0)
    in_specs = [
        pl.BlockSpec((tm, D_MODEL), tile),
        pl.BlockSpec((None, 1, N_MOD * D_MODEL), mod_row),
        _const_spec((1, D_MODEL)),
        _const_spec((D_MODEL, P_END)),
        _const_spec((1, 256)),
        _const_spec((1, 128)),
        _const_spec((1, CQ_PAD)),
        _const_spec((1, MLA_KV_RANK)),
        _const_spec((CQ_PAD, MLA_HEADS * MLA_BLK)),
        _const_spec((MLA_KV_RANK, MLA_HEADS * MLA_BLK + 256)),
    ]
    args = [x2d, mod3, weights["n1g"], weights["w_in"], weights["qng"], weights["kng"], weights["mqng"],
            weights["kvng"], weights["w_uq"], weights["w_ukv"]]
    out_shape = [
        jax.ShapeDtypeStruct((n_tok, Q_COLS), BF16),
        jax.ShapeDtypeStruct((n_tok, K_COLS), BF16),
        jax.ShapeDtypeStruct((n_tok, V_COLS), BF16),
        jax.ShapeDtypeStruct((seq, n_batch * SSM_WIDTH), F32),
    ]
    out_specs = [
        pl.BlockSpec((tm, Q_COLS), tile),
        pl.BlockSpec((tm, K_COLS), tile),
        pl.BlockSpec((tm, V_COLS), tile),
        pl.BlockSpec((tm, SSM_WIDTH), lambda i: (i % tpb, i // tpb)),
    ]
    if latent:
        pos = lambda i: (0, i % tpb, 0)
        in_specs += [pl.BlockSpec((3, tm, 256), pos), pl.BlockSpec((3, tm, 256), pos),
                     pl.BlockSpec((3, tm, 128), pos)]
        args += list(rope_tabs)
    else:
        for width in (256, 256, 128, 128, MLA_KV_RANK, MLA_ROPE):
            out_shape.append(jax.ShapeDtypeStruct((n_tok, width), F32))
            out_specs.append(pl.BlockSpec((tm, width), tile))
    return pl.pallas_call(
        functools.partial(_in_proj_kernel, latent=latent),
        grid=(n_tok // tm,),
        in_specs=in_specs,
        out_specs=out_specs,
        out_shape=out_shape,
        compiler_params=pltpu.CompilerParams(vmem_limit_bytes=VMEM_LIMIT),
        name="in_proj_lat" if latent else "in_proj_ctx",
    )(*args)


def _cache_prep_kernel(cdk_ref, cdv_ref, cgk_ref, cgv_ref, cckv_ref, ckrb_ref, wukv_ref, k_ref, v_ref):
    kv = _dot(cckv_ref[...].astype(BF16), wukv_ref[...])
    krb = ckrb_ref[...]
    k_ref[:, 0:256] = cdk_ref[...].astype(BF16)
    k_ref[:, 256:384] = cgk_ref[...].astype(BF16)
    for hd in range(MLA_HEADS):
        k_ref[:, 384 + hd * MLA_BLK:384 + (hd + 1) * MLA_BLK] = (
            kv[:, hd * MLA_BLK:(hd + 1) * MLA_BLK] + krb).astype(BF16)
    v_ref[:, 0:256] = cdv_ref[...].astype(BF16)
    v_ref[:, 256:384] = cgv_ref[...].astype(BF16)
    v_ref[:, 384:V_COLS] = kv[:, MLA_HEADS * MLA_BLK:].astype(BF16)


def _cache_prep(caches, layer, w_ukv):
    cdk, cdv, cgk, cgv, cckv, ckrb = caches
    n_batch, _, past, _ = cdk.shape

    def spec(width):
        return pl.BlockSpec((None, None, past, width), lambda b: (b, layer, 0, 0))

    return pl.pallas_call(
        _cache_prep_kernel,
        grid=(n_batch,),
        in_specs=[spec(256), spec(256), spec(128), spec(128), spec(MLA_KV_RANK), spec(MLA_BLK),
                  _const_spec((MLA_KV_RANK, MLA_HEADS * MLA_BLK + 256))],
        out_specs=[pl.BlockSpec((None, past, K_COLS), lambda b: (b, 0, 0)),
                   pl.BlockSpec((None, past, V_COLS), lambda b: (b, 0, 0))],
        out_shape=[jax.ShapeDtypeStruct((n_batch, past, K_COLS), BF16),
                   jax.ShapeDtypeStruct((n_batch, past, V_COLS), BF16)],
        name="cache_prep",
    )(cdk, cdv, cgk, cgv, cckv, ckrb, w_ukv)


def _softmax(s):
    m = jnp.max(s, axis=-1, keepdims=True)
    e = jnp.exp(s - m)
    return e * (1.0 / jnp.sum(e, axis=-1, keepdims=True))


def _lane_mask(width, lo, hi):
    lane = lax.broadcasted_iota(jnp.int32, (1, width), 1)
    return jnp.where((lane >= lo) & (lane < hi), 1.0, 0.0).astype(BF16)


def _attn_kernel(q_ref, k_ref, v_ref, lq1_ref, lk1_ref, lq2_ref, lk2_ref, subg_ref, o_ref, *, lam_init):
    tq = q_ref.shape[0]
    lam = (jnp.exp(jnp.sum(lq1_ref[...] * lk1_ref[...], axis=-1, keepdims=True))
           - jnp.exp(jnp.sum(lq2_ref[...] * lk2_ref[...], axis=-1, keepdims=True)) + lam_init)

    dq = q_ref[:, 0:256]
    dk = k_ref[:, 0:256]
    dv = v_ref[:, 0:256]
    acc = jnp.zeros((tq, 256), F32)
    for hd in range(DIFF_HEADS):
        base = hd * 2 * DIFF_QK
        p1 = _softmax(_dot_nt(dq * _lane_mask(256, base, base + DIFF_QK), dk))
        p2 = _softmax(_dot_nt(dq * _lane_mask(256, base + DIFF_QK, base + 2 * DIFF_QK), dk))
        p = p1 - lam * p2
        acc = acc + _dot(p.astype(BF16), dv * _lane_mask(256, hd * DIFF_V, (hd + 1) * DIFF_V))
    ms = _seg_sum_sq(acc, DIFF_V) * (1.0 / DIFF_V)
    o_ref[:, 0:256] = (acc * lax.rsqrt(ms + EPS) * subg_ref[...] * (1.0 - lam_init)).astype(o_ref.dtype)

    gk = k_ref[:, 256:384]
    gv = v_ref[:, 256:384]
    for blk in range(2):
        gq = q_ref[:, 256 + blk * 128:256 + (blk + 1) * 128]
        acc = jnp.zeros((tq, 128), F32)
        for side in range(2):
            msk = _lane_mask(128, side * GQA_HEAD_DIM, (side + 1) * GQA_HEAD_DIM)
            p = _softmax(_dot_nt(gq * msk, gk))
            acc = acc + _dot(p.astype(BF16), gv * msk)
        o_ref[:, 256 + blk * 128:256 + (blk + 1) * 128] = acc.astype(o_ref.dtype)

    mv = v_ref[:, 384:V_COLS]
    acc = jnp.zeros((tq, 256), F32)
    for hd in range(MLA_HEADS):
        mq = q_ref[:, 512 + hd * MLA_BLK:512 + (hd + 1) * MLA_BLK]
        mk = k_ref[:, 384 + hd * MLA_BLK:384 + (hd + 1) * MLA_BLK]
        p = _softmax(_dot_nt(mq, mk))
        acc = acc + _dot(p.astype(BF16), mv * _lane_mask(256, hd * MLA_V, (hd + 1) * MLA_V))
    o_ref[:, 512:O_COLS] = acc.astype(o_ref.dtype)


def _attention(q, k, v, lam_params, subg, lam_init, n_batch, tq_total, tk):
    tq = TOKEN_TILE
    nq = tq_total // tq
    small = _const_spec((1, DIFF_QK))
    return pl.pallas_call(
        functools.partial(_attn_kernel, lam_init=lam_init),
        grid=(n_batch * nq,),
        in_specs=[
            pl.BlockSpec((tq, Q_COLS), lambda i: (i, 0)),
            pl.BlockSpec((None, tk, K_COLS), lambda i: (i // nq, 0, 0)),
            pl.BlockSpec((None, tk, V_COLS), lambda i: (i // nq, 0, 0)),
            small, small, small, small,
            _const_spec((1, 256)),
        ],
        out_specs=pl.BlockSpec((tq, O_COLS), lambda i: (i, 0)),
        out_shape=jax.ShapeDtypeStruct((n_batch * tq_total, O_COLS), BF16),
        compiler_params=pltpu.CompilerParams(vmem_limit_bytes=VMEM_LIMIT),
        name="attention_tk%d" % tk,
    )(q, k, v, *lam_params, subg)


def _gelu_tanh(x):
    return 0.5 * x * (1.0 + jnp.tanh(math.sqrt(2.0 / math.pi) * (x + 0.044715 * (x * x * x))))


def _ssm_kernel(*refs, n_batch, reverse, finalize, emit_state):
    (u_ref, h0_ref, are_ref, aim_ref, ldt_ref, bre_ref, bim_ref, cre_ref, cim_ref) = refs[:9]
    pos = 9
    if finalize:
        yb_ref, d_ref, wglu_ref = refs[pos:pos + 3]
        pos += 3
    out_ref = refs[pos]
    pos += 1
    if emit_state:
        hfin_ref = refs[pos]
        pos += 1
    bbar_scr, hst_scr, bu_scr = refs[pos:]

    step = pl.program_id(0)
    rows = u_ref.shape[0]
    a_re = are_ref[...]
    a_im = aim_ref[...]
    dt = jnp.exp(ldt_ref[...])
    mag = jnp.exp(a_re * dt)
    abr = mag * jnp.cos(a_im * dt)
    abi = mag * jnp.sin(a_im * dt)

    @pl.when(step == 0)
    def _():
        den = a_re * a_re + a_im * a_im
        xr = abr - 1.0
        cr = (xr * a_re + abi * a_im) / den
        ci = (abi * a_re - xr * a_im) / den
        bre = bre_ref[...]
        bim = bim_ref[...]
        bbar_scr[:, 0:SSM_LANES] = (cr * bre - ci * bim).astype(BF16)
        bbar_scr[:, SSM_LANES:2 * SSM_LANES] = (cr * bim + ci * bre).astype(BF16)
        hst_scr[...] = h0_ref[...]

    u = u_ref[...]
    bu_scr[...] = _dot(u.astype(BF16), bbar_scr[...])

    lw = 256
    if n_batch % 8 == 0:
        n_steps = rows // n_batch
        for j in range(SSM_LANES // lw):
            sre = slice(j * lw, (j + 1) * lw)
            sim = slice(SSM_LANES + j * lw, SSM_LANES + (j + 1) * lw)
            ar = jnp.broadcast_to(abr[:, sre], (n_batch, lw))
            ai = jnp.broadcast_to(abi[:, sre], (n_batch, lw))

            def body(i, carry, sre=sre, sim=sim, ar=ar, ai=ai):
                hr, hi = carry
                t = (n_steps - 1 - i) if reverse else i
                r0 = pl.multiple_of(t * n_batch, n_batch)
                nr = ar * hr - ai * hi + bu_scr[pl.ds(r0, n_batch), sre]
                ni = ar * hi + ai * hr + bu_scr[pl.ds(r0, n_batch), sim]
                bu_scr[pl.ds(r0, n_batch), sre] = nr
                bu_scr[pl.ds(r0, n_batch), sim] = ni
                return nr, ni

            hr, hi = lax.fori_loop(0, n_steps, body, (hst_scr[:, sre], hst_scr[:, sim]), unroll=2)
            hst_scr[:, sre] = hr
            hst_scr[:, sim] = hi
    else:
        assert n_batch == 4
        n_blk = rows // 8
        low = lax.broadcasted_iota(jnp.int32, (8, lw), 0) < 4
        first = jnp.logical_not(low) if reverse else low
        for j in range(SSM_LANES // lw):
            sre = slice(j * lw, (j + 1) * lw)
            sim = slice(SSM_LANES + j * lw, SSM_LANES + (j + 1) * lw)
            ar = jnp.broadcast_to(abr[:, sre], (8, lw))
            ai = jnp.broadcast_to(abi[:, sre], (8, lw))
            a2r = ar * ar - ai * ai
            a2i = 2.0 * ar * ai
            cr = jnp.where(first, ar, a2r)
            ci = jnp.where(first, ai, a2i)

            def body(i, carry, sre=sre, sim=sim, ar=ar, ai=ai, cr=cr, ci=ci):
                pr, pi = carry
                blk = (n_blk - 1 - i) if reverse else i
                r0 = pl.multiple_of(blk * 8, 8)
                vr = bu_scr[pl.ds(r0, 8), sre]
                vi = bu_scr[pl.ds(r0, 8), sim]
                sr = jnp.where(first, 0.0, pltpu.roll(vr, 4, axis=0))
                si = jnp.where(first, 0.0, pltpu.roll(vi, 4, axis=0))
                wr = vr + (ar * sr - ai * si)
                wi = vi + (ar * si + ai * sr)
                nr = cr * pr - ci * pi + wr
                ni = cr * pi + ci * pr + wi
                bu_scr[pl.ds(r0, 8), sre] = nr
                bu_scr[pl.ds(r0, 8), sim] = ni
                return (jnp.where(first, pltpu.roll(nr, 4, axis=0), nr),
                        jnp.where(first, pltpu.roll(ni, 4, axis=0), ni))

            pr, pi = lax.fori_loop(0, n_blk, body, (hst_scr[:, sre], hst_scr[:, sim]), unroll=2)
            hst_scr[:, sre] = pr
            hst_scr[:, sim] = pi

    y = (_dot(bu_scr[:, 0:SSM_LANES].astype(BF16), cre_ref[...])
         - _dot(bu_scr[:, SSM_LANES:2 * SSM_LANES].astype(BF16), cim_ref[...]))
    if finalize:
        yt = y + yb_ref[...] + u * d_ref[...]
        z = _dot(_gelu_tanh(yt).astype(BF16), wglu_ref[...])
        out_ref[...] = (z[:, 0:SSM_WIDTH] * _sigmoid(z[:, SSM_WIDTH:2 * SSM_WIDTH])).astype(out_ref.dtype)
    else:
        out_ref[...] = y

    if emit_state:
        @pl.when(step == pl.num_programs(0) - 1)
        def _():
            hfin_ref[...] = hst_scr[...]


def _ssm_direction(u_tb, h0, params, direction, n_batch, rows_per_step, reverse, y_other=None, tail=None,
                   emit_state=False):
    n_rows = u_tb.shape[0]
    n_steps = n_rows // rows_per_step
    finalize = y_other is not None
    chunk = (lambda k: (n_steps - 1 - k, 0)) if reverse else (lambda k: (k, 0))
    h_rows = h0.shape[0]
    in_specs = [
        pl.BlockSpec((rows_per_step, SSM_WIDTH), chunk),
        _const_spec((h_rows, 2 * SSM_LANES)),
        _const_spec((1, SSM_LANES)), _const_spec((1, SSM_LANES)), _const_spec((1, SSM_LANES)),
        _const_spec((SSM_WIDTH, SSM_LANES)), _const_spec((SSM_WIDTH, SSM_LANES)),
        _const_spec((SSM_LANES, SSM_WIDTH)), _const_spec((SSM_LANES, SSM_WIDTH)),
    ]
    args = [u_tb, h0, params["a_re"][direction], params["a_im"][direction], params["log_dt"][direction],
            params["b_re"][direction], params["b_im"][direction], params["c_re"][direction],
            params["c_im"][direction]]
    if finalize:
        in_specs += [pl.BlockSpec((rows_per_step, SSM_WIDTH), chunk), _const_spec((1, SSM_WIDTH)),
                     _const_spec((SSM_WIDTH, 2 * SSM_WIDTH))]
        args += [y_other, tail["d"], tail["w_glu"]]
    out_shape = [jax.ShapeDtypeStruct((n_rows, SSM_WIDTH), BF16 if finalize else F32)]
    out_specs = [pl.BlockSpec((rows_per_step, SSM_WIDTH), chunk)]
    if emit_state:
        out_shape.append(jax.ShapeDtypeStruct((h_rows, 2 * SSM_LANES), F32))
        out_specs.append(_const_spec((h_rows, 2 * SSM_LANES)))
    res = pl.pallas_call(
        functools.partial(_ssm_kernel, n_batch=n_batch, reverse=reverse, finalize=finalize,
                          emit_state=emit_state),
        grid=(n_steps,),
        in_specs=in_specs,
        out_specs=out_specs,
        out_shape=out_shape,
        scratch_shapes=[pltpu.VMEM((SSM_WIDTH, 2 * SSM_LANES), BF16),
                        pltpu.VMEM((h_rows, 2 * SSM_LANES), F32),
                        pltpu.VMEM((rows_per_step, 2 * SSM_LANES), F32)],
        compiler_params=pltpu.CompilerParams(vmem_limit_bytes=VMEM_LIMIT),
        name="ssm_%s_b%d" % ("bwd" if reverse else "fwd", n_batch),
    )(*args)
    return res


def _out_mlp_kernel(x_ref, oa_ref, os_ref, mod_ref, woa_ref, wos_ref, n2g_ref, w1_ref, w2_ref, fng_ref, y_ref,
                    *, final):
    g1 = mod_ref[:, 2 * D_MODEL:3 * D_MODEL]
    sh2 = mod_ref[:, 3 * D_MODEL:4 * D_MODEL]
    sc2 = mod_ref[:, 4 * D_MODEL:5 * D_MODEL]
    g2 = mod_ref[:, 5 * D_MODEL:6 * D_MODEL]
    mix = _dot(oa_ref[...], woa_ref[...]) + _dot(os_ref[...], wos_ref[...])
    x1 = x_ref[...] + g1 * mix
    ms = jnp.mean(x1 * x1, axis=-1, keepdims=True)
    h = x1 * lax.rsqrt(ms + EPS) * n2g_ref[...]
    h = h * (1.0 + sc2) + sh2
    a = jnp.maximum(_dot(h.astype(BF16), w1_ref[...]), 0.0)
    x2 = x1 + g2 * _dot((a * a).astype(BF16), w2_ref[...])
    if final:
        ms2 = jnp.mean(x2 * x2, axis=-1, keepdims=True)
        x2 = x2 * lax.rsqrt(ms2 + EPS) * fng_ref[...]
    y_ref[...] = x2


def _resident_spec(shape):
    nd = len(shape)
    return pl.BlockSpec(shape, lambda i: (0,) * nd, pipeline_mode=pl.Buffered(1))


def _out_mlp(x2d, o_attn, o_ssm_tb, mod3, layer, weights, n_batch, seq, latent, final):
    tm = TOKEN_TILE
    tpb = seq // tm
    n_tok = n_batch * seq
    if latent:
        mod_row = lambda i: (layer * MOD_ROWS + 1 + i // tpb, 0, 0)
    else:
        mod_row = lambda i: (layer * MOD_ROWS, 0, 0)
    tile = lambda i: (i, 0)
    return pl.pallas_call(
        functools.partial(_out_mlp_kernel, final=final),
        grid=(n_tok // tm,),
        in_specs=[
            pl.BlockSpec((tm, D_MODEL), tile),
            pl.BlockSpec((tm, O_COLS), tile),
            pl.BlockSpec((tm, SSM_WIDTH), lambda i: (i % tpb, i // tpb)),
            pl.BlockSpec((None, 1, N_MOD * D_MODEL), mod_row),
            _resident_spec((O_COLS, D_MODEL)),
            _resident_spec((SSM_WIDTH, D_MODEL)),
            _const_spec((1, D_MODEL)),
            _resident_spec((D_MODEL, D_FF)),
            _resident_spec((D_FF, D_MODEL)),
            _const_spec((1, D_MODEL)),
        ],
        out_specs=pl.BlockSpec((tm, D_MODEL), tile),
        out_shape=jax.ShapeDtypeStruct((n_tok, D_MODEL), F32),
        compiler_params=pltpu.CompilerParams(vmem_limit_bytes=VMEM_LIMIT),
        name="out_mlp_%s%s" % ("lat" if latent else "ctx", "_final" if final else ""),
    )(x2d, o_attn, o_ssm_tb, mod3, weights["w_out_attn"], weights["w_out_ssm"], weights["n2g"],
      weights["w1"], weights["w2"], weights["fng"])


def _in_proj_columns():
    idx = np.full((P_END,), IN_COLS, np.int32)
    idx[P_DQ:P_DQ + 768] = np.arange(0, 768)
    gq0 = 768
    for slot, hd in enumerate((0, 2, 1, 3)):
        idx[P_GQ + slot * 64:P_GQ + (slot + 1) * 64] = gq0 + hd * 64 + np.arange(64)
    idx[P_GK:P_GK + 512] = np.arange(1024, 1536)
    idx[P_CKV:P_CKV + 128] = np.arange(1728, 1856)
    idx[P_CQ:P_CQ + MLA_Q_RANK] = np.arange(1536, 1728)
    idx[P_KR + MLA_NOPE:P_KR + MLA_NOPE + MLA_ROPE] = np.arange(1856, 1888)
    return idx


def _uq_columns():
    idx = np.full((MLA_HEADS * MLA_BLK,), MLA_HEADS * (MLA_NOPE + MLA_ROPE), np.int32)
    for hd in range(MLA_HEADS):
        idx[hd * MLA_BLK:hd * MLA_BLK + MLA_NOPE + MLA_ROPE] = hd * (MLA_NOPE + MLA_ROPE) + np.arange(
            MLA_NOPE + MLA_ROPE)
    return idx


def _ukv_columns():
    n_src = MLA_HEADS * (MLA_NOPE + MLA_V)
    idx = np.full((MLA_HEADS * MLA_BLK + 256,), n_src, np.int32)
    for hd in range(MLA_HEADS):
        idx[hd * MLA_BLK:hd * MLA_BLK + MLA_NOPE] = hd * (MLA_NOPE + MLA_V) + np.arange(MLA_NOPE)
        idx[MLA_HEADS * MLA_BLK + hd * MLA_V:MLA_HEADS * MLA_BLK + (hd + 1) * MLA_V] = (
            hd * (MLA_NOPE + MLA_V) + MLA_NOPE + np.arange(MLA_V))
    return idx


def _take_cols(w, idx):
    wz = jnp.concatenate([w, jnp.zeros(w.shape[:-1] + (1,), w.dtype)], axis=-1)
    return jnp.take(wz, jnp.asarray(idx), axis=-1)


def _rope_tables(t_len, chunk, n_chunks):
    n = chunk // 4
    rows = t_len // GRID_W
    row = jnp.repeat(jnp.arange(rows), GRID_W).astype(F32)
    col = jnp.tile(jnp.arange(GRID_W), rows).astype(F32)
    freq = ROPE_THETA ** (-jnp.arange(n, dtype=F32) / n)
    ang_r = row[:, None] * freq
    ang_c = col[:, None] * freq
    cr, sr, cc, sc = jnp.cos(ang_r), jnp.sin(ang_r), jnp.cos(ang_c), jnp.sin(ang_c)
    z = jnp.zeros_like(cr)
    tabs = jnp.stack([jnp.concatenate([cr, cr, cc, cc], axis=-1),
                      jnp.concatenate([-sr, z, -sc, z], axis=-1),
                      jnp.concatenate([z, sr, z, sc], axis=-1)])
    return jnp.tile(tabs, (1, 1, n_chunks))


def _mla_rope_tables(t_len):
    tabs = _rope_tables(t_len, MLA_ROPE, 1)
    ident = jnp.stack([jnp.ones((t_len, MLA_NOPE), F32), jnp.zeros((t_len, MLA_NOPE), F32),
                       jnp.zeros((t_len, MLA_NOPE), F32)])
    tail = ident[:, :, :MLA_BLK - MLA_NOPE - MLA_ROPE]
    return jnp.concatenate([ident, tabs, tail], axis=-1)


def _block_diag(blocks):
    g, r, c = blocks.shape
    eye = jnp.eye(g, dtype=blocks.dtype)
    return (blocks[:, :, None, :] * eye[:, None, :, None]).reshape(g * r, g * c)


def _layer_weights(p, l):
    w = {}
    w["n1g"] = p["norm1_g"][l].reshape(1, D_MODEL)
    w["n2g"] = p["norm2_g"][l].reshape(1, D_MODEL)
    w["fng"] = p["final_norm_g"].reshape(1, D_MODEL)
    w["w_in"] = _take_cols(p["w_in"][l], _in_proj_columns()).astype(BF16)
    w["qng"] = jnp.tile(p["gqa_qn_g"][l], GQA_HEADS).reshape(1, 256)
    w["kng"] = jnp.tile(p["gqa_kn_g"][l], GQA_KV_HEADS).reshape(1, 128)
    w["mqng"] = jnp.pad(p["mla_qn_g"][l], (0, CQ_PAD - MLA_Q_RANK)).reshape(1, CQ_PAD)
    w["kvng"] = p["mla_kvn_g"][l].reshape(1, MLA_KV_RANK)
    w["w_uq"] = jnp.pad(_take_cols(p["mla_w_uq"][l], _uq_columns()),
                        ((0, CQ_PAD - MLA_Q_RANK), (0, 0))).astype(BF16)
    w["w_ukv"] = _take_cols(p["mla_w_ukv"][l], _ukv_columns()).astype(BF16)
    w_out = p["w_out"][l]
    gqa_rows = np.concatenate([256 + hd * 64 + np.arange(64) for hd in (0, 2, 1, 3)])
    rows = np.concatenate([np.arange(0, 256), gqa_rows, np.arange(768, 1024)])
    w["w_out_attn"] = jnp.take(w_out, jnp.asarray(rows), axis=0).astype(BF16)
    w["w_out_ssm"] = w_out[512:768].astype(BF16)
    w["w1"] = p["mlp_w1"][l].astype(BF16)
    w["w2"] = p["mlp_w2"][l].astype(BF16)
    w["lam"] = tuple(p[n][l].reshape(1, DIFF_QK) for n in ("diff_lq1", "diff_lk1", "diff_lq2", "diff_lk2"))
    w["subg"] = jnp.tile(p["diff_subln_g"][l], DIFF_HEADS).reshape(1, 256)
    ssm = {
        "a_re": p["ssm_a_re"][l].reshape(2, 1, SSM_LANES),
        "a_im": p["ssm_a_im"][l].reshape(2, 1, SSM_LANES),
        "log_dt": jnp.repeat(p["ssm_log_dt"][l], SSM_STATE, axis=-1).reshape(2, 1, SSM_LANES),
        "b_re": jnp.stack([_block_diag(jnp.swapaxes(p["ssm_b_re"][l, d], 1, 2)) for d in range(2)]),
        "b_im": jnp.stack([_block_diag(jnp.swapaxes(p["ssm_b_im"][l, d], 1, 2)) for d in range(2)]),
        "c_re": jnp.stack([_block_diag(jnp.swapaxes(p["ssm_c_re"][l, d], 1, 2)) for d in range(2)]).astype(BF16),
        "c_im": jnp.stack([_block_diag(jnp.swapaxes(p["ssm_c_im"][l, d], 1, 2)) for d in range(2)]).astype(BF16),
    }
    w["ssm"] = ssm
    w["ssm_tail"] = {"d": p["ssm_d"][l].reshape(1, SSM_WIDTH), "w_glu": p["ssm_w_glu"][l].astype(BF16)}
    return w


def _ssm_mixer(u_tb, h0_f, h0_b, w, n_batch, rows_per_step, emit_state):
    bwd = _ssm_direction(u_tb, h0_b, w["ssm"], 1, n_batch, rows_per_step, reverse=True,
                         emit_state=emit_state)
    fwd = _ssm_direction(u_tb, h0_f, w["ssm"], 0, n_batch, rows_per_step, reverse=False,
                         y_other=bwd[0], tail=w["ssm_tail"], emit_state=emit_state)
    return fwd, bwd


def kernel(x_prompt, x_sample, cache_diff_k, cache_diff_v, cache_gqa_k, cache_gqa_v, cache_mla_ckv, cache_mla_krope, state_ssm_re, state_ssm_im, c, c_ctx, norm1_g, norm2_g, w_ada, b_ada, w_in, w_out, diff_lq1, diff_lk1, diff_lq2, diff_lk2, diff_subln_g, gqa_qn_g, gqa_kn_g, ssm_a_re, ssm_a_im, ssm_log_dt, ssm_b_re, ssm_b_im, ssm_c_re, ssm_c_im, ssm_d, ssm_w_glu, mla_qn_g, mla_kvn_g, mla_w_uq, mla_w_ukv, mlp_w1, mlp_w2, final_norm_g):
    p = dict(norm1_g=norm1_g, norm2_g=norm2_g, w_in=w_in, w_out=w_out, diff_lq1=diff_lq1, diff_lk1=diff_lk1,
             diff_lq2=diff_lq2, diff_lk2=diff_lk2, diff_subln_g=diff_subln_g, gqa_qn_g=gqa_qn_g,
             gqa_kn_g=gqa_kn_g, ssm_a_re=ssm_a_re, ssm_a_im=ssm_a_im, ssm_log_dt=ssm_log_dt, ssm_b_re=ssm_b_re,
             ssm_b_im=ssm_b_im, ssm_c_re=ssm_c_re, ssm_c_im=ssm_c_im, ssm_d=ssm_d, ssm_w_glu=ssm_w_glu,
             mla_qn_g=mla_qn_g, mla_kvn_g=mla_kvn_g, mla_w_uq=mla_w_uq, mla_w_ukv=mla_w_ukv, mlp_w1=mlp_w1,
             mlp_w2=mlp_w2, final_norm_g=final_norm_g)
    n_layers = w_in.shape[0]
    n_ctx, seq, _ = x_prompt.shape
    n_lat, t_lat, _ = x_sample.shape
    past = cache_diff_k.shape[2]
    assert n_lat + 1 <= MOD_ROWS and n_lat == 4 and n_ctx % 8 == 0
    assert seq == TOKEN_TILE and t_lat % TOKEN_TILE == 0

    cond = jnp.concatenate([c_ctx[None], c, jnp.zeros((MOD_ROWS - 1 - n_lat, D_MODEL), F32)], axis=0)
    mod3 = _adaln(cond, w_ada, b_ada).reshape(n_layers * MOD_ROWS, 1, N_MOD * D_MODEL)

    rope_tabs = (_rope_tables(t_lat, DIFF_QK, 256 // DIFF_QK),
                 _rope_tables(t_lat, GQA_HEAD_DIM, 256 // GQA_HEAD_DIM),
                 _mla_rope_tables(t_lat))
    caches = (cache_diff_k.reshape(n_lat, n_layers, past, 256),
              cache_diff_v.reshape(n_lat, n_layers, past, 256),
              cache_gqa_k.reshape(n_lat, n_layers, past, 128),
              cache_gqa_v.reshape(n_lat, n_layers, past, 128),
              cache_mla_ckv,
              jnp.pad(cache_mla_krope, ((0, 0), (0, 0), (0, 0), (MLA_NOPE, MLA_BLK - MLA_NOPE - MLA_ROPE))))

    lam_inits = [0.8 - 0.6 * math.exp(-0.3 * l) for l in range(n_layers)]
    weights = [_layer_weights(p, l) for l in range(n_layers)]

    x = x_prompt.reshape(n_ctx * seq, D_MODEL)
    ctx_rows = 32 * n_ctx
    zero_state = jnp.zeros((n_ctx, 2 * SSM_LANES), F32)
    ctx_out = []
    for l in range(n_layers):
        w = weights[l]
        q, k, v, u, cdk, cdv, cgk, cgv, cckv, ckr = _in_proj(x, mod3, l, w, n_ctx, seq, latent=False)
        o_attn = _attention(q, k.reshape(n_ctx, seq, K_COLS), v.reshape(n_ctx, seq, V_COLS), w["lam"], w["subg"],
                            lam_inits[l], n_ctx, seq, seq)
        fwd, bwd = _ssm_mixer(u.reshape(seq * n_ctx, SSM_WIDTH), zero_state, zero_state, w, n_ctx, ctx_rows,
                              emit_state=True)
        o_ssm = fwd[0].reshape(seq, n_ctx * SSM_WIDTH)
        x = _out_mlp(x, o_attn, o_ssm, mod3, l, w, n_ctx, seq, latent=False, final=(l == n_layers - 1))
        h_fin = jnp.stack([fwd[1], bwd[1]], axis=1)
        ctx_out.append((cdk, cdv, cgk, cgv, cckv, ckr,
                        h_fin[:, :, :SSM_LANES].reshape(n_ctx, 2, SSM_GROUPS, SSM_STATE),
                        h_fin[:, :, SSM_LANES:].reshape(n_ctx, 2, SSM_GROUPS, SSM_STATE)))
    y_prompt = x.reshape(n_ctx, seq, D_MODEL)

    x = x_sample.reshape(n_lat * t_lat, D_MODEL)
    lat_rows = 256 * n_lat
    for l in range(n_layers):
        w = weights[l]
        q, k, v, u = _in_proj(x, mod3, l, w, n_lat, t_lat, latent=True, rope_tabs=rope_tabs)
        kc, vc = _cache_prep(caches, l, w["w_ukv"])
        k_all = jnp.concatenate([kc, k.reshape(n_lat, t_lat, K_COLS)], axis=1)
        v_all = jnp.concatenate([vc, v.reshape(n_lat, t_lat, V_COLS)], axis=1)
        o_attn = _attention(q, k_all, v_all, w["lam"], w["subg"], lam_inits[l], n_lat, t_lat, past + t_lat)
        h0 = jnp.concatenate([state_ssm_re[:, l].reshape(n_lat, 2, SSM_LANES),
                              state_ssm_im[:, l].reshape(n_lat, 2, SSM_LANES)], axis=-1)
        h0_f = jnp.concatenate([h0[:, 0], h0[:, 0]], axis=0)
        h0_b = jnp.concatenate([h0[:, 1], h0[:, 1]], axis=0)
        fwd, _ = _ssm_mixer(u.reshape(t_lat * n_lat, SSM_WIDTH), h0_f, h0_b, w, n_lat, lat_rows,
                            emit_state=False)
        o_ssm = fwd[0].reshape(t_lat, n_lat * SSM_WIDTH)
        x = _out_mlp(x, o_attn, o_ssm, mod3, l, w, n_lat, t_lat, latent=True, final=(l == n_layers - 1))
    y_sample = x.reshape(n_lat, t_lat, D_MODEL)

    def stacked(i, tail_shape):
        return jnp.stack([s[i].reshape((n_ctx, seq) + tail_shape) for s in ctx_out], axis=1)

    return (y_prompt, y_sample,
            stacked(0, (DIFF_HEADS, 2 * DIFF_QK)), stacked(1, (DIFF_HEADS, DIFF_V)),
            stacked(2, (GQA_KV_HEADS, GQA_HEAD_DIM)), stacked(3, (GQA_KV_HEADS, GQA_HEAD_DIM)),
            stacked(4, (MLA_KV_RANK,)), stacked(5, (MLA_ROPE,)),
            jnp.stack([s[6] for s in ctx_out], axis=1), jnp.stack([s[7] for s in ctx_out], axis=1))
```

```python
import functools
import math

import numpy as np
import jax
import jax.numpy as jnp
from jax import lax
from jax.experimental import pallas as pl
from jax.experimental.pallas import tpu as pltpu

F32 = jnp.float32
BF16 = jnp.bfloat16

D_MODEL = 1024
GRID_W = 64
ROPE_THETA = 10000.0
EPS = 1e-6
DIFF_HEADS, DIFF_QK, DIFF_V = 4, 32, 64
GQA_HEADS, GQA_KV_HEADS, GQA_HEAD_DIM = 4, 2, 64
SSM_WIDTH, SSM_GROUP, SSM_STATE = 256, 16, 64
SSM_GROUPS = SSM_WIDTH // SSM_GROUP
SSM_LANES = SSM_GROUPS * SSM_STATE
MLA_HEADS, MLA_Q_RANK, MLA_KV_RANK, MLA_NOPE, MLA_ROPE, MLA_V = 4, 192, 128, 64, 32, 64
MLA_QK = MLA_NOPE + MLA_ROPE
D_FF = 4 * D_MODEL
N_MOD = 6
IN_COLS = 1888
MOD_ROWS = 8

P_DQ, P_DK, P_DV, P_GQ, P_GK, P_GV, P_U, P_CQ, P_CKV, P_KR, P_END = (
    0, 256, 512, 768, 1024, 1152, 1280, 1536, 1792, 1920, 2048)
S_CQ_END, S_CKV, S_KR = 1728, 1728, 1856
CQ_PAD = P_CKV - P_CQ
MLA_BLK = 128
Q_DIFF, Q_GQA, Q_MLA, Q_COLS = 0, 256, 512, 1024
K_DIFF, K_GQA, K_MLA, K_COLS = 0, 256, 512, 1024
V_DIFF, V_GQA, V_MLA, V_COLS = 0, 256, 512, 1024
O_COLS = 768

VEC_N1G, VEC_N2G, VEC_FNG, VEC_QNG, VEC_KNG, VEC_MQNG, VEC_KVNG, VEC_SUBG, VEC_SSMD, VEC_LAM, VEC_END = (
    0, 1024, 2048, 3072, 3328, 3456, 3712, 3840, 4096, 4352, 4480)

TOKEN_TILE = 256
VMEM_LIMIT = 48 * 1024 * 1024


def _dot(a, b):
    return jnp.dot(a, b, preferred_element_type=F32)


def _dot_nt(a, b):
    return lax.dot_general(a, b, (((1,), (1,)), ((), ())), preferred_element_type=F32)


def _block_ones(width, seg):
    shift = seg.bit_length() - 1
    r = jnp.right_shift(lax.broadcasted_iota(jnp.int32, (width, width), 0), shift)
    c = jnp.right_shift(lax.broadcasted_iota(jnp.int32, (width, width), 1), shift)
    return jnp.where(r == c, 1.0, 0.0).astype(BF16)


def _seg_sum_sq(y, seg):
    ones = _block_ones(y.shape[-1], seg)
    sq = y * y
    hi = sq.astype(BF16)
    lo = (sq - hi.astype(F32)).astype(BF16)
    return _dot(hi, ones) + _dot(lo, ones)


def _rope(x, tab_ref, quarter):
    outs = []
    for j in range(x.shape[-1] // 128):
        sl = slice(j * 128, (j + 1) * 128)
        xs = x[:, sl]
        up = pltpu.roll(xs, 128 - quarter, axis=1)
        dn = pltpu.roll(xs, quarter, axis=1)
        outs.append(xs * tab_ref[0, :, sl] + up * tab_ref[1, :, sl] + dn * tab_ref[2, :, sl])
    return outs[0] if len(outs) == 1 else jnp.concatenate(outs, axis=-1)


def _sigmoid(x):
    return 1.0 / (1.0 + jnp.exp(-x))


def _lane_lt(width, bound):
    return lax.broadcasted_iota(jnp.int32, (1, width), 1) < bound


def _dup_halves(x):
    swapped = pltpu.roll(x, 64, axis=1)
    left = _lane_lt(128, 64)
    return jnp.where(left, x, swapped), jnp.where(left, swapped, x)


def _const_spec(shape):
    nd = len(shape)
    return pl.BlockSpec(shape, lambda *_: (0,) * nd)


def _layer_spec(shape, layer, single_buffer=False):
    nd = len(shape)
    kw = {"pipeline_mode": pl.Buffered(1)} if single_buffer else {}
    return pl.BlockSpec((None,) + tuple(shape), lambda *_: (layer,) + (0,) * nd, **kw)


def _adaln_kernel(cond_ref, w_ref, b_ref, o_ref):
    c = cond_ref[...]
    s = c * _sigmoid(c)
    o_ref[...] = _dot(s.astype(BF16), w_ref[...].astype(BF16)) + b_ref[...]


def _adaln(cond, w_ada, b_ada):
    n_layers = w_ada.shape[0]
    tn = 1024
    return pl.pallas_call(
        _adaln_kernel,
        grid=(n_layers, N_MOD * D_MODEL // tn),
        in_specs=[
            pl.BlockSpec((MOD_ROWS, D_MODEL), lambda l, j: (0, 0)),
            pl.BlockSpec((None, D_MODEL, tn), lambda l, j: (l, 0, j)),
            pl.BlockSpec((None, 1, tn), lambda l, j: (l, 0, j)),
        ],
        out_specs=pl.BlockSpec((None, MOD_ROWS, tn), lambda l, j: (l, 0, j)),
        out_shape=jax.ShapeDtypeStruct((n_layers, MOD_ROWS, N_MOD * D_MODEL), F32),
        compiler_params=pltpu.CompilerParams(vmem_limit_bytes=VMEM_LIMIT),
        name="adaln",
    )(cond, w_ada, b_ada.reshape(n_layers, 1, N_MOD * D_MODEL))


N_CACHE = 6


def _in_proj_kernel(*refs, latent, n_alias):
    x_ref, mod_ref, vec_ref, wint_ref, wuq_ref, wukv_ref = refs[:6]
    pos = 6
    if latent:
        rope_d_ref, rope_g_ref, rope_m_ref = refs[pos:pos + 3]
        pos += 3
    pos += n_alias
    q_ref, k_ref, v_ref, u_ref = refs[pos:pos + 4]
    pos += 4
    if not latent:
        cdk_ref, cdv_ref, cgk_ref, cgv_ref, cckv_ref, ckr_ref = refs[pos:pos + N_CACHE]
        pos += N_CACHE
    win_scr, wuq_scr, wukv_scr = refs[pos:]

    @pl.when(pl.program_id(0) == 0)
    def _():
        win_scr[0:S_CQ_END, :] = wint_ref[0:S_CQ_END, :].astype(BF16)
        win_scr[S_CQ_END:P_CKV, :] = jnp.zeros((P_CKV - S_CQ_END, D_MODEL), BF16)
        win_scr[P_CKV:P_KR, :] = wint_ref[S_CKV:S_KR, :].astype(BF16)
        win_scr[P_KR:P_KR + MLA_NOPE, :] = jnp.zeros((MLA_NOPE, D_MODEL), BF16)
        win_scr[P_KR + MLA_NOPE:P_KR + MLA_QK, :] = wint_ref[S_KR:IN_COLS, :].astype(BF16)
        win_scr[P_KR + MLA_QK:P_END, :] = jnp.zeros((P_END - P_KR - MLA_QK, D_MODEL), BF16)
        wuq_scr[...] = jnp.zeros(wuq_scr.shape, BF16)
        wuq = wuq_ref[...]
        for hd in range(MLA_HEADS):
            wuq_scr[0:MLA_Q_RANK, hd * MLA_BLK:hd * MLA_BLK + MLA_QK] = (
                wuq[:, hd * MLA_QK:(hd + 1) * MLA_QK].astype(BF16))
        wukv_scr[...] = wukv_ref[...].astype(BF16)

    x = x_ref[...]
    sh1 = mod_ref[:, 0:D_MODEL]
    sc1 = mod_ref[:, D_MODEL:2 * D_MODEL]
    ms = jnp.mean(x * x, axis=-1, keepdims=True)
    h = x * lax.rsqrt(ms + EPS) * vec_ref[:, VEC_N1G:VEC_N1G + D_MODEL]
    h = h * (1.0 + sc1) + sh1
    proj = _dot_nt(h.astype(BF16), win_scr[...])

    dq = proj[:, P_DQ:P_DK]
    dk = proj[:, P_DK:P_DV]
    dv = proj[:, P_DV:P_GQ]
    gq = proj[:, P_GQ:P_GK]
    gk = proj[:, P_GK:P_GV]
    gv = proj[:, P_GV:P_U]
    u_ref[...] = proj[:, P_U:P_CQ]
    cq = proj[:, P_CQ:P_CKV]
    ckv = proj[:, P_CKV:P_KR]
    krb = proj[:, P_KR:P_END]

    qng = vec_ref[:, VEC_QNG:VEC_QNG + 256]
    kng = vec_ref[:, VEC_KNG:VEC_KNG + 128]
    gq = gq * lax.rsqrt(_seg_sum_sq(gq, GQA_HEAD_DIM) * (1.0 / GQA_HEAD_DIM) + EPS) * qng
    gk = gk * lax.rsqrt(_seg_sum_sq(gk, GQA_HEAD_DIM) * (1.0 / GQA_HEAD_DIM) + EPS) * kng
    ckv_n = (ckv * lax.rsqrt(jnp.mean(ckv * ckv, axis=-1, keepdims=True) + EPS)
             * vec_ref[:, VEC_KVNG:VEC_KVNG + MLA_KV_RANK])
    cq_ms = jnp.sum(cq * cq, axis=-1, keepdims=True) * (1.0 / MLA_Q_RANK)
    cq_n = cq * lax.rsqrt(cq_ms + EPS) * vec_ref[:, VEC_MQNG:VEC_MQNG + CQ_PAD]
    mq = _dot(cq_n.astype(BF16), wuq_scr[...])
    kv = _dot(ckv_n.astype(BF16), wukv_scr[...])

    if not latent:
        cdk_ref[...] = dk.T
        cdv_ref[...] = dv.T
        cgk_ref[...] = gk.T
        cgv_ref[...] = gv.T
        cckv_ref[...] = ckv_n
        ckr_ref[...] = krb.T[MLA_NOPE:MLA_QK, :]
    else:
        dq = _rope(dq, rope_d_ref, DIFF_QK // 4)
        dk = _rope(dk, rope_d_ref, DIFF_QK // 4)
        gq = _rope(gq, rope_g_ref, GQA_HEAD_DIM // 4)
        gk = _rope(gk, rope_g_ref, GQA_HEAD_DIM // 4)
        mq = jnp.concatenate(
            [_rope(mq[:, hd * MLA_BLK:(hd + 1) * MLA_BLK], rope_m_ref, MLA_ROPE // 4)
             for hd in range(MLA_HEADS)], axis=-1)
        krb = _rope(krb, rope_m_ref, MLA_ROPE // 4)

    q_ref[:, Q_DIFF:Q_GQA] = (dq * (DIFF_QK ** -0.5)).astype(BF16)
    q_ref[:, Q_GQA:Q_MLA] = (gq * (GQA_HEAD_DIM ** -0.5)).astype(BF16)
    q_ref[:, Q_MLA:Q_COLS] = (mq * (MLA_QK ** -0.5)).astype(BF16)
    k_ref[:, K_DIFF:K_GQA] = dk.astype(BF16)
    gk0, gk1 = _dup_halves(gk)
    k_ref[:, K_GQA:K_GQA + 128] = gk0.astype(BF16)
    k_ref[:, K_GQA + 128:K_MLA] = gk1.astype(BF16)
    nope = _lane_lt(MLA_BLK, MLA_NOPE)
    for hd in range(MLA_HEADS):
        blk = slice(hd * MLA_BLK, (hd + 1) * MLA_BLK)
        k_ref[:, K_MLA + hd * MLA_BLK:K_MLA + (hd + 1) * MLA_BLK] = jnp.where(nope, kv[:, blk], krb).astype(BF16)
    v_ref[:, V_DIFF:V_GQA] = dv.astype(BF16)
    gv0, gv1 = _dup_halves(gv)
    v_ref[:, V_GQA:V_GQA + 128] = gv0.astype(BF16)
    v_ref[:, V_GQA + 128:V_MLA] = gv1.astype(BF16)
    v_ref[:, V_MLA:V_COLS] = kv.astype(BF16)


def _in_proj(x2d, mod3, layer, wts, n_batch, seq, latent, rope_tabs=None, prev_caches=None):
    tm = TOKEN_TILE
    tpb = seq // tm
    n_tok = n_batch * seq
    n_layers = wts["vecs"].shape[0]
    if latent:
        mod_row = lambda i: (layer * MOD_ROWS + 1 + i // tpb, 0, 0)
    else:
        mod_row = lambda i: (layer * MOD_ROWS, 0, 0)
    tile = lambda i: (i, 0)
    in_specs = [
        pl.BlockSpec((tm, D_MODEL), tile),
        pl.BlockSpec((None, 1, N_MOD * D_MODEL), mod_row),
        _layer_spec((1, VEC_END), layer),
        _layer_spec((IN_COLS, D_MODEL), layer, single_buffer=True),
        _layer_spec((MLA_Q_RANK, MLA_HEADS * MLA_QK), layer, single_buffer=True),
        _layer_spec((MLA_KV_RANK, MLA_HEADS * MLA_BLK), layer, single_buffer=True),
    ]
    args = [x2d, mod3, wts["vecs"], wts["w_in_t"], wts["w_uq"], wts["w_ukv"]]
    out_shape = [
        jax.ShapeDtypeStruct((n_tok, Q_COLS), BF16),
        jax.ShapeDtypeStruct((n_tok, K_COLS), BF16),
        jax.ShapeDtypeStruct((n_tok, V_COLS), BF16),
        jax.ShapeDtypeStruct((seq, n_batch * SSM_WIDTH), F32),
    ]
    out_specs = [
        pl.BlockSpec((tm, Q_COLS), tile),
        pl.BlockSpec((tm, K_COLS), tile),
        pl.BlockSpec((tm, V_COLS), tile),
        pl.BlockSpec((tm, SSM_WIDTH), lambda i: (i % tpb, i // tpb)),
    ]
    aliases = {}
    n_alias = 0
    if latent:
        pos = lambda i: (0, i % tpb, 0)
        in_specs += [pl.BlockSpec((3, tm, 256), pos), pl.BlockSpec((3, tm, 256), pos),
                     pl.BlockSpec((3, tm, 128), pos)]
        args += list(rope_tabs)
    else:
        assert tpb == 1
        cache_dims = [(256, seq), (256, seq), (128, seq), (128, seq), (seq, MLA_KV_RANK), (MLA_ROPE, seq)]
        for dims in cache_dims:
            out_shape.append(jax.ShapeDtypeStruct((n_batch, n_layers) + dims, F32))
            out_specs.append(pl.BlockSpec((None, None) + dims, lambda i: (i, layer, 0, 0)))
        if prev_caches is not None:
            n_alias = N_CACHE
            for j, arr in enumerate(prev_caches):
                aliases[len(args)] = 4 + j
                in_specs.append(pl.BlockSpec(memory_space=pl.ANY))
                args.append(arr)
    return pl.pallas_call(
        functools.partial(_in_proj_kernel, latent=latent, n_alias=n_alias),
        grid=(n_tok // tm,),
        in_specs=in_specs,
        out_specs=out_specs,
        out_shape=out_shape,
        input_output_aliases=aliases,
        scratch_shapes=[pltpu.VMEM((P_END, D_MODEL), BF16),
                        pltpu.VMEM((CQ_PAD, MLA_HEADS * MLA_BLK), BF16),
                        pltpu.VMEM((MLA_KV_RANK, MLA_HEADS * MLA_BLK), BF16)],
        compiler_params=pltpu.CompilerParams(vmem_limit_bytes=VMEM_LIMIT),
        name="in_proj_lat" if latent else "in_proj_ctx",
    )(*args)


def _cache_prep_kernel(cdk_ref, cdv_ref, cgk_ref, cgv_ref, cckv_ref, ckr_ref, wukv_ref, k_ref, v_ref):
    past = k_ref.shape[0]
    kv = _dot(cckv_ref[...].astype(BF16), wukv_ref[...].astype(BF16))
    krb = jnp.concatenate([jnp.zeros((MLA_NOPE, past), F32), ckr_ref[...],
                           jnp.zeros((MLA_BLK - MLA_QK, past), F32)], axis=0).T
    k_ref[:, K_DIFF:K_GQA] = cdk_ref[...].T.astype(BF16)
    gk0, gk1 = _dup_halves(cgk_ref[...].T)
    k_ref[:, K_GQA:K_GQA + 128] = gk0.astype(BF16)
    k_ref[:, K_GQA + 128:K_MLA] = gk1.astype(BF16)
    nope = _lane_lt(MLA_BLK, MLA_NOPE)
    for hd in range(MLA_HEADS):
        blk = slice(hd * MLA_BLK, (hd + 1) * MLA_BLK)
        k_ref[:, K_MLA + hd * MLA_BLK:K_MLA + (hd + 1) * MLA_BLK] = jnp.where(nope, kv[:, blk], krb).astype(BF16)
    v_ref[:, V_DIFF:V_GQA] = cdv_ref[...].T.astype(BF16)
    gv0, gv1 = _dup_halves(cgv_ref[...].T)
    v_ref[:, V_GQA:V_GQA + 128] = gv0.astype(BF16)
    v_ref[:, V_GQA + 128:V_MLA] = gv1.astype(BF16)
    v_ref[:, V_MLA:V_COLS] = kv.astype(BF16)


def _cache_prep(caches, layer, w_ukv):
    cdk, cdv, cgk, cgv, cckv, ckr = caches
    n_batch, _, _, past = cdk.shape

    def spec(rows, cols):
        return pl.BlockSpec((None, None, rows, cols), lambda b: (b, layer, 0, 0))

    return pl.pallas_call(
        _cache_prep_kernel,
        grid=(n_batch,),
        in_specs=[spec(256, past), spec(256, past), spec(128, past), spec(128, past),
                  spec(past, MLA_KV_RANK), spec(MLA_ROPE, past),
                  _layer_spec((MLA_KV_RANK, MLA_HEADS * MLA_BLK), layer)],
        out_specs=[pl.BlockSpec((None, past, K_COLS), lambda b: (b, 0, 0)),
                   pl.BlockSpec((None, past, V_COLS), lambda b: (b, 0, 0))],
        out_shape=[jax.ShapeDtypeStruct((n_batch, past, K_COLS), BF16),
                   jax.ShapeDtypeStruct((n_batch, past, V_COLS), BF16)],
        name="cache_prep",
    )(cdk, cdv, cgk, cgv, cckv, ckr, w_ukv)


def _softmax(s):
    m = jnp.max(s, axis=-1, keepdims=True)
    e = jnp.exp(s - m)
    return e * (1.0 / jnp.sum(e, axis=-1, keepdims=True))


def _lane_mask(width, lo, hi):
    lane = lax.broadcasted_iota(jnp.int32, (1, width), 1)
    return jnp.where((lane >= lo) & (lane < hi), 1.0, 0.0).astype(BF16)


def _attn_kernel(*refs, lam_init, n_q, past):
    if past:
        q_ref, kc_ref, ko_ref, vc_ref, vo_ref, vec_ref, o_ref, k_ref, v_ref = refs

        @pl.when(pl.program_id(0) % n_q == 0)
        def _():
            k_ref[0:past, :] = kc_ref[...]
            k_ref[past:, :] = ko_ref[...]
            v_ref[0:past, :] = vc_ref[...]
            v_ref[past:, :] = vo_ref[...]
    else:
        q_ref, k_ref, v_ref, vec_ref, o_ref = refs
    tq = q_ref.shape[0]
    lq1 = vec_ref[:, VEC_LAM:VEC_LAM + 32]
    lk1 = vec_ref[:, VEC_LAM + 32:VEC_LAM + 64]
    lq2 = vec_ref[:, VEC_LAM + 64:VEC_LAM + 96]
    lk2 = vec_ref[:, VEC_LAM + 96:VEC_LAM + 128]
    lam = (jnp.exp(jnp.sum(lq1 * lk1, axis=-1, keepdims=True))
           - jnp.exp(jnp.sum(lq2 * lk2, axis=-1, keepdims=True)) + lam_init)

    dq = q_ref[:, Q_DIFF:Q_GQA]
    dk = k_ref[:, K_DIFF:K_GQA]
    dv = v_ref[:, V_DIFF:V_GQA]
    acc = jnp.zeros((tq, 256), F32)
    for hd in range(DIFF_HEADS):
        base = hd * 2 * DIFF_QK
        p1 = _softmax(_dot_nt(dq * _lane_mask(256, base, base + DIFF_QK), dk))
        p2 = _softmax(_dot_nt(dq * _lane_mask(256, base + DIFF_QK, base + 2 * DIFF_QK), dk))
        p = p1 - lam * p2
        acc = acc + _dot(p.astype(BF16), dv * _lane_mask(256, hd * DIFF_V, (hd + 1) * DIFF_V))
    ms = _seg_sum_sq(acc, DIFF_V) * (1.0 / DIFF_V)
    subg = vec_ref[:, VEC_SUBG:VEC_SUBG + 256]
    o_ref[:, 0:256] = (acc * lax.rsqrt(ms + EPS) * subg * (1.0 - lam_init)).astype(o_ref.dtype)

    for blk in range(GQA_KV_HEADS):
        gq = q_ref[:, Q_GQA + blk * 128:Q_GQA + (blk + 1) * 128]
        gk = k_ref[:, K_GQA + blk * 128:K_GQA + (blk + 1) * 128]
        gv = v_ref[:, V_GQA + blk * 128:V_GQA + (blk + 1) * 128]
        acc = jnp.zeros((tq, 128), F32)
        for side in range(2):
            msk = _lane_mask(128, side * GQA_HEAD_DIM, (side + 1) * GQA_HEAD_DIM)
            p = _softmax(_dot_nt(gq * msk, gk))
            acc = acc + _dot(p.astype(BF16), gv * msk)
        o_ref[:, 256 + blk * 128:256 + (blk + 1) * 128] = acc.astype(o_ref.dtype)

    upper = _lane_mask(MLA_BLK, MLA_NOPE, MLA_BLK)
    for pair in range(MLA_HEADS // 2):
        outs = []
        for hd in (2 * pair, 2 * pair + 1):
            mq = q_ref[:, Q_MLA + hd * MLA_BLK:Q_MLA + (hd + 1) * MLA_BLK]
            mk = k_ref[:, K_MLA + hd * MLA_BLK:K_MLA + (hd + 1) * MLA_BLK]
            mv = v_ref[:, V_MLA + hd * MLA_BLK:V_MLA + (hd + 1) * MLA_BLK]
            p = _softmax(_dot_nt(mq, mk))
            outs.append(_dot(p.astype(BF16), mv * upper))
        both = pltpu.roll(outs[0], MLA_V, axis=1) + outs[1]
        o_ref[:, 512 + pair * 128:512 + (pair + 1) * 128] = both.astype(o_ref.dtype)


def _attention(q, k_own, v_own, vecs, layer, lam_init, n_batch, tq_total, k_cache=None, v_cache=None):
    tq = TOKEN_TILE
    nq = tq_total // tq
    past = 0 if k_cache is None else k_cache.shape[1]
    tk = past + tq_total
    own = lambda cols: pl.BlockSpec((tq_total, cols), lambda i: (i // nq, 0))
    in_specs = [pl.BlockSpec((tq, Q_COLS), lambda i: (i, 0))]
    scratch = []
    if past:
        cached = lambda cols: pl.BlockSpec((None, past, cols), lambda i: (i // nq, 0, 0))
        in_specs += [cached(K_COLS), own(K_COLS), cached(V_COLS), own(V_COLS)]
        args = [q, k_cache, k_own, v_cache, v_own]
        scratch = [pltpu.VMEM((tk, K_COLS), BF16), pltpu.VMEM((tk, V_COLS), BF16)]
    else:
        in_specs += [own(K_COLS), own(V_COLS)]
        args = [q, k_own, v_own]
    in_specs.append(_layer_spec((1, VEC_END), layer))
    args.append(vecs)
    return pl.pallas_call(
        functools.partial(_attn_kernel, lam_init=lam_init, n_q=nq, past=past),
        grid=(n_batch * nq,),
        in_specs=in_specs,
        out_specs=pl.BlockSpec((tq, O_COLS), lambda i: (i, 0)),
        out_shape=jax.ShapeDtypeStruct((n_batch * tq_total, O_COLS), BF16),
        scratch_shapes=scratch,
        compiler_params=pltpu.CompilerParams(vmem_limit_bytes=VMEM_LIMIT),
        name="attention_tk%d" % tk,
    )(*args)


def _gelu_tanh(x):
    return 0.5 * x * (1.0 + jnp.tanh(math.sqrt(2.0 / math.pi) * (x + 0.044715 * (x * x * x))))


def _block_diag_lanes(blocks):
    pair = jnp.concatenate([blocks, blocks], axis=-1)
    tiled = jnp.concatenate([pair] * (SSM_LANES // 128), axis=-1)
    r = jnp.right_shift(lax.broadcasted_iota(jnp.int32, tiled.shape, 0), 4)
    c = jnp.right_shift(lax.broadcasted_iota(jnp.int32, tiled.shape, 1), 6)
    return jnp.where(r == c, tiled, 0.0)


def _ssm_kernel(*refs, n_batch, reverse, finalize, emit_state):
    (u_ref, h0_ref, par_ref, bre_ref, bim_ref, cre_ref, cim_ref) = refs[:7]
    pos = 7
    if finalize:
        yb_ref, vec_ref, wglu_ref = refs[pos:pos + 3]
        pos += 3
    out_ref = refs[pos]
    pos += 1
    if emit_state:
        hfin_ref = refs[pos]
        pos += 1
    bbar_scr, cblk_scr, hst_scr, bu_scr = refs[pos:]

    step = pl.program_id(0)
    rows = u_ref.shape[0]
    a_re = par_ref[0:1, :]
    a_im = par_ref[1:2, :]
    dt = jnp.exp(par_ref[2:3, :])
    mag = jnp.exp(a_re * dt)
    abr = mag * jnp.cos(a_im * dt)
    abi = mag * jnp.sin(a_im * dt)

    @pl.when(step == 0)
    def _():
        den = a_re * a_re + a_im * a_im
        xr = abr - 1.0
        cr = (xr * a_re + abi * a_im) / den
        ci = (abi * a_re - xr * a_im) / den
        bre = _block_diag_lanes(bre_ref[...])
        bim = _block_diag_lanes(bim_ref[...])
        bbar_scr[:, 0:SSM_LANES] = (cr * bre - ci * bim).astype(BF16)
        bbar_scr[:, SSM_LANES:2 * SSM_LANES] = (cr * bim + ci * bre).astype(BF16)
        cblk_scr[:, 0:SSM_LANES] = _block_diag_lanes(cre_ref[...]).astype(BF16)
        cblk_scr[:, SSM_LANES:2 * SSM_LANES] = (-_block_diag_lanes(cim_ref[...])).astype(BF16)
        hst_scr[...] = h0_ref[...]

    u = u_ref[...]
    bu_scr[...] = _dot(u.astype(BF16), bbar_scr[...])

    lw = 256
    if n_batch % 8 == 0:
        n_steps = rows // n_batch
        for j in range(SSM_LANES // lw):
            sre = slice(j * lw, (j + 1) * lw)
            sim = slice(SSM_LANES + j * lw, SSM_LANES + (j + 1) * lw)
            ar = jnp.broadcast_to(abr[:, sre], (n_batch, lw))
            ai = jnp.broadcast_to(abi[:, sre], (n_batch, lw))

            def body(i, carry, sre=sre, sim=sim, ar=ar, ai=ai):
                hr, hi = carry
                t = (n_steps - 1 - i) if reverse else i
                r0 = pl.multiple_of(t * n_batch, n_batch)
                nr = ar * hr - ai * hi + bu_scr[pl.ds(r0, n_batch), sre]
                ni = ar * hi + ai * hr + bu_scr[pl.ds(r0, n_batch), sim]
                bu_scr[pl.ds(r0, n_batch), sre] = nr
                bu_scr[pl.ds(r0, n_batch), sim] = ni
                return nr, ni

            hr, hi = lax.fori_loop(0, n_steps, body, (hst_scr[:, sre], hst_scr[:, sim]), unroll=2)
            hst_scr[:, sre] = hr
            hst_scr[:, sim] = hi
    else:
        assert n_batch == 4
        n_blk = rows // 8
        low = lax.broadcasted_iota(jnp.int32, (8, lw), 0) < 4
        first = jnp.logical_not(low) if reverse else low
        for j in range(SSM_LANES // lw):
            sre = slice(j * lw, (j + 1) * lw)
            sim = slice(SSM_LANES + j * lw, SSM_LANES + (j + 1) * lw)
            ar = jnp.broadcast_to(abr[:, sre], (8, lw))
            ai = jnp.broadcast_to(abi[:, sre], (8, lw))
            a2r = ar * ar - ai * ai
            a2i = 2.0 * ar * ai
            cr = jnp.where(first, ar, a2r)
            ci = jnp.where(first, ai, a2i)

            def body(i, carry, sre=sre, sim=sim, ar=ar, ai=ai, cr=cr, ci=ci):
                pr, pi = carry
                blk = (n_blk - 1 - i) if reverse else i
                r0 = pl.multiple_of(blk * 8, 8)
                vr = bu_scr[pl.ds(r0, 8), sre]
                vi = bu_scr[pl.ds(r0, 8), sim]
                sr = jnp.where(first, 0.0, pltpu.roll(vr, 4, axis=0))
                si = jnp.where(first, 0.0, pltpu.roll(vi, 4, axis=0))
                wr = vr + (ar * sr - ai * si)
                wi = vi + (ar * si + ai * sr)
                nr = cr * pr - ci * pi + wr
                ni = cr * pi + ci * pr + wi
                bu_scr[pl.ds(r0, 8), sre] = nr
                bu_scr[pl.ds(r0, 8), sim] = ni
                return (jnp.where(first, pltpu.roll(nr, 4, axis=0), nr),
                        jnp.where(first, pltpu.roll(ni, 4, axis=0), ni))

            pr, pi = lax.fori_loop(0, n_blk, body, (hst_scr[:, sre], hst_scr[:, sim]), unroll=2)
            hst_scr[:, sre] = pr
            hst_scr[:, sim] = pi

    y = _dot_nt(bu_scr[...].astype(BF16), cblk_scr[...])
    if finalize:
        yt = y + yb_ref[...] + u * vec_ref[:, VEC_SSMD:VEC_SSMD + SSM_WIDTH]
        z = _dot(_gelu_tanh(yt).astype(BF16), wglu_ref[...])
        out_ref[...] = (z[:, 0:SSM_WIDTH] * _sigmoid(z[:, SSM_WIDTH:2 * SSM_WIDTH])).astype(out_ref.dtype)
    else:
        out_ref[...] = y

    if emit_state:
        @pl.when(step == pl.num_programs(0) - 1)
        def _():
            hfin_ref[...] = hst_scr[...]


def _ssm_direction(u_tb, h0, h0_spec, wts, layer, direction, n_batch, rows_per_step, reverse, y_other=None,
                   emit_state=False):
    n_rows = u_tb.shape[0]
    n_steps = n_rows // rows_per_step
    finalize = y_other is not None
    chunk = (lambda k: (n_steps - 1 - k, 0)) if reverse else (lambda k: (k, 0))
    h_rows = h0.shape[-2]

    def par(shape):
        nd = len(shape)
        return pl.BlockSpec((None, None) + shape, lambda k: (layer, direction) + (0,) * nd)

    in_specs = [pl.BlockSpec((rows_per_step, SSM_WIDTH), chunk), h0_spec, par((3, SSM_LANES)),
                par((SSM_WIDTH, SSM_STATE)), par((SSM_WIDTH, SSM_STATE)),
                par((SSM_WIDTH, SSM_STATE)), par((SSM_WIDTH, SSM_STATE))]
    args = [u_tb, h0, wts["ssm_par"], wts["ssm_b_re"], wts["ssm_b_im"], wts["ssm_c_re"], wts["ssm_c_im"]]
    if finalize:
        in_specs += [pl.BlockSpec((rows_per_step, SSM_WIDTH), chunk), _layer_spec((1, VEC_END), layer),
                     _layer_spec((SSM_WIDTH, 2 * SSM_WIDTH), layer)]
        args += [y_other, wts["vecs"], wts["w_glu"]]
    out_shape = [jax.ShapeDtypeStruct((n_rows, SSM_WIDTH), BF16 if finalize else F32)]
    out_specs = [pl.BlockSpec((rows_per_step, SSM_WIDTH), chunk)]
    if emit_state:
        out_shape.append(jax.ShapeDtypeStruct((h_rows, 2 * SSM_LANES), F32))
        out_specs.append(_const_spec((h_rows, 2 * SSM_LANES)))
    return pl.pallas_call(
        functools.partial(_ssm_kernel, n_batch=n_batch, reverse=reverse, finalize=finalize,
                          emit_state=emit_state),
        grid=(n_steps,),
        in_specs=in_specs,
        out_specs=out_specs,
        out_shape=out_shape,
        scratch_shapes=[pltpu.VMEM((SSM_WIDTH, 2 * SSM_LANES), BF16),
                        pltpu.VMEM((SSM_WIDTH, 2 * SSM_LANES), BF16),
                        pltpu.VMEM((h_rows, 2 * SSM_LANES), F32),
                        pltpu.VMEM((rows_per_step, 2 * SSM_LANES), F32)],
        compiler_params=pltpu.CompilerParams(vmem_limit_bytes=VMEM_LIMIT),
        name="ssm_%s_b%d" % ("bwd" if reverse else "fwd", n_batch),
    )(*args)


def _ssm_mixer(u_tb, h0, h0_specs, wts, layer, n_batch, rows_per_step, emit_state):
    bwd = _ssm_direction(u_tb, h0, h0_specs[1], wts, layer, 1, n_batch, rows_per_step, reverse=True,
                         emit_state=emit_state)
    fwd = _ssm_direction(u_tb, h0, h0_specs[0], wts, layer, 0, n_batch, rows_per_step, reverse=False,
                         y_other=bwd[0], emit_state=emit_state)
    return fwd, bwd


def _out_mlp_kernel(x_ref, oa_ref, os_ref, mod_ref, vec_ref, wout_ref, w1_ref, w2_ref, y_ref, *, final):
    g1 = mod_ref[:, 2 * D_MODEL:3 * D_MODEL]
    sh2 = mod_ref[:, 3 * D_MODEL:4 * D_MODEL]
    sc2 = mod_ref[:, 4 * D_MODEL:5 * D_MODEL]
    g2 = mod_ref[:, 5 * D_MODEL:6 * D_MODEL]
    mix = (_dot(oa_ref[:, 0:512], wout_ref[0:512, :]) + _dot(os_ref[...], wout_ref[512:768, :])
           + _dot(oa_ref[:, 512:O_COLS], wout_ref[768:1024, :]))
    x1 = x_ref[...] + g1 * mix
    ms = jnp.mean(x1 * x1, axis=-1, keepdims=True)
    h = x1 * lax.rsqrt(ms + EPS) * vec_ref[:, VEC_N2G:VEC_N2G + D_MODEL]
    h = h * (1.0 + sc2) + sh2
    a = jnp.maximum(_dot(h.astype(BF16), w1_ref[...]), 0.0)
    x2 = x1 + g2 * _dot((a * a).astype(BF16), w2_ref[...])
    if final:
        ms2 = jnp.mean(x2 * x2, axis=-1, keepdims=True)
        x2 = x2 * lax.rsqrt(ms2 + EPS) * vec_ref[:, VEC_FNG:VEC_FNG + D_MODEL]
    y_ref[...] = x2


def _out_mlp(x2d, o_attn, o_ssm_tb, mod3, layer, wts, n_batch, seq, latent, final):
    tm = TOKEN_TILE
    tpb = seq // tm
    n_tok = n_batch * seq
    if latent:
        mod_row = lambda i: (layer * MOD_ROWS + 1 + i // tpb, 0, 0)
    else:
        mod_row = lambda i: (layer * MOD_ROWS, 0, 0)
    tile = lambda i: (i, 0)
    return pl.pallas_call(
        functools.partial(_out_mlp_kernel, final=final),
        grid=(n_tok // tm,),
        in_specs=[
            pl.BlockSpec((tm, D_MODEL), tile),
            pl.BlockSpec((tm, O_COLS), tile),
            pl.BlockSpec((tm, SSM_WIDTH), lambda i: (i % tpb, i // tpb)),
            pl.BlockSpec((None, 1, N_MOD * D_MODEL), mod_row),
            _layer_spec((1, VEC_END), layer),
            _layer_spec((D_MODEL, D_MODEL), layer, single_buffer=True),
            _layer_spec((D_MODEL, D_FF), layer, single_buffer=True),
            _layer_spec((D_FF, D_MODEL), layer, single_buffer=True),
        ],
        out_specs=pl.BlockSpec((tm, D_MODEL), tile),
        out_shape=jax.ShapeDtypeStruct((n_tok, D_MODEL), F32),
        compiler_params=pltpu.CompilerParams(vmem_limit_bytes=VMEM_LIMIT),
        name="out_mlp_%s%s" % ("lat" if latent else "ctx", "_final" if final else ""),
    )(x2d, o_attn, o_ssm_tb, mod3, wts["vecs"], wts["w_out"], wts["w1"], wts["w2"])


def _rope_tables(t_len, chunk, n_chunks):
    n = chunk // 4
    rows = t_len // GRID_W
    row = np.repeat(np.arange(rows), GRID_W).astype(np.float32)
    col = np.tile(np.arange(GRID_W), rows).astype(np.float32)
    freq = (np.float32(ROPE_THETA) ** (-np.arange(n, dtype=np.float32) / np.float32(n))).astype(np.float32)
    ang_r = (row[:, None] * freq).astype(np.float32)
    ang_c = (col[:, None] * freq).astype(np.float32)
    cr, sr, cc, sc = np.cos(ang_r), np.sin(ang_r), np.cos(ang_c), np.sin(ang_c)
    z = np.zeros_like(cr)
    tabs = np.stack([np.concatenate([cr, cr, cc, cc], axis=-1),
                     np.concatenate([-sr, z, -sc, z], axis=-1),
                     np.concatenate([z, sr, z, sc], axis=-1)]).astype(np.float32)
    return np.tile(tabs, (1, 1, n_chunks))


def _mla_rope_tables(t_len):
    tabs = _rope_tables(t_len, MLA_ROPE, 1)
    ident = np.stack([np.ones((t_len, MLA_NOPE), np.float32), np.zeros((t_len, MLA_NOPE), np.float32),
                      np.zeros((t_len, MLA_NOPE), np.float32)])
    tail = ident[:, :, :MLA_BLK - MLA_QK]
    return np.concatenate([ident, tabs, tail], axis=-1)


def _pack_vectors(p, n_layers):
    def rows(a):
        return a.reshape(n_layers, 1, -1)

    fng = jnp.broadcast_to(p["final_norm_g"].reshape(1, 1, D_MODEL), (n_layers, 1, D_MODEL))
    return jnp.concatenate([
        rows(p["norm1_g"]), rows(p["norm2_g"]), fng,
        jnp.tile(rows(p["gqa_qn_g"]), (1, 1, GQA_HEADS)),
        jnp.tile(rows(p["gqa_kn_g"]), (1, 1, GQA_KV_HEADS)),
        rows(p["mla_qn_g"]), jnp.zeros((n_layers, 1, CQ_PAD - MLA_Q_RANK), F32),
        rows(p["mla_kvn_g"]),
        jnp.tile(rows(p["diff_subln_g"]), (1, 1, DIFF_HEADS)),
        rows(p["ssm_d"]),
        rows(p["diff_lq1"]), rows(p["diff_lk1"]), rows(p["diff_lq2"]), rows(p["diff_lk2"]),
    ], axis=-1)


def kernel(x_prompt, x_sample, cache_diff_k, cache_diff_v, cache_gqa_k, cache_gqa_v, cache_mla_ckv, cache_mla_krope, state_ssm_re, state_ssm_im, c, c_ctx, norm1_g, norm2_g, w_ada, b_ada, w_in, w_out, diff_lq1, diff_lk1, diff_lq2, diff_lk2, diff_subln_g, gqa_qn_g, gqa_kn_g, ssm_a_re, ssm_a_im, ssm_log_dt, ssm_b_re, ssm_b_im, ssm_c_re, ssm_c_im, ssm_d, ssm_w_glu, mla_qn_g, mla_kvn_g, mla_w_uq, mla_w_ukv, mlp_w1, mlp_w2, final_norm_g):
    n_layers = w_in.shape[0]
    n_ctx, seq, _ = x_prompt.shape
    n_lat, t_lat, _ = x_sample.shape
    past = cache_diff_k.shape[2]
    assert n_lat + 1 <= MOD_ROWS and n_lat == 4 and n_ctx % 8 == 0
    assert seq == TOKEN_TILE and t_lat % TOKEN_TILE == 0

    p = dict(norm1_g=norm1_g, norm2_g=norm2_g, final_norm_g=final_norm_g, gqa_qn_g=gqa_qn_g, gqa_kn_g=gqa_kn_g,
             mla_qn_g=mla_qn_g, mla_kvn_g=mla_kvn_g, diff_subln_g=diff_subln_g, ssm_d=ssm_d, diff_lq1=diff_lq1,
             diff_lk1=diff_lk1, diff_lq2=diff_lq2, diff_lk2=diff_lk2)
    ssm_rows = SSM_GROUPS * SSM_GROUP
    wts = {
        "vecs": _pack_vectors(p, n_layers),
        "w_in_t": jnp.swapaxes(w_in, 1, 2),
        "w_uq": mla_w_uq, "w_ukv": mla_w_ukv,
        "w_out": w_out.astype(BF16), "w1": mlp_w1.astype(BF16), "w2": mlp_w2.astype(BF16),
        "w_glu": ssm_w_glu.astype(BF16),
        "ssm_par": jnp.stack([ssm_a_re.reshape(n_layers, 2, SSM_LANES), ssm_a_im.reshape(n_layers, 2, SSM_LANES),
                              jnp.repeat(ssm_log_dt, SSM_STATE, axis=-1)], axis=2),
        "ssm_b_re": jnp.swapaxes(ssm_b_re, 3, 4).reshape(n_layers, 2, ssm_rows, SSM_STATE),
        "ssm_b_im": jnp.swapaxes(ssm_b_im, 3, 4).reshape(n_layers, 2, ssm_rows, SSM_STATE),
        "ssm_c_re": ssm_c_re.reshape(n_layers, 2, ssm_rows, SSM_STATE),
        "ssm_c_im": ssm_c_im.reshape(n_layers, 2, ssm_rows, SSM_STATE),
    }

    cond = jnp.concatenate([c_ctx[None], c, jnp.zeros((MOD_ROWS - 1 - n_lat, D_MODEL), F32)], axis=0)
    mod3 = _adaln(cond, w_ada, b_ada).reshape(n_layers * MOD_ROWS, 1, N_MOD * D_MODEL)

    rope_tabs = (jnp.asarray(_rope_tables(t_lat, DIFF_QK, 256 // DIFF_QK)),
                 jnp.asarray(_rope_tables(t_lat, GQA_HEAD_DIM, 256 // GQA_HEAD_DIM)),
                 jnp.asarray(_mla_rope_tables(t_lat)))
    caches = (jnp.transpose(cache_diff_k, (0, 1, 3, 4, 2)).reshape(n_lat, n_layers, 256, past),
              jnp.transpose(cache_diff_v, (0, 1, 3, 4, 2)).reshape(n_lat, n_layers, 256, past),
              jnp.transpose(cache_gqa_k, (0, 1, 3, 4, 2)).reshape(n_lat, n_layers, 128, past),
              jnp.transpose(cache_gqa_v, (0, 1, 3, 4, 2)).reshape(n_lat, n_layers, 128, past),
              cache_mla_ckv,
              jnp.swapaxes(cache_mla_krope, 2, 3))

    lam_inits = [0.8 - 0.6 * math.exp(-0.3 * l) for l in range(n_layers)]

    x = x_prompt.reshape(n_ctx * seq, D_MODEL)
    ctx_rows = 32 * n_ctx
    zero_state = jnp.zeros((n_ctx, 2 * SSM_LANES), F32)
    zero_spec = _const_spec((n_ctx, 2 * SSM_LANES))
    kept = None
    states = []
    for l in range(n_layers):
        res = _in_proj(x, mod3, l, wts, n_ctx, seq, latent=False, prev_caches=kept)
        q, k, v, u = res[:4]
        kept = res[4:]
        o_attn = _attention(q, k, v, wts["vecs"], l, lam_inits[l], n_ctx, seq)
        fwd, bwd = _ssm_mixer(u.reshape(seq * n_ctx, SSM_WIDTH), zero_state, (zero_spec, zero_spec), wts, l,
                              n_ctx, ctx_rows, emit_state=True)
        o_ssm = fwd[0].reshape(seq, n_ctx * SSM_WIDTH)
        x = _out_mlp(x, o_attn, o_ssm, mod3, l, wts, n_ctx, seq, latent=False, final=(l == n_layers - 1))
        states.append(jnp.stack([fwd[1], bwd[1]], axis=1))
    y_prompt = x.reshape(n_ctx, seq, D_MODEL)

    x = x_sample.reshape(n_lat * t_lat, D_MODEL)
    lat_rows = 256 * n_lat
    h0 = jnp.concatenate([state_ssm_re.reshape(n_lat, n_layers, 2, SSM_LANES),
                          state_ssm_im.reshape(n_lat, n_layers, 2, SSM_LANES)], axis=-1)
    h0 = jnp.transpose(h0, (1, 2, 0, 3))
    h0 = jnp.concatenate([h0, h0], axis=2)
    for l in range(n_layers):
        h0_specs = tuple(pl.BlockSpec((None, None, 2 * n_lat, 2 * SSM_LANES),
                                      lambda k, l=l, d=d: (l, d, 0, 0)) for d in range(2))
        q, k, v, u = _in_proj(x, mod3, l, wts, n_lat, t_lat, latent=True, rope_tabs=rope_tabs)
        kc, vc = _cache_prep(caches, l, wts["w_ukv"])
        o_attn = _attention(q, k, v, wts["vecs"], l, lam_inits[l], n_lat, t_lat, k_cache=kc, v_cache=vc)
        fwd, _ = _ssm_mixer(u.reshape(t_lat * n_lat, SSM_WIDTH), h0, h0_specs, wts, l, n_lat, lat_rows,
                            emit_state=False)
        o_ssm = fwd[0].reshape(t_lat, n_lat * SSM_WIDTH)
        x = _out_mlp(x, o_attn, o_ssm, mod3, l, wts, n_lat, t_lat, latent=True, final=(l == n_layers - 1))
    y_sample = x.reshape(n_lat, t_lat, D_MODEL)

    cdk, cdv, cgk, cgv, cckv, ckr = kept

    def token_major(a, heads, dim):
        return jnp.transpose(a.reshape(n_ctx, n_layers, heads, dim, seq), (0, 1, 4, 2, 3))

    st = jnp.stack(states, axis=1)
    return (y_prompt, y_sample,
            token_major(cdk, DIFF_HEADS, 2 * DIFF_QK), token_major(cdv, DIFF_HEADS, DIFF_V),
            token_major(cgk, GQA_KV_HEADS, GQA_HEAD_DIM), token_major(cgv, GQA_KV_HEADS, GQA_HEAD_DIM),
            cckv, jnp.swapaxes(ckr, 2, 3),
            st[..., :SSM_LANES].reshape(n_ctx, n_layers, 2, SSM_GROUPS, SSM_STATE),
            st[..., SSM_LANES:].reshape(n_ctx, n_layers, 2, SSM_GROUPS, SSM_STATE))
```

```python
import functools
import math

import numpy as np
import jax
import jax.numpy as jnp
from jax import lax
from jax.experimental import pallas as pl
from jax.experimental.pallas import tpu as pltpu

F32 = jnp.float32
BF16 = jnp.bfloat16

D_MODEL = 1024
GRID_W = 64
ROPE_THETA = 10000.0
EPS = 1e-6
DIFF_HEADS, DIFF_QK, DIFF_V = 4, 32, 64
GQA_HEADS, GQA_KV_HEADS, GQA_HEAD_DIM = 4, 2, 64
SSM_WIDTH, SSM_GROUP, SSM_STATE = 256, 16, 64
SSM_GROUPS = SSM_WIDTH // SSM_GROUP
SSM_LANES = SSM_GROUPS * SSM_STATE
MLA_HEADS, MLA_Q_RANK, MLA_KV_RANK, MLA_NOPE, MLA_ROPE, MLA_V = 4, 192, 128, 64, 32, 64
MLA_QK = MLA_NOPE + MLA_ROPE
D_FF = 4 * D_MODEL
N_MOD = 6
LOG2E = math.log2(math.e)
IN_COLS = 1888
MOD_ROWS = 8

P_DQ, P_DK, P_DV, P_GQ, P_GK, P_GV, P_U, P_CQ, P_CKV, P_KR, P_END = (
    0, 256, 512, 768, 1024, 1152, 1280, 1536, 1792, 1920, 2048)
S_CQ_END, S_CKV, S_KR = 1728, 1728, 1856
CQ_PAD = P_CKV - P_CQ
MLA_BLK = 128
Q_DIFF, Q_GQA, Q_MLA, Q_COLS = 0, 256, 512, 1024
K_DIFF, K_GQA, K_MLA, K_COLS = 0, 256, 512, 1024
V_DIFF, V_GQA, V_MLA, V_COLS = 0, 256, 512, 1024
O_COLS = 768

VEC_N1G, VEC_N2G, VEC_FNG, VEC_QNG, VEC_KNG, VEC_MQNG, VEC_KVNG, VEC_SUBG, VEC_SSMD, VEC_LAM, VEC_END = (
    0, 1024, 2048, 3072, 3328, 3456, 3712, 3840, 4096, 4352, 4480)

TOKEN_TILE = 256
CTX_SSM_CHUNK = 32
LAT_SSM_CHUNK = 256
VMEM_LIMIT = 48 * 1024 * 1024


def _dot(a, b):
    return jnp.dot(a, b, preferred_element_type=F32)


def _dot_nt(a, b):
    return lax.dot_general(a, b, (((1,), (1,)), ((), ())), preferred_element_type=F32)


def _block_ones(width, seg):
    shift = seg.bit_length() - 1
    r = jnp.right_shift(lax.broadcasted_iota(jnp.int32, (width, width), 0), shift)
    c = jnp.right_shift(lax.broadcasted_iota(jnp.int32, (width, width), 1), shift)
    return jnp.where(r == c, 1.0, 0.0).astype(BF16)


def _seg_sum_sq(y, seg):
    ones = _block_ones(y.shape[-1], seg)
    sq = y * y
    hi = sq.astype(BF16)
    lo = (sq - hi.astype(F32)).astype(BF16)
    return _dot(hi, ones) + _dot(lo, ones)


def _rope(x, tab_ref, quarter):
    outs = []
    for j in range(x.shape[-1] // 128):
        sl = slice(j * 128, (j + 1) * 128)
        xs = x[:, sl]
        up = pltpu.roll(xs, 128 - quarter, axis=1)
        dn = pltpu.roll(xs, quarter, axis=1)
        outs.append(xs * tab_ref[0, :, sl] + up * tab_ref[1, :, sl] + dn * tab_ref[2, :, sl])
    return outs[0] if len(outs) == 1 else jnp.concatenate(outs, axis=-1)


def _sigmoid(x):
    return 1.0 / (1.0 + jnp.exp(-x))


def _lane_lt(width, bound):
    return lax.broadcasted_iota(jnp.int32, (1, width), 1) < bound


def _dup_halves(x):
    swapped = pltpu.roll(x, 64, axis=1)
    left = _lane_lt(128, 64)
    return jnp.where(left, x, swapped), jnp.where(left, swapped, x)


def _const_spec(shape):
    nd = len(shape)
    return pl.BlockSpec(shape, lambda *_: (0,) * nd)


def _layer_spec(shape, layer, single_buffer=False):
    nd = len(shape)
    kw = {"pipeline_mode": pl.Buffered(1)} if single_buffer else {}
    return pl.BlockSpec((None,) + tuple(shape), lambda *_: (layer,) + (0,) * nd, **kw)


def _adaln_kernel(cond_ref, w_ref, b_ref, o_ref):
    c = cond_ref[...]
    s = c * _sigmoid(c)
    o_ref[...] = _dot(s.astype(BF16), w_ref[...].astype(BF16)) + b_ref[...]


def _adaln(cond, w_ada, b_ada):
    n_layers = w_ada.shape[0]
    tn = 1024
    return pl.pallas_call(
        _adaln_kernel,
        grid=(n_layers, N_MOD * D_MODEL // tn),
        in_specs=[
            pl.BlockSpec((MOD_ROWS, D_MODEL), lambda l, j: (0, 0)),
            pl.BlockSpec((None, D_MODEL, tn), lambda l, j: (l, 0, j)),
            pl.BlockSpec((None, 1, tn), lambda l, j: (l, 0, j)),
        ],
        out_specs=pl.BlockSpec((None, MOD_ROWS, tn), lambda l, j: (l, 0, j)),
        out_shape=jax.ShapeDtypeStruct((n_layers, MOD_ROWS, N_MOD * D_MODEL), F32),
        compiler_params=pltpu.CompilerParams(vmem_limit_bytes=VMEM_LIMIT),
        name="adaln",
    )(cond, w_ada, b_ada.reshape(n_layers, 1, N_MOD * D_MODEL))


N_CACHE = 6


def _in_proj_kernel(*refs, latent, n_alias):
    x_ref, mod_ref, vec_ref, wint_ref, wuq_ref, wukv_ref = refs[:6]
    pos = 6
    if latent:
        rope_d_ref, rope_g_ref, rope_m_ref = refs[pos:pos + 3]
        pos += 3
    pos += n_alias
    q_ref, k_ref, v_ref, u_ref = refs[pos:pos + 4]
    pos += 4
    if not latent:
        cdk_ref, cdv_ref, cgk_ref, cgv_ref, cckv_ref, ckr_ref = refs[pos:pos + N_CACHE]
        pos += N_CACHE
    win_scr, wuq_scr, wukv_scr = refs[pos:]

    @pl.when(pl.program_id(0) == 0)
    def _():
        win_scr[0:S_CQ_END, :] = wint_ref[0:S_CQ_END, :].astype(BF16)
        win_scr[S_CQ_END:P_CKV, :] = jnp.zeros((P_CKV - S_CQ_END, D_MODEL), BF16)
        win_scr[P_CKV:P_KR, :] = wint_ref[S_CKV:S_KR, :].astype(BF16)
        win_scr[P_KR:P_KR + MLA_NOPE, :] = jnp.zeros((MLA_NOPE, D_MODEL), BF16)
        win_scr[P_KR + MLA_NOPE:P_KR + MLA_QK, :] = wint_ref[S_KR:IN_COLS, :].astype(BF16)
        win_scr[P_KR + MLA_QK:P_END, :] = jnp.zeros((P_END - P_KR - MLA_QK, D_MODEL), BF16)
        wuq_scr[...] = jnp.zeros(wuq_scr.shape, BF16)
        wuq = wuq_ref[...]
        for hd in range(MLA_HEADS):
            wuq_scr[0:MLA_Q_RANK, hd * MLA_BLK:hd * MLA_BLK + MLA_QK] = (
                wuq[:, hd * MLA_QK:(hd + 1) * MLA_QK].astype(BF16))
        wukv_scr[...] = wukv_ref[...].astype(BF16)

    x = x_ref[...]
    sh1 = mod_ref[:, 0:D_MODEL]
    sc1 = mod_ref[:, D_MODEL:2 * D_MODEL]
    ms = jnp.mean(x * x, axis=-1, keepdims=True)
    h = x * lax.rsqrt(ms + EPS) * vec_ref[:, VEC_N1G:VEC_N1G + D_MODEL]
    h = h * (1.0 + sc1) + sh1
    proj = _dot_nt(h.astype(BF16), win_scr[...])

    dq = proj[:, P_DQ:P_DK]
    dk = proj[:, P_DK:P_DV]
    dv = proj[:, P_DV:P_GQ]
    gq = proj[:, P_GQ:P_GK]
    gk = proj[:, P_GK:P_GV]
    gv = proj[:, P_GV:P_U]
    u_ref[...] = proj[:, P_U:P_CQ]
    cq = proj[:, P_CQ:P_CKV]
    ckv = proj[:, P_CKV:P_KR]
    krb = proj[:, P_KR:P_END]

    qng = vec_ref[:, VEC_QNG:VEC_QNG + 256]
    kng = vec_ref[:, VEC_KNG:VEC_KNG + 128]
    gq = gq * lax.rsqrt(_seg_sum_sq(gq, GQA_HEAD_DIM) * (1.0 / GQA_HEAD_DIM) + EPS) * qng
    gk = gk * lax.rsqrt(_seg_sum_sq(gk, GQA_HEAD_DIM) * (1.0 / GQA_HEAD_DIM) + EPS) * kng
    ckv_n = (ckv * lax.rsqrt(jnp.mean(ckv * ckv, axis=-1, keepdims=True) + EPS)
             * vec_ref[:, VEC_KVNG:VEC_KVNG + MLA_KV_RANK])
    cq_ms = jnp.sum(cq * cq, axis=-1, keepdims=True) * (1.0 / MLA_Q_RANK)
    cq_n = cq * lax.rsqrt(cq_ms + EPS) * vec_ref[:, VEC_MQNG:VEC_MQNG + CQ_PAD]
    mq = _dot(cq_n.astype(BF16), wuq_scr[...])
    kv = _dot(ckv_n.astype(BF16), wukv_scr[...])

    if not latent:
        cdk_ref[...] = dk.T
        cdv_ref[...] = dv.T
        cgk_ref[...] = gk.T
        cgv_ref[...] = gv.T
        cckv_ref[...] = ckv_n
        ckr_ref[...] = krb.T[MLA_NOPE:MLA_QK, :]
    else:
        dq = _rope(dq, rope_d_ref, DIFF_QK // 4)
        dk = _rope(dk, rope_d_ref, DIFF_QK // 4)
        gq = _rope(gq, rope_g_ref, GQA_HEAD_DIM // 4)
        gk = _rope(gk, rope_g_ref, GQA_HEAD_DIM // 4)
        mq = jnp.concatenate(
            [_rope(mq[:, hd * MLA_BLK:(hd + 1) * MLA_BLK], rope_m_ref, MLA_ROPE // 4)
             for hd in range(MLA_HEADS)], axis=-1)
        krb = _rope(krb, rope_m_ref, MLA_ROPE // 4)

    q_ref[:, Q_DIFF:Q_GQA] = (dq * (LOG2E * DIFF_QK ** -0.5)).astype(BF16)
    q_ref[:, Q_GQA:Q_MLA] = (gq * (LOG2E * GQA_HEAD_DIM ** -0.5)).astype(BF16)
    q_ref[:, Q_MLA:Q_COLS] = (mq * (LOG2E * MLA_QK ** -0.5)).astype(BF16)
    k_ref[:, K_DIFF:K_GQA] = dk.astype(BF16)
    gk0, gk1 = _dup_halves(gk)
    k_ref[:, K_GQA:K_GQA + 128] = gk0.astype(BF16)
    k_ref[:, K_GQA + 128:K_MLA] = gk1.astype(BF16)
    nope = _lane_lt(MLA_BLK, MLA_NOPE)
    for hd in range(MLA_HEADS):
        blk = slice(hd * MLA_BLK, (hd + 1) * MLA_BLK)
        k_ref[:, K_MLA + hd * MLA_BLK:K_MLA + (hd + 1) * MLA_BLK] = jnp.where(nope, kv[:, blk], krb).astype(BF16)
    v_ref[:, V_DIFF:V_GQA] = dv.astype(BF16)
    gv0, gv1 = _dup_halves(gv)
    v_ref[:, V_GQA:V_GQA + 128] = gv0.astype(BF16)
    v_ref[:, V_GQA + 128:V_MLA] = gv1.astype(BF16)
    v_ref[:, V_MLA:V_COLS] = kv.astype(BF16)


def _in_proj(x2d, mod3, layer, wts, n_batch, seq, latent, rope_tabs=None, prev_caches=None):
    tm = TOKEN_TILE
    tpb = seq // tm
    n_tok = n_batch * seq
    n_layers = wts["vecs"].shape[0]
    if latent:
        mod_row = lambda i: (layer * MOD_ROWS + 1 + i // tpb, 0, 0)
    else:
        mod_row = lambda i: (layer * MOD_ROWS, 0, 0)
    tile = lambda i: (i, 0)
    in_specs = [
        pl.BlockSpec((tm, D_MODEL), tile),
        pl.BlockSpec((None, 1, N_MOD * D_MODEL), mod_row),
        _layer_spec((1, VEC_END), layer),
        _layer_spec((IN_COLS, D_MODEL), layer, single_buffer=True),
        _layer_spec((MLA_Q_RANK, MLA_HEADS * MLA_QK), layer, single_buffer=True),
        _layer_spec((MLA_KV_RANK, MLA_HEADS * MLA_BLK), layer, single_buffer=True),
    ]
    args = [x2d, mod3, wts["vecs"], wts["w_in_t"], wts["w_uq"], wts["w_ukv"]]
    out_shape = [
        jax.ShapeDtypeStruct((n_tok, Q_COLS), BF16),
        jax.ShapeDtypeStruct((n_tok, K_COLS), BF16),
        jax.ShapeDtypeStruct((n_tok, V_COLS), BF16),
        jax.ShapeDtypeStruct((n_tok, SSM_WIDTH), F32),
    ]
    out_specs = [
        pl.BlockSpec((tm, Q_COLS), tile),
        pl.BlockSpec((tm, K_COLS), tile),
        pl.BlockSpec((tm, V_COLS), tile),
        pl.BlockSpec((tm, SSM_WIDTH), tile),
    ]
    aliases = {}
    n_alias = 0
    if latent:
        pos = lambda i: (0, i % tpb, 0)
        in_specs += [pl.BlockSpec((3, tm, 256), pos), pl.BlockSpec((3, tm, 256), pos),
                     pl.BlockSpec((3, tm, 128), pos)]
        args += list(rope_tabs)
    else:
        assert tpb == 1
        cache_dims = [(256, seq), (256, seq), (128, seq), (128, seq), (seq, MLA_KV_RANK), (MLA_ROPE, seq)]
        for dims in cache_dims:
            out_shape.append(jax.ShapeDtypeStruct((n_batch, n_layers) + dims, F32))
            out_specs.append(pl.BlockSpec((None, None) + dims, lambda i: (i, layer, 0, 0)))
        if prev_caches is not None:
            n_alias = N_CACHE
            for j, arr in enumerate(prev_caches):
                aliases[len(args)] = 4 + j
                in_specs.append(pl.BlockSpec(memory_space=pl.ANY))
                args.append(arr)
    return pl.pallas_call(
        functools.partial(_in_proj_kernel, latent=latent, n_alias=n_alias),
        grid=(n_tok // tm,),
        in_specs=in_specs,
        out_specs=out_specs,
        out_shape=out_shape,
        input_output_aliases=aliases,
        scratch_shapes=[pltpu.VMEM((P_END, D_MODEL), BF16),
                        pltpu.VMEM((CQ_PAD, MLA_HEADS * MLA_BLK), BF16),
                        pltpu.VMEM((MLA_KV_RANK, MLA_HEADS * MLA_BLK), BF16)],
        compiler_params=pltpu.CompilerParams(vmem_limit_bytes=VMEM_LIMIT),
        name="in_proj_lat" if latent else "in_proj_ctx",
    )(*args)


def _cache_prep_kernel(cdk_ref, cdv_ref, cgk_ref, cgv_ref, cckv_ref, ckr_ref, wukv_ref, k_ref, v_ref):
    past = k_ref.shape[0]
    kv = _dot(cckv_ref[...].astype(BF16), wukv_ref[...].astype(BF16))
    krb = jnp.concatenate([jnp.zeros((MLA_NOPE, past), F32), ckr_ref[...],
                           jnp.zeros((MLA_BLK - MLA_QK, past), F32)], axis=0).T
    k_ref[:, K_DIFF:K_GQA] = cdk_ref[...].T.astype(BF16)
    gk0, gk1 = _dup_halves(cgk_ref[...].T)
    k_ref[:, K_GQA:K_GQA + 128] = gk0.astype(BF16)
    k_ref[:, K_GQA + 128:K_MLA] = gk1.astype(BF16)
    nope = _lane_lt(MLA_BLK, MLA_NOPE)
    for hd in range(MLA_HEADS):
        blk = slice(hd * MLA_BLK, (hd + 1) * MLA_BLK)
        k_ref[:, K_MLA + hd * MLA_BLK:K_MLA + (hd + 1) * MLA_BLK] = jnp.where(nope, kv[:, blk], krb).astype(BF16)
    v_ref[:, V_DIFF:V_GQA] = cdv_ref[...].T.astype(BF16)
    gv0, gv1 = _dup_halves(cgv_ref[...].T)
    v_ref[:, V_GQA:V_GQA + 128] = gv0.astype(BF16)
    v_ref[:, V_GQA + 128:V_MLA] = gv1.astype(BF16)
    v_ref[:, V_MLA:V_COLS] = kv.astype(BF16)


def _cache_prep(caches, layer, w_ukv):
    cdk, cdv, cgk, cgv, cckv, ckr = caches
    n_batch, _, _, past = cdk.shape

    def spec(rows, cols):
        return pl.BlockSpec((None, None, rows, cols), lambda b: (b, layer, 0, 0))

    return pl.pallas_call(
        _cache_prep_kernel,
        grid=(n_batch,),
        in_specs=[spec(256, past), spec(256, past), spec(128, past), spec(128, past),
                  spec(past, MLA_KV_RANK), spec(MLA_ROPE, past),
                  _layer_spec((MLA_KV_RANK, MLA_HEADS * MLA_BLK), layer)],
        out_specs=[pl.BlockSpec((None, past, K_COLS), lambda b: (b, 0, 0)),
                   pl.BlockSpec((None, past, V_COLS), lambda b: (b, 0, 0))],
        out_shape=[jax.ShapeDtypeStruct((n_batch, past, K_COLS), BF16),
                   jax.ShapeDtypeStruct((n_batch, past, V_COLS), BF16)],
        name="cache_prep",
    )(cdk, cdv, cgk, cgv, cckv, ckr, w_ukv)


def _exp_scores(s):
    m = jnp.max(s, axis=-1, keepdims=True)
    e = jnp.exp2(s - m)
    return e.astype(BF16), 1.0 / jnp.sum(e, axis=-1, keepdims=True)


def _lane_mask(width, lo, hi):
    lane = lax.broadcasted_iota(jnp.int32, (1, width), 1)
    return jnp.where((lane >= lo) & (lane < hi), 1.0, 0.0).astype(BF16)


def _attn_kernel(*refs, lam_init, n_q, past):
    if past:
        q_ref, kc_ref, ko_ref, vc_ref, vo_ref, vec_ref, o_ref, k_ref, v_ref = refs

        @pl.when(pl.program_id(0) % n_q == 0)
        def _():
            k_ref[0:past, :] = kc_ref[...]
            k_ref[past:, :] = ko_ref[...]
            v_ref[0:past, :] = vc_ref[...]
            v_ref[past:, :] = vo_ref[...]
    else:
        q_ref, k_ref, v_ref, vec_ref, o_ref = refs
    tq = q_ref.shape[0]
    lq1 = vec_ref[:, VEC_LAM:VEC_LAM + 32]
    lk1 = vec_ref[:, VEC_LAM + 32:VEC_LAM + 64]
    lq2 = vec_ref[:, VEC_LAM + 64:VEC_LAM + 96]
    lk2 = vec_ref[:, VEC_LAM + 96:VEC_LAM + 128]
    lam = (jnp.exp(jnp.sum(lq1 * lk1, axis=-1, keepdims=True))
           - jnp.exp(jnp.sum(lq2 * lk2, axis=-1, keepdims=True)) + lam_init)

    dq = q_ref[:, Q_DIFF:Q_GQA]
    dk = k_ref[:, K_DIFF:K_GQA]
    dv = v_ref[:, V_DIFF:V_GQA]
    acc = jnp.zeros((tq, 256), F32)
    for hd in range(DIFF_HEADS):
        base = hd * 2 * DIFF_QK
        e1, r1 = _exp_scores(_dot_nt(dq * _lane_mask(256, base, base + DIFF_QK), dk))
        e2, r2 = _exp_scores(_dot_nt(dq * _lane_mask(256, base + DIFF_QK, base + 2 * DIFF_QK), dk))
        vm = dv * _lane_mask(256, hd * DIFF_V, (hd + 1) * DIFF_V)
        acc = acc + _dot(e1, vm) * r1 - _dot(e2, vm) * (lam * r2)
    ms = _seg_sum_sq(acc, DIFF_V) * (1.0 / DIFF_V)
    subg = vec_ref[:, VEC_SUBG:VEC_SUBG + 256]
    o_ref[:, 0:256] = (acc * lax.rsqrt(ms + EPS) * subg * (1.0 - lam_init)).astype(o_ref.dtype)

    for blk in range(GQA_KV_HEADS):
        gq = q_ref[:, Q_GQA + blk * 128:Q_GQA + (blk + 1) * 128]
        gk = k_ref[:, K_GQA + blk * 128:K_GQA + (blk + 1) * 128]
        gv = v_ref[:, V_GQA + blk * 128:V_GQA + (blk + 1) * 128]
        acc = jnp.zeros((tq, 128), F32)
        for side in range(2):
            msk = _lane_mask(128, side * GQA_HEAD_DIM, (side + 1) * GQA_HEAD_DIM)
            e, r = _exp_scores(_dot_nt(gq * msk, gk))
            acc = acc + _dot(e, gv * msk) * r
        o_ref[:, 256 + blk * 128:256 + (blk + 1) * 128] = acc.astype(o_ref.dtype)

    upper = _lane_mask(MLA_BLK, MLA_NOPE, MLA_BLK)
    for pair in range(MLA_HEADS // 2):
        outs = []
        for hd in (2 * pair, 2 * pair + 1):
            mq = q_ref[:, Q_MLA + hd * MLA_BLK:Q_MLA + (hd + 1) * MLA_BLK]
            mk = k_ref[:, K_MLA + hd * MLA_BLK:K_MLA + (hd + 1) * MLA_BLK]
            mv = v_ref[:, V_MLA + hd * MLA_BLK:V_MLA + (hd + 1) * MLA_BLK]
            e, r = _exp_scores(_dot_nt(mq, mk))
            outs.append(_dot(e, mv * upper) * r)
        both = pltpu.roll(outs[0], MLA_V, axis=1) + outs[1]
        o_ref[:, 512 + pair * 128:512 + (pair + 1) * 128] = both.astype(o_ref.dtype)


def _attention(q, k_own, v_own, vecs, layer, lam_init, n_batch, tq_total, k_cache=None, v_cache=None):
    tq = TOKEN_TILE
    nq = tq_total // tq
    past = 0 if k_cache is None else k_cache.shape[1]
    tk = past + tq_total
    own = lambda cols: pl.BlockSpec((tq_total, cols), lambda i: (i // nq, 0))
    in_specs = [pl.BlockSpec((tq, Q_COLS), lambda i: (i, 0))]
    scratch = []
    if past:
        cached = lambda cols: pl.BlockSpec((None, past, cols), lambda i: (i // nq, 0, 0))
        in_specs += [cached(K_COLS), own(K_COLS), cached(V_COLS), own(V_COLS)]
        args = [q, k_cache, k_own, v_cache, v_own]
        scratch = [pltpu.VMEM((tk, K_COLS), BF16), pltpu.VMEM((tk, V_COLS), BF16)]
    else:
        in_specs += [own(K_COLS), own(V_COLS)]
        args = [q, k_own, v_own]
    in_specs.append(_layer_spec((1, VEC_END), layer))
    args.append(vecs)
    return pl.pallas_call(
        functools.partial(_attn_kernel, lam_init=lam_init, n_q=nq, past=past),
        grid=(n_batch * nq,),
        in_specs=in_specs,
        out_specs=pl.BlockSpec((tq, O_COLS), lambda i: (i, 0)),
        out_shape=jax.ShapeDtypeStruct((n_batch * tq_total, O_COLS), BF16),
        scratch_shapes=scratch,
        compiler_params=pltpu.CompilerParams(vmem_limit_bytes=VMEM_LIMIT),
        name="attention_tk%d" % tk,
    )(*args)


def _gelu_tanh(x):
    return 0.5 * x * (1.0 + jnp.tanh(math.sqrt(2.0 / math.pi) * (x + 0.044715 * (x * x * x))))


def _block_diag_lanes(blocks):
    pair = jnp.concatenate([blocks, blocks], axis=-1)
    tiled = jnp.concatenate([pair] * (SSM_LANES // 128), axis=-1)
    r = jnp.right_shift(lax.broadcasted_iota(jnp.int32, tiled.shape, 0), 4)
    c = jnp.right_shift(lax.broadcasted_iota(jnp.int32, tiled.shape, 1), 6)
    return jnp.where(r == c, tiled, 0.0)


def _ssm_kernel(*refs, n_batch, emit_state):
    uf_ref, ub_ref, h0_ref, par_ref, bre_ref, bim_ref, cre_ref, cim_ref, yf_ref, yb_ref = refs[:10]
    pos = 10
    if emit_state:
        hfin_ref = refs[pos]
        pos += 1
    bbar_scr, cblk_scr, hst_scr, bu_scr = refs[pos:]

    step = pl.program_id(0)
    t_chunk = uf_ref.shape[1]
    rows = n_batch * t_chunk
    abar = []
    for d in range(2):
        a_re = par_ref[d, 0:1, :]
        a_im = par_ref[d, 1:2, :]
        dt = jnp.exp(par_ref[d, 2:3, :])
        mag = jnp.exp(a_re * dt)
        abar.append((mag * jnp.cos(a_im * dt), mag * jnp.sin(a_im * dt)))

    @pl.when(step == 0)
    def _():
        for d in range(2):
            a_re = par_ref[d, 0:1, :]
            a_im = par_ref[d, 1:2, :]
            abr, abi = abar[d]
            den = a_re * a_re + a_im * a_im
            xr = abr - 1.0
            cr = (xr * a_re + abi * a_im) / den
            ci = (abi * a_re - xr * a_im) / den
            bre = _block_diag_lanes(bre_ref[d])
            bim = _block_diag_lanes(bim_ref[d])
            bbar_scr[d, :, 0:SSM_LANES] = (cr * bre - ci * bim).astype(BF16)
            bbar_scr[d, :, SSM_LANES:2 * SSM_LANES] = (cr * bim + ci * bre).astype(BF16)
            cblk_scr[d, :, 0:SSM_LANES] = _block_diag_lanes(cre_ref[d]).astype(BF16)
            cblk_scr[d, :, SSM_LANES:2 * SSM_LANES] = (-_block_diag_lanes(cim_ref[d])).astype(BF16)
        hst_scr[...] = h0_ref[...]

    for d, u_ref in enumerate((uf_ref, ub_ref)):
        u_tm = pltpu.einshape("btf->(tb)f", u_ref[...])
        bu_scr[d] = _dot(u_tm.astype(BF16), bbar_scr[d])

    lw = 128
    if n_batch % 8 == 0:
        for j in range(SSM_LANES // lw):
            sre = slice(j * lw, (j + 1) * lw)
            sim = slice(SSM_LANES + j * lw, SSM_LANES + (j + 1) * lw)
            coef = [(jnp.broadcast_to(abar[d][0][:, sre], (n_batch, lw)),
                     jnp.broadcast_to(abar[d][1][:, sre], (n_batch, lw))) for d in range(2)]

            def body(i, carry, sre=sre, sim=sim, coef=coef):
                new = []
                for d in range(2):
                    hr, hi = carry[2 * d], carry[2 * d + 1]
                    ar, ai = coef[d]
                    t = i if d == 0 else t_chunk - 1 - i
                    r0 = pl.multiple_of(t * n_batch, n_batch)
                    nr = ar * hr - ai * hi + bu_scr[d, pl.ds(r0, n_batch), sre]
                    ni = ar * hi + ai * hr + bu_scr[d, pl.ds(r0, n_batch), sim]
                    bu_scr[d, pl.ds(r0, n_batch), sre] = nr
                    bu_scr[d, pl.ds(r0, n_batch), sim] = ni
                    new += [nr, ni]
                return tuple(new)

            init = (hst_scr[0, :, sre], hst_scr[0, :, sim], hst_scr[1, :, sre], hst_scr[1, :, sim])
            fin = lax.fori_loop(0, t_chunk, body, init, unroll=2)
            hst_scr[0, :, sre] = fin[0]
            hst_scr[0, :, sim] = fin[1]
            hst_scr[1, :, sre] = fin[2]
            hst_scr[1, :, sim] = fin[3]
    else:
        assert n_batch == 4
        n_blk = rows // 8
        low = lax.broadcasted_iota(jnp.int32, (8, lw), 0) < 4
        firsts = (low, jnp.logical_not(low))
        for j in range(SSM_LANES // lw):
            sre = slice(j * lw, (j + 1) * lw)
            sim = slice(SSM_LANES + j * lw, SSM_LANES + (j + 1) * lw)
            coef = []
            for d in range(2):
                ar = jnp.broadcast_to(abar[d][0][:, sre], (8, lw))
                ai = jnp.broadcast_to(abar[d][1][:, sre], (8, lw))
                coef.append((ar, ai, jnp.where(firsts[d], ar, ar * ar - ai * ai),
                             jnp.where(firsts[d], ai, 2.0 * ar * ai)))

            def body(i, carry, sre=sre, sim=sim, coef=coef):
                new = []
                for d in range(2):
                    pr, pi = carry[2 * d], carry[2 * d + 1]
                    ar, ai, cr, ci = coef[d]
                    first = firsts[d]
                    blk = i if d == 0 else n_blk - 1 - i
                    r0 = pl.multiple_of(blk * 8, 8)
                    vr = bu_scr[d, pl.ds(r0, 8), sre]
                    vi = bu_scr[d, pl.ds(r0, 8), sim]
                    sr = jnp.where(first, 0.0, pltpu.roll(vr, 4, axis=0))
                    si = jnp.where(first, 0.0, pltpu.roll(vi, 4, axis=0))
                    nr = cr * pr - ci * pi + (vr + (ar * sr - ai * si))
                    ni = cr * pi + ci * pr + (vi + (ar * si + ai * sr))
                    bu_scr[d, pl.ds(r0, 8), sre] = nr
                    bu_scr[d, pl.ds(r0, 8), sim] = ni
                    new += [jnp.where(first, pltpu.roll(nr, 4, axis=0), nr),
                            jnp.where(first, pltpu.roll(ni, 4, axis=0), ni)]
                return tuple(new)

            init = (hst_scr[0, :, sre], hst_scr[0, :, sim], hst_scr[1, :, sre], hst_scr[1, :, sim])
            fin = lax.fori_loop(0, n_blk, body, init, unroll=2)
            hst_scr[0, :, sre] = fin[0]
            hst_scr[0, :, sim] = fin[1]
            hst_scr[1, :, sre] = fin[2]
            hst_scr[1, :, sim] = fin[3]

    for d, y_ref in enumerate((yf_ref, yb_ref)):
        y = _dot_nt(bu_scr[d].astype(BF16), cblk_scr[d])
        y_ref[...] = pltpu.einshape("(tb)f->btf", y, b=n_batch)

    if emit_state:
        @pl.when(step == pl.num_programs(0) - 1)
        def _():
            hfin_ref[...] = hst_scr[...]


def _ssm_scan(u_tok, h0, h0_spec, wts, layer, n_batch, seq, t_chunk, emit_state):
    n_steps = seq // t_chunk
    rows = n_batch * t_chunk
    h_rows = h0.shape[-2]
    fwd_blk = lambda k: (0, k, 0)
    bwd_blk = lambda k: (0, n_steps - 1 - k, 0)
    chunk = (n_batch, t_chunk, SSM_WIDTH)

    def par(shape):
        return _layer_spec((2,) + shape, layer)

    out_shape = [jax.ShapeDtypeStruct((n_batch, seq, SSM_WIDTH), F32)] * 2
    out_specs = [pl.BlockSpec(chunk, fwd_blk), pl.BlockSpec(chunk, bwd_blk)]
    if emit_state:
        out_shape.append(jax.ShapeDtypeStruct((2, h_rows, 2 * SSM_LANES), F32))
        out_specs.append(_const_spec((2, h_rows, 2 * SSM_LANES)))
    u3 = u_tok.reshape(n_batch, seq, SSM_WIDTH)
    return pl.pallas_call(
        functools.partial(_ssm_kernel, n_batch=n_batch, emit_state=emit_state),
        grid=(n_steps,),
        in_specs=[pl.BlockSpec(chunk, fwd_blk), pl.BlockSpec(chunk, bwd_blk), h0_spec, par((3, SSM_LANES)),
                  par((SSM_WIDTH, SSM_STATE)), par((SSM_WIDTH, SSM_STATE)),
                  par((SSM_WIDTH, SSM_STATE)), par((SSM_WIDTH, SSM_STATE))],
        out_specs=out_specs,
        out_shape=out_shape,
        scratch_shapes=[pltpu.VMEM((2, SSM_WIDTH, 2 * SSM_LANES), BF16),
                        pltpu.VMEM((2, SSM_WIDTH, 2 * SSM_LANES), BF16),
                        pltpu.VMEM((2, h_rows, 2 * SSM_LANES), F32),
                        pltpu.VMEM((2, rows, 2 * SSM_LANES), F32)],
        compiler_params=pltpu.CompilerParams(vmem_limit_bytes=VMEM_LIMIT),
        name="ssm_b%d" % n_batch,
    )(u3, u3, h0, wts["ssm_par"], wts["ssm_b_re"], wts["ssm_b_im"], wts["ssm_c_re"], wts["ssm_c_im"])


def _out_mlp_kernel(x_ref, oa_ref, yf_ref, yb_ref, u_ref, mod_ref, vec_ref, wout_ref, wglu_ref, w1_ref, w2_ref,
                    y_ref, *, final):
    g1 = mod_ref[:, 2 * D_MODEL:3 * D_MODEL]
    sh2 = mod_ref[:, 3 * D_MODEL:4 * D_MODEL]
    sc2 = mod_ref[:, 4 * D_MODEL:5 * D_MODEL]
    g2 = mod_ref[:, 5 * D_MODEL:6 * D_MODEL]
    ys = yf_ref[...] + yb_ref[...] + u_ref[...] * vec_ref[:, VEC_SSMD:VEC_SSMD + SSM_WIDTH]
    z = _dot(_gelu_tanh(ys).astype(BF16), wglu_ref[...])
    o_ssm = (z[:, 0:SSM_WIDTH] * _sigmoid(z[:, SSM_WIDTH:2 * SSM_WIDTH])).astype(BF16)
    mix = (_dot(oa_ref[:, 0:512], wout_ref[0:512, :]) + _dot(o_ssm, wout_ref[512:768, :])
           + _dot(oa_ref[:, 512:O_COLS], wout_ref[768:1024, :]))
    x1 = x_ref[...] + g1 * mix
    ms = jnp.mean(x1 * x1, axis=-1, keepdims=True)
    h = x1 * lax.rsqrt(ms + EPS) * vec_ref[:, VEC_N2G:VEC_N2G + D_MODEL]
    h = h * (1.0 + sc2) + sh2
    a = jnp.maximum(_dot(h.astype(BF16), w1_ref[...]), 0.0)
    x2 = x1 + g2 * _dot((a * a).astype(BF16), w2_ref[...])
    if final:
        ms2 = jnp.mean(x2 * x2, axis=-1, keepdims=True)
        x2 = x2 * lax.rsqrt(ms2 + EPS) * vec_ref[:, VEC_FNG:VEC_FNG + D_MODEL]
    y_ref[...] = x2


def _out_mlp(x2d, o_attn, y_fwd, y_bwd, u, mod3, layer, wts, n_batch, seq, latent, final):
    tm = TOKEN_TILE
    tpb = seq // tm
    n_tok = n_batch * seq
    if latent:
        mod_row = lambda i: (layer * MOD_ROWS + 1 + i // tpb, 0, 0)
    else:
        mod_row = lambda i: (layer * MOD_ROWS, 0, 0)
    tile = lambda i: (i, 0)
    return pl.pallas_call(
        functools.partial(_out_mlp_kernel, final=final),
        grid=(n_tok // tm,),
        in_specs=[
            pl.BlockSpec((tm, D_MODEL), tile),
            pl.BlockSpec((tm, O_COLS), tile),
            pl.BlockSpec((tm, SSM_WIDTH), tile),
            pl.BlockSpec((tm, SSM_WIDTH), tile),
            pl.BlockSpec((tm, SSM_WIDTH), tile),
            pl.BlockSpec((None, 1, N_MOD * D_MODEL), mod_row),
            _layer_spec((1, VEC_END), layer),
            _layer_spec((D_MODEL, D_MODEL), layer, single_buffer=True),
            _layer_spec((SSM_WIDTH, 2 * SSM_WIDTH), layer, single_buffer=True),
            _layer_spec((D_MODEL, D_FF), layer, single_buffer=True),
            _layer_spec((D_FF, D_MODEL), layer, single_buffer=True),
        ],
        out_specs=pl.BlockSpec((tm, D_MODEL), tile),
        out_shape=jax.ShapeDtypeStruct((n_tok, D_MODEL), F32),
        compiler_params=pltpu.CompilerParams(vmem_limit_bytes=VMEM_LIMIT),
        name="out_mlp_%s%s" % ("lat" if latent else "ctx", "_final" if final else ""),
    )(x2d, o_attn, y_fwd, y_bwd, u, mod3, wts["vecs"], wts["w_out"], wts["w_glu"], wts["w1"], wts["w2"])


def _rope_tables(t_len, chunk, n_chunks):
    n = chunk // 4
    rows = t_len // GRID_W
    row = np.repeat(np.arange(rows), GRID_W).astype(np.float32)
    col = np.tile(np.arange(GRID_W), rows).astype(np.float32)
    freq = (np.float32(ROPE_THETA) ** (-np.arange(n, dtype=np.float32) / np.float32(n))).astype(np.float32)
    ang_r = (row[:, None] * freq).astype(np.float32)
    ang_c = (col[:, None] * freq).astype(np.float32)
    cr, sr, cc, sc = np.cos(ang_r), np.sin(ang_r), np.cos(ang_c), np.sin(ang_c)
    z = np.zeros_like(cr)
    tabs = np.stack([np.concatenate([cr, cr, cc, cc], axis=-1),
                     np.concatenate([-sr, z, -sc, z], axis=-1),
                     np.concatenate([z, sr, z, sc], axis=-1)]).astype(np.float32)
    return np.tile(tabs, (1, 1, n_chunks))


def _mla_rope_tables(t_len):
    tabs = _rope_tables(t_len, MLA_ROPE, 1)
    ident = np.stack([np.ones((t_len, MLA_NOPE), np.float32), np.zeros((t_len, MLA_NOPE), np.float32),
                      np.zeros((t_len, MLA_NOPE), np.float32)])
    tail = ident[:, :, :MLA_BLK - MLA_QK]
    return np.concatenate([ident, tabs, tail], axis=-1)


def _pack_vectors(p, n_layers):
    def rows(a):
        return a.reshape(n_layers, 1, -1)

    fng = jnp.broadcast_to(p["final_norm_g"].reshape(1, 1, D_MODEL), (n_layers, 1, D_MODEL))
    return jnp.concatenate([
        rows(p["norm1_g"]), rows(p["norm2_g"]), fng,
        jnp.tile(rows(p["gqa_qn_g"]), (1, 1, GQA_HEADS)),
        jnp.tile(rows(p["gqa_kn_g"]), (1, 1, GQA_KV_HEADS)),
        rows(p["mla_qn_g"]), jnp.zeros((n_layers, 1, CQ_PAD - MLA_Q_RANK), F32),
        rows(p["mla_kvn_g"]),
        jnp.tile(rows(p["diff_subln_g"]), (1, 1, DIFF_HEADS)),
        rows(p["ssm_d"]),
        rows(p["diff_lq1"]), rows(p["diff_lk1"]), rows(p["diff_lq2"]), rows(p["diff_lk2"]),
    ], axis=-1)


def kernel(x_prompt, x_sample, cache_diff_k, cache_diff_v, cache_gqa_k, cache_gqa_v, cache_mla_ckv, cache_mla_krope, state_ssm_re, state_ssm_im, c, c_ctx, norm1_g, norm2_g, w_ada, b_ada, w_in, w_out, diff_lq1, diff_lk1, diff_lq2, diff_lk2, diff_subln_g, gqa_qn_g, gqa_kn_g, ssm_a_re, ssm_a_im, ssm_log_dt, ssm_b_re, ssm_b_im, ssm_c_re, ssm_c_im, ssm_d, ssm_w_glu, mla_qn_g, mla_kvn_g, mla_w_uq, mla_w_ukv, mlp_w1, mlp_w2, final_norm_g):
    n_layers = w_in.shape[0]
    n_ctx, seq, _ = x_prompt.shape
    n_lat, t_lat, _ = x_sample.shape
    past = cache_diff_k.shape[2]
    assert n_lat + 1 <= MOD_ROWS and n_lat == 4 and n_ctx % 8 == 0
    assert seq == TOKEN_TILE and t_lat % TOKEN_TILE == 0

    p = dict(norm1_g=norm1_g, norm2_g=norm2_g, final_norm_g=final_norm_g, gqa_qn_g=gqa_qn_g, gqa_kn_g=gqa_kn_g,
             mla_qn_g=mla_qn_g, mla_kvn_g=mla_kvn_g, diff_subln_g=diff_subln_g, ssm_d=ssm_d, diff_lq1=diff_lq1,
             diff_lk1=diff_lk1, diff_lq2=diff_lq2, diff_lk2=diff_lk2)
    ssm_rows = SSM_GROUPS * SSM_GROUP
    wts = {
        "vecs": _pack_vectors(p, n_layers),
        "w_in_t": jnp.swapaxes(w_in, 1, 2),
        "w_uq": mla_w_uq, "w_ukv": mla_w_ukv,
        "w_out": w_out.astype(BF16), "w1": mlp_w1.astype(BF16), "w2": mlp_w2.astype(BF16),
        "w_glu": ssm_w_glu.astype(BF16),
        "ssm_par": jnp.stack([ssm_a_re.reshape(n_layers, 2, SSM_LANES), ssm_a_im.reshape(n_layers, 2, SSM_LANES),
                              jnp.repeat(ssm_log_dt, SSM_STATE, axis=-1)], axis=2),
        "ssm_b_re": jnp.swapaxes(ssm_b_re, 3, 4).reshape(n_layers, 2, ssm_rows, SSM_STATE),
        "ssm_b_im": jnp.swapaxes(ssm_b_im, 3, 4).reshape(n_layers, 2, ssm_rows, SSM_STATE),
        "ssm_c_re": ssm_c_re.reshape(n_layers, 2, ssm_rows, SSM_STATE),
        "ssm_c_im": ssm_c_im.reshape(n_layers, 2, ssm_rows, SSM_STATE),
    }

    cond = jnp.concatenate([c_ctx[None], c, jnp.zeros((MOD_ROWS - 1 - n_lat, D_MODEL), F32)], axis=0)
    mod3 = _adaln(cond, w_ada, b_ada).reshape(n_layers * MOD_ROWS, 1, N_MOD * D_MODEL)

    rope_tabs = (jnp.asarray(_rope_tables(t_lat, DIFF_QK, 256 // DIFF_QK)),
                 jnp.asarray(_rope_tables(t_lat, GQA_HEAD_DIM, 256 // GQA_HEAD_DIM)),
                 jnp.asarray(_mla_rope_tables(t_lat)))
    caches = (jnp.transpose(cache_diff_k, (0, 1, 3, 4, 2)).reshape(n_lat, n_layers, 256, past),
              jnp.transpose(cache_diff_v, (0, 1, 3, 4, 2)).reshape(n_lat, n_layers, 256, past),
              jnp.transpose(cache_gqa_k, (0, 1, 3, 4, 2)).reshape(n_lat, n_layers, 128, past),
              jnp.transpose(cache_gqa_v, (0, 1, 3, 4, 2)).reshape(n_lat, n_layers, 128, past),
              cache_mla_ckv,
              jnp.swapaxes(cache_mla_krope, 2, 3))

    lam_inits = [0.8 - 0.6 * math.exp(-0.3 * l) for l in range(n_layers)]

    x = x_prompt.reshape(n_ctx * seq, D_MODEL)
    zero_state = jnp.zeros((2, n_ctx, 2 * SSM_LANES), F32)
    zero_spec = _const_spec((2, n_ctx, 2 * SSM_LANES))
    kept = None
    states = []
    for l in range(n_layers):
        res = _in_proj(x, mod3, l, wts, n_ctx, seq, latent=False, prev_caches=kept)
        q, k, v, u = res[:4]
        kept = res[4:]
        o_attn = _attention(q, k, v, wts["vecs"], l, lam_inits[l], n_ctx, seq)
        y_fwd, y_bwd, h_fin = _ssm_scan(u, zero_state, zero_spec, wts, l, n_ctx, seq, CTX_SSM_CHUNK,
                                        emit_state=True)
        x = _out_mlp(x, o_attn, y_fwd.reshape(n_ctx * seq, SSM_WIDTH), y_bwd.reshape(n_ctx * seq, SSM_WIDTH), u,
                     mod3, l, wts, n_ctx, seq, latent=False, final=(l == n_layers - 1))
        states.append(h_fin)
    y_prompt = x.reshape(n_ctx, seq, D_MODEL)

    x = x_sample.reshape(n_lat * t_lat, D_MODEL)
    h0 = jnp.concatenate([state_ssm_re.reshape(n_lat, n_layers, 2, SSM_LANES),
                          state_ssm_im.reshape(n_lat, n_layers, 2, SSM_LANES)], axis=-1)
    h0 = jnp.transpose(h0, (1, 2, 0, 3))
    h0 = jnp.concatenate([h0, h0], axis=2)
    for l in range(n_layers):
        h0_spec = _layer_spec((2, 2 * n_lat, 2 * SSM_LANES), l)
        q, k, v, u = _in_proj(x, mod3, l, wts, n_lat, t_lat, latent=True, rope_tabs=rope_tabs)
        kc, vc = _cache_prep(caches, l, wts["w_ukv"])
        o_attn = _attention(q, k, v, wts["vecs"], l, lam_inits[l], n_lat, t_lat, k_cache=kc, v_cache=vc)
        y_fwd, y_bwd = _ssm_scan(u, h0, h0_spec, wts, l, n_lat, t_lat, LAT_SSM_CHUNK, emit_state=False)
        x = _out_mlp(x, o_attn, y_fwd.reshape(n_lat * t_lat, SSM_WIDTH), y_bwd.reshape(n_lat * t_lat, SSM_WIDTH), u,
                     mod3, l, wts, n_lat, t_lat, latent=True, final=(l == n_layers - 1))
    y_sample = x.reshape(n_lat, t_lat, D_MODEL)

    cdk, cdv, cgk, cgv, cckv, ckr = kept

    def token_major(a, heads, dim):
        return jnp.transpose(a.reshape(n_ctx, n_layers, heads, dim, seq), (0, 1, 4, 2, 3))

    st = jnp.transpose(jnp.stack(states), (2, 0, 1, 3))
    return (y_prompt, y_sample,
            token_major(cdk, DIFF_HEADS, 2 * DIFF_QK), token_major(cdv, DIFF_HEADS, DIFF_V),
            token_major(cgk, GQA_KV_HEADS, GQA_HEAD_DIM), token_major(cgv, GQA_KV_HEADS, GQA_HEAD_DIM),
            cckv, jnp.swapaxes(ckr, 2, 3),
            st[..., :SSM_LANES].reshape(n_ctx, n_layers, 2, SSM_GROUPS, SSM_STATE),
            st[..., SSM_LANES:].reshape(n_ctx, n_layers, 2, SSM_GROUPS, SSM_STATE))
```

```python
import functools
import math

import numpy as np
import jax
import jax.numpy as jnp
from jax import lax
from jax.experimental import pallas as pl
from jax.experimental.pallas import tpu as pltpu

F32 = jnp.float32
BF16 = jnp.bfloat16

D_MODEL = 1024
GRID_W = 64
ROPE_THETA = 10000.0
EPS = 1e-6
DIFF_HEADS, DIFF_QK, DIFF_V = 4, 32, 64
GQA_HEADS, GQA_KV_HEADS, GQA_HEAD_DIM = 4, 2, 64
SSM_WIDTH, SSM_GROUP, SSM_STATE = 256, 16, 64
SSM_GROUPS = SSM_WIDTH // SSM_GROUP
SSM_LANES = SSM_GROUPS * SSM_STATE
MLA_HEADS, MLA_Q_RANK, MLA_KV_RANK, MLA_NOPE, MLA_ROPE, MLA_V = 4, 192, 128, 64, 32, 64
MLA_QK = MLA_NOPE + MLA_ROPE
D_FF = 4 * D_MODEL
N_MOD = 6
LOG2E = math.log2(math.e)
IN_COLS = 1888
MOD_ROWS = 8

P_DQ, P_DK, P_DV, P_GQ, P_GK, P_GV, P_U, P_CQ, P_CKV, P_KR, P_END = (
    0, 256, 512, 768, 1024, 1152, 1280, 1536, 1792, 1920, 2048)
S_CQ_END, S_CKV, S_KR = 1728, 1728, 1856
CQ_PAD = P_CKV - P_CQ
MLA_BLK = 128
Q_DIFF, Q_GQA, Q_MLA, Q_COLS = 0, 256, 512, 1024
K_DIFF, K_GQA, K_MLA, K_COLS = 0, 256, 512, 1024
V_DIFF, V_GQA, V_MLA, V_COLS = 0, 256, 512, 1024
O_COLS = 768

VEC_N1G, VEC_N2G, VEC_FNG, VEC_QNG, VEC_KNG, VEC_MQNG, VEC_KVNG, VEC_SUBG, VEC_SSMD, VEC_LAM, VEC_END = (
    0, 1024, 2048, 3072, 3328, 3456, 3712, 3840, 4096, 4352, 4480)

TOKEN_TILE = 256
CTX_SSM_CHUNK = 32
LAT_SSM_CHUNK = 256
STACK_MAX_KEYS = 512
VMEM_LIMIT = 48 * 1024 * 1024


def _dot(a, b):
    return jnp.dot(a, b, preferred_element_type=F32)


def _dot_nt(a, b):
    return lax.dot_general(a, b, (((1,), (1,)), ((), ())), preferred_element_type=F32)


def _block_ones(width, seg):
    shift = seg.bit_length() - 1
    r = jnp.right_shift(lax.broadcasted_iota(jnp.int32, (width, width), 0), shift)
    c = jnp.right_shift(lax.broadcasted_iota(jnp.int32, (width, width), 1), shift)
    return jnp.where(r == c, 1.0, 0.0).astype(BF16)


def _seg_sum_sq(y, seg):
    ones = _block_ones(y.shape[-1], seg)
    sq = y * y
    hi = sq.astype(BF16)
    lo = (sq - hi.astype(F32)).astype(BF16)
    return _dot(hi, ones) + _dot(lo, ones)


def _rope(x, tab_ref, quarter):
    outs = []
    for j in range(x.shape[-1] // 128):
        sl = slice(j * 128, (j + 1) * 128)
        xs = x[:, sl]
        up = pltpu.roll(xs, 128 - quarter, axis=1)
        dn = pltpu.roll(xs, quarter, axis=1)
        outs.append(xs * tab_ref[0, :, sl] + up * tab_ref[1, :, sl] + dn * tab_ref[2, :, sl])
    return outs[0] if len(outs) == 1 else jnp.concatenate(outs, axis=-1)


def _sigmoid(x):
    return 1.0 / (1.0 + jnp.exp(-x))


def _lane_lt(width, bound):
    return lax.broadcasted_iota(jnp.int32, (1, width), 1) < bound


def _dup_halves(x):
    swapped = pltpu.roll(x, 64, axis=1)
    left = _lane_lt(128, 64)
    return jnp.where(left, x, swapped), jnp.where(left, swapped, x)


def _const_spec(shape):
    nd = len(shape)
    return pl.BlockSpec(shape, lambda *_: (0,) * nd)


def _layer_spec(shape, layer, single_buffer=False):
    nd = len(shape)
    kw = {"pipeline_mode": pl.Buffered(1)} if single_buffer else {}
    return pl.BlockSpec((None,) + tuple(shape), lambda *_: (layer,) + (0,) * nd, **kw)


def _adaln_kernel(cond_ref, w_ref, b_ref, o_ref):
    c = cond_ref[...]
    s = c * _sigmoid(c)
    o_ref[...] = _dot(s.astype(BF16), w_ref[...].astype(BF16)) + b_ref[...]


def _adaln(cond, w_ada, b_ada):
    n_layers = w_ada.shape[0]
    tn = 1024
    return pl.pallas_call(
        _adaln_kernel,
        grid=(n_layers, N_MOD * D_MODEL // tn),
        in_specs=[
            pl.BlockSpec((MOD_ROWS, D_MODEL), lambda l, j: (0, 0)),
            pl.BlockSpec((None, D_MODEL, tn), lambda l, j: (l, 0, j)),
            pl.BlockSpec((None, 1, tn), lambda l, j: (l, 0, j)),
        ],
        out_specs=pl.BlockSpec((None, MOD_ROWS, tn), lambda l, j: (l, 0, j)),
        out_shape=jax.ShapeDtypeStruct((n_layers, MOD_ROWS, N_MOD * D_MODEL), F32),
        compiler_params=pltpu.CompilerParams(vmem_limit_bytes=VMEM_LIMIT),
        name="adaln",
    )(cond, w_ada, b_ada.reshape(n_layers, 1, N_MOD * D_MODEL))


N_CACHE = 6


def _in_proj_kernel(*refs, latent, n_alias):
    x_ref, mod_ref, vec_ref, wint_ref, wuq_ref, wukv_ref = refs[:6]
    pos = 6
    if latent:
        rope_d_ref, rope_g_ref, rope_m_ref = refs[pos:pos + 3]
        pos += 3
    pos += n_alias
    q_ref, k_ref, v_ref, u_ref = refs[pos:pos + 4]
    pos += 4
    if not latent:
        cdk_ref, cdv_ref, cgk_ref, cgv_ref, cckv_ref, ckr_ref = refs[pos:pos + N_CACHE]
        pos += N_CACHE
    win_scr, wuq_scr, wukv_scr = refs[pos:]

    @pl.when(pl.program_id(0) == 0)
    def _():
        win_scr[0:S_CQ_END, :] = wint_ref[0:S_CQ_END, :].astype(BF16)
        win_scr[S_CQ_END:P_CKV, :] = jnp.zeros((P_CKV - S_CQ_END, D_MODEL), BF16)
        win_scr[P_CKV:P_KR, :] = wint_ref[S_CKV:S_KR, :].astype(BF16)
        win_scr[P_KR:P_KR + MLA_NOPE, :] = jnp.zeros((MLA_NOPE, D_MODEL), BF16)
        win_scr[P_KR + MLA_NOPE:P_KR + MLA_QK, :] = wint_ref[S_KR:IN_COLS, :].astype(BF16)
        win_scr[P_KR + MLA_QK:P_END, :] = jnp.zeros((P_END - P_KR - MLA_QK, D_MODEL), BF16)
        wuq_scr[...] = jnp.zeros(wuq_scr.shape, BF16)
        wuq = wuq_ref[...]
        for hd in range(MLA_HEADS):
            wuq_scr[0:MLA_Q_RANK, hd * MLA_BLK:hd * MLA_BLK + MLA_QK] = (
                wuq[:, hd * MLA_QK:(hd + 1) * MLA_QK].astype(BF16))
        wukv_scr[...] = wukv_ref[...].astype(BF16)

    x = x_ref[...]
    sh1 = mod_ref[:, 0:D_MODEL]
    sc1 = mod_ref[:, D_MODEL:2 * D_MODEL]
    ms = jnp.mean(x * x, axis=-1, keepdims=True)
    h = x * lax.rsqrt(ms + EPS) * vec_ref[:, VEC_N1G:VEC_N1G + D_MODEL]
    h = h * (1.0 + sc1) + sh1
    proj = _dot_nt(h.astype(BF16), win_scr[...])

    dq = proj[:, P_DQ:P_DK]
    dk = proj[:, P_DK:P_DV]
    dv = proj[:, P_DV:P_GQ]
    gq = proj[:, P_GQ:P_GK]
    gk = proj[:, P_GK:P_GV]
    gv = proj[:, P_GV:P_U]
    u_ref[...] = proj[:, P_U:P_CQ]
    cq = proj[:, P_CQ:P_CKV]
    ckv = proj[:, P_CKV:P_KR]
    krb = proj[:, P_KR:P_END]

    qng = vec_ref[:, VEC_QNG:VEC_QNG + 256]
    kng = vec_ref[:, VEC_KNG:VEC_KNG + 128]
    gq = gq * lax.rsqrt(_seg_sum_sq(gq, GQA_HEAD_DIM) * (1.0 / GQA_HEAD_DIM) + EPS) * qng
    gk = gk * lax.rsqrt(_seg_sum_sq(gk, GQA_HEAD_DIM) * (1.0 / GQA_HEAD_DIM) + EPS) * kng
    ckv_n = (ckv * lax.rsqrt(jnp.mean(ckv * ckv, axis=-1, keepdims=True) + EPS)
             * vec_ref[:, VEC_KVNG:VEC_KVNG + MLA_KV_RANK])
    cq_ms = jnp.sum(cq * cq, axis=-1, keepdims=True) * (1.0 / MLA_Q_RANK)
    cq_n = cq * lax.rsqrt(cq_ms + EPS) * vec_ref[:, VEC_MQNG:VEC_MQNG + CQ_PAD]
    mq = _dot(cq_n.astype(BF16), wuq_scr[...])
    kv = _dot(ckv_n.astype(BF16), wukv_scr[...])

    if not latent:
        cdk_ref[...] = dk.T
        cdv_ref[...] = dv.T
        cgk_ref[...] = gk.T
        cgv_ref[...] = gv.T
        cckv_ref[...] = ckv_n
        ckr_ref[...] = krb.T[MLA_NOPE:MLA_QK, :]
    else:
        dq = _rope(dq, rope_d_ref, DIFF_QK // 4)
        dk = _rope(dk, rope_d_ref, DIFF_QK // 4)
        gq = _rope(gq, rope_g_ref, GQA_HEAD_DIM // 4)
        gk = _rope(gk, rope_g_ref, GQA_HEAD_DIM // 4)
        mq = jnp.concatenate(
            [_rope(mq[:, hd * MLA_BLK:(hd + 1) * MLA_BLK], rope_m_ref, MLA_ROPE // 4)
             for hd in range(MLA_HEADS)], axis=-1)
        krb = _rope(krb, rope_m_ref, MLA_ROPE // 4)

    q_ref[:, Q_DIFF:Q_GQA] = (dq * (LOG2E * DIFF_QK ** -0.5)).astype(BF16)
    q_ref[:, Q_GQA:Q_MLA] = (gq * (LOG2E * GQA_HEAD_DIM ** -0.5)).astype(BF16)
    q_ref[:, Q_MLA:Q_COLS] = (mq * (LOG2E * MLA_QK ** -0.5)).astype(BF16)
    k_ref[:, K_DIFF:K_GQA] = dk.astype(BF16)
    gk0, gk1 = _dup_halves(gk)
    k_ref[:, K_GQA:K_GQA + 128] = gk0.astype(BF16)
    k_ref[:, K_GQA + 128:K_MLA] = gk1.astype(BF16)
    nope = _lane_lt(MLA_BLK, MLA_NOPE)
    for hd in range(MLA_HEADS):
        blk = slice(hd * MLA_BLK, (hd + 1) * MLA_BLK)
        k_ref[:, K_MLA + hd * MLA_BLK:K_MLA + (hd + 1) * MLA_BLK] = jnp.where(nope, kv[:, blk], krb).astype(BF16)
    v_ref[:, V_DIFF:V_GQA] = dv.astype(BF16)
    gv0, gv1 = _dup_halves(gv)
    v_ref[:, V_GQA:V_GQA + 128] = gv0.astype(BF16)
    v_ref[:, V_GQA + 128:V_MLA] = gv1.astype(BF16)
    v_ref[:, V_MLA:V_COLS] = kv.astype(BF16)


def _in_proj(x2d, mod3, layer, wts, n_batch, seq, latent, rope_tabs=None, prev_caches=None):
    tm = TOKEN_TILE
    tpb = seq // tm
    n_tok = n_batch * seq
    n_layers = wts["vecs"].shape[0]
    if latent:
        mod_row = lambda i: (layer * MOD_ROWS + 1 + i // tpb, 0, 0)
    else:
        mod_row = lambda i: (layer * MOD_ROWS, 0, 0)
    tile = lambda i: (i, 0)
    in_specs = [
        pl.BlockSpec((tm, D_MODEL), tile),
        pl.BlockSpec((None, 1, N_MOD * D_MODEL), mod_row),
        _layer_spec((1, VEC_END), layer),
        _layer_spec((IN_COLS, D_MODEL), layer, single_buffer=True),
        _layer_spec((MLA_Q_RANK, MLA_HEADS * MLA_QK), layer, single_buffer=True),
        _layer_spec((MLA_KV_RANK, MLA_HEADS * MLA_BLK), layer, single_buffer=True),
    ]
    args = [x2d, mod3, wts["vecs"], wts["w_in_t"], wts["w_uq"], wts["w_ukv"]]
    out_shape = [
        jax.ShapeDtypeStruct((n_tok, Q_COLS), BF16),
        jax.ShapeDtypeStruct((n_tok, K_COLS), BF16),
        jax.ShapeDtypeStruct((n_tok, V_COLS), BF16),
        jax.ShapeDtypeStruct((n_tok, SSM_WIDTH), F32),
    ]
    out_specs = [
        pl.BlockSpec((tm, Q_COLS), tile),
        pl.BlockSpec((tm, K_COLS), tile),
        pl.BlockSpec((tm, V_COLS), tile),
        pl.BlockSpec((tm, SSM_WIDTH), tile),
    ]
    aliases = {}
    n_alias = 0
    if latent:
        pos = lambda i: (0, i % tpb, 0)
        in_specs += [pl.BlockSpec((3, tm, 256), pos), pl.BlockSpec((3, tm, 256), pos),
                     pl.BlockSpec((3, tm, 128), pos)]
        args += list(rope_tabs)
    else:
        assert tpb == 1
        cache_dims = [(256, seq), (256, seq), (128, seq), (128, seq), (seq, MLA_KV_RANK), (MLA_ROPE, seq)]
        for dims in cache_dims:
            out_shape.append(jax.ShapeDtypeStruct((n_batch, n_layers) + dims, F32))
            out_specs.append(pl.BlockSpec((None, None) + dims, lambda i: (i, layer, 0, 0)))
        if prev_caches is not None:
            n_alias = N_CACHE
            for j, arr in enumerate(prev_caches):
                aliases[len(args)] = 4 + j
                in_specs.append(pl.BlockSpec(memory_space=pl.ANY))
                args.append(arr)
    return pl.pallas_call(
        functools.partial(_in_proj_kernel, latent=latent, n_alias=n_alias),
        grid=(n_tok // tm,),
        in_specs=in_specs,
        out_specs=out_specs,
        out_shape=out_shape,
        input_output_aliases=aliases,
        scratch_shapes=[pltpu.VMEM((P_END, D_MODEL), BF16),
                        pltpu.VMEM((CQ_PAD, MLA_HEADS * MLA_BLK), BF16),
                        pltpu.VMEM((MLA_KV_RANK, MLA_HEADS * MLA_BLK), BF16)],
        compiler_params=pltpu.CompilerParams(vmem_limit_bytes=VMEM_LIMIT),
        name="in_proj_lat" if latent else "in_proj_ctx",
    )(*args)


def _cache_prep_kernel(cdk_ref, cdv_ref, cgk_ref, cgv_ref, cckv_ref, ckr_ref, wukv_ref, k_ref, v_ref):
    past = k_ref.shape[0]
    kv = _dot(cckv_ref[...].astype(BF16), wukv_ref[...].astype(BF16))
    krb = jnp.concatenate([jnp.zeros((MLA_NOPE, past), F32), ckr_ref[...],
                           jnp.zeros((MLA_BLK - MLA_QK, past), F32)], axis=0).T
    k_ref[:, K_DIFF:K_GQA] = cdk_ref[...].T.astype(BF16)
    gk0, gk1 = _dup_halves(cgk_ref[...].T)
    k_ref[:, K_GQA:K_GQA + 128] = gk0.astype(BF16)
    k_ref[:, K_GQA + 128:K_MLA] = gk1.astype(BF16)
    nope = _lane_lt(MLA_BLK, MLA_NOPE)
    for hd in range(MLA_HEADS):
        blk = slice(hd * MLA_BLK, (hd + 1) * MLA_BLK)
        k_ref[:, K_MLA + hd * MLA_BLK:K_MLA + (hd + 1) * MLA_BLK] = jnp.where(nope, kv[:, blk], krb).astype(BF16)
    v_ref[:, V_DIFF:V_GQA] = cdv_ref[...].T.astype(BF16)
    gv0, gv1 = _dup_halves(cgv_ref[...].T)
    v_ref[:, V_GQA:V_GQA + 128] = gv0.astype(BF16)
    v_ref[:, V_GQA + 128:V_MLA] = gv1.astype(BF16)
    v_ref[:, V_MLA:V_COLS] = kv.astype(BF16)


def _cache_prep(caches, layer, w_ukv):
    cdk, cdv, cgk, cgv, cckv, ckr = caches
    n_batch, _, _, past = cdk.shape

    def spec(rows, cols):
        return pl.BlockSpec((None, None, rows, cols), lambda b: (b, layer, 0, 0))

    return pl.pallas_call(
        _cache_prep_kernel,
        grid=(n_batch,),
        in_specs=[spec(256, past), spec(256, past), spec(128, past), spec(128, past),
                  spec(past, MLA_KV_RANK), spec(MLA_ROPE, past),
                  _layer_spec((MLA_KV_RANK, MLA_HEADS * MLA_BLK), layer)],
        out_specs=[pl.BlockSpec((None, past, K_COLS), lambda b: (b, 0, 0)),
                   pl.BlockSpec((None, past, V_COLS), lambda b: (b, 0, 0))],
        out_shape=[jax.ShapeDtypeStruct((n_batch, past, K_COLS), BF16),
                   jax.ShapeDtypeStruct((n_batch, past, V_COLS), BF16)],
        name="cache_prep",
    )(cdk, cdv, cgk, cgv, cckv, ckr, w_ukv)


def _exp_scores(s):
    m = jnp.max(s, axis=-1, keepdims=True)
    e = jnp.exp2(s - m)
    return e.astype(BF16), 1.0 / jnp.sum(e, axis=-1, keepdims=True)


def _lane_mask(width, lo, hi):
    lane = lax.broadcasted_iota(jnp.int32, (1, width), 1)
    return jnp.where((lane >= lo) & (lane < hi), 1.0, 0.0).astype(BF16)


def _attn_kernel(*refs, lam_init, n_q, past):
    if past:
        q_ref, kc_ref, ko_ref, vc_ref, vo_ref, vec_ref, o_ref, k_ref, v_ref = refs

        @pl.when(pl.program_id(0) % n_q == 0)
        def _():
            k_ref[0:past, :] = kc_ref[...]
            k_ref[past:, :] = ko_ref[...]
            v_ref[0:past, :] = vc_ref[...]
            v_ref[past:, :] = vo_ref[...]
    else:
        q_ref, k_ref, v_ref, vec_ref, o_ref = refs
    tq = q_ref.shape[0]
    stack_rows = k_ref.shape[0] <= STACK_MAX_KEYS
    lq1 = vec_ref[:, VEC_LAM:VEC_LAM + 32]
    lk1 = vec_ref[:, VEC_LAM + 32:VEC_LAM + 64]
    lq2 = vec_ref[:, VEC_LAM + 64:VEC_LAM + 96]
    lk2 = vec_ref[:, VEC_LAM + 96:VEC_LAM + 128]
    lam = (jnp.exp(jnp.sum(lq1 * lk1, axis=-1, keepdims=True))
           - jnp.exp(jnp.sum(lq2 * lk2, axis=-1, keepdims=True)) + lam_init)

    def masked_heads(qv, k, v, masks):
        if stack_rows:
            e, r = _exp_scores(_dot_nt(jnp.concatenate([qv * m for m in masks], axis=0), k))
            o = _dot(e, v) * r
            return [o[j * tq:(j + 1) * tq] for j in range(len(masks))]
        outs = []
        for m in masks:
            e, r = _exp_scores(_dot_nt(qv * m, k))
            outs.append(_dot(e, v) * r)
        return outs

    parts = masked_heads(q_ref[:, Q_DIFF:Q_GQA], k_ref[:, K_DIFF:K_GQA], v_ref[:, V_DIFF:V_GQA],
                         [_lane_mask(256, j * DIFF_QK, (j + 1) * DIFF_QK) for j in range(2 * DIFF_HEADS)])
    lane = lax.broadcasted_iota(jnp.int32, (1, 256), 1)
    acc = jnp.zeros((tq, 256), F32)
    for hd in range(DIFF_HEADS):
        head = (lane >= hd * DIFF_V) & (lane < (hd + 1) * DIFF_V)
        acc = jnp.where(head, parts[2 * hd] - lam * parts[2 * hd + 1], acc)
    ms = _seg_sum_sq(acc, DIFF_V) * (1.0 / DIFF_V)
    subg = vec_ref[:, VEC_SUBG:VEC_SUBG + 256]
    o_ref[:, 0:256] = (acc * lax.rsqrt(ms + EPS) * subg * (1.0 - lam_init)).astype(o_ref.dtype)

    left = _lane_lt(128, GQA_HEAD_DIM)
    for blk in range(GQA_KV_HEADS):
        o_l, o_r = masked_heads(q_ref[:, Q_GQA + blk * 128:Q_GQA + (blk + 1) * 128],
                                k_ref[:, K_GQA + blk * 128:K_GQA + (blk + 1) * 128],
                                v_ref[:, V_GQA + blk * 128:V_GQA + (blk + 1) * 128],
                                [_lane_mask(128, 0, GQA_HEAD_DIM), _lane_mask(128, GQA_HEAD_DIM, 2 * GQA_HEAD_DIM)])
        o_ref[:, 256 + blk * 128:256 + (blk + 1) * 128] = jnp.where(left, o_l, o_r).astype(o_ref.dtype)

    upper = _lane_mask(MLA_BLK, MLA_NOPE, MLA_BLK)
    for pair in range(MLA_HEADS // 2):
        outs = []
        for hd in (2 * pair, 2 * pair + 1):
            mq = q_ref[:, Q_MLA + hd * MLA_BLK:Q_MLA + (hd + 1) * MLA_BLK]
            mk = k_ref[:, K_MLA + hd * MLA_BLK:K_MLA + (hd + 1) * MLA_BLK]
            mv = v_ref[:, V_MLA + hd * MLA_BLK:V_MLA + (hd + 1) * MLA_BLK]
            e, r = _exp_scores(_dot_nt(mq, mk))
            outs.append(_dot(e, mv * upper) * r)
        both = pltpu.roll(outs[0], MLA_V, axis=1) + outs[1]
        o_ref[:, 512 + pair * 128:512 + (pair + 1) * 128] = both.astype(o_ref.dtype)


def _attention(q, k_own, v_own, vecs, layer, lam_init, n_batch, tq_total, k_cache=None, v_cache=None):
    tq = TOKEN_TILE
    nq = tq_total // tq
    past = 0 if k_cache is None else k_cache.shape[1]
    tk = past + tq_total
    own = lambda cols: pl.BlockSpec((tq_total, cols), lambda i: (i // nq, 0))
    in_specs = [pl.BlockSpec((tq, Q_COLS), lambda i: (i, 0))]
    scratch = []
    if past:
        cached = lambda cols: pl.BlockSpec((None, past, cols), lambda i: (i // nq, 0, 0))
        in_specs += [cached(K_COLS), own(K_COLS), cached(V_COLS), own(V_COLS)]
        args = [q, k_cache, k_own, v_cache, v_own]
        scratch = [pltpu.VMEM((tk, K_COLS), BF16), pltpu.VMEM((tk, V_COLS), BF16)]
    else:
        in_specs += [own(K_COLS), own(V_COLS)]
        args = [q, k_own, v_own]
    in_specs.append(_layer_spec((1, VEC_END), layer))
    args.append(vecs)
    return pl.pallas_call(
        functools.partial(_attn_kernel, lam_init=lam_init, n_q=nq, past=past),
        grid=(n_batch * nq,),
        in_specs=in_specs,
        out_specs=pl.BlockSpec((tq, O_COLS), lambda i: (i, 0)),
        out_shape=jax.ShapeDtypeStruct((n_batch * tq_total, O_COLS), BF16),
        scratch_shapes=scratch,
        compiler_params=pltpu.CompilerParams(vmem_limit_bytes=VMEM_LIMIT),
        name="attention_tk%d" % tk,
    )(*args)


def _gelu_tanh(x):
    return 0.5 * x * (1.0 + jnp.tanh(math.sqrt(2.0 / math.pi) * (x + 0.044715 * (x * x * x))))


def _block_diag_lanes(blocks):
    pair = jnp.concatenate([blocks, blocks], axis=-1)
    tiled = jnp.concatenate([pair] * (SSM_LANES // 128), axis=-1)
    r = jnp.right_shift(lax.broadcasted_iota(jnp.int32, tiled.shape, 0), 4)
    c = jnp.right_shift(lax.broadcasted_iota(jnp.int32, tiled.shape, 1), 6)
    return jnp.where(r == c, tiled, 0.0)


def _ssm_kernel(*refs, n_batch, emit_state):
    uf_ref, ub_ref, h0_ref, par_ref, bre_ref, bim_ref, cre_ref, cim_ref, yf_ref, yb_ref = refs[:10]
    pos = 10
    if emit_state:
        hfin_ref = refs[pos]
        pos += 1
    bbar_scr, cblk_scr, hst_scr, bu_scr, hb_scr = refs[pos:]

    step = pl.program_id(0)
    t_chunk = uf_ref.shape[1]
    rows = n_batch * t_chunk
    abar = []
    for d in range(2):
        a_re = par_ref[d, 0:1, :]
        a_im = par_ref[d, 1:2, :]
        dt = jnp.exp(par_ref[d, 2:3, :])
        mag = jnp.exp(a_re * dt)
        abar.append((mag * jnp.cos(a_im * dt), mag * jnp.sin(a_im * dt)))

    @pl.when(step == 0)
    def _():
        for d in range(2):
            a_re = par_ref[d, 0:1, :]
            a_im = par_ref[d, 1:2, :]
            abr, abi = abar[d]
            den = a_re * a_re + a_im * a_im
            xr = abr - 1.0
            cr = (xr * a_re + abi * a_im) / den
            ci = (abi * a_re - xr * a_im) / den
            bre = _block_diag_lanes(bre_ref[d])
            bim = _block_diag_lanes(bim_ref[d])
            bbar_scr[d, :, 0:SSM_LANES] = (cr * bre - ci * bim).astype(BF16)
            bbar_scr[d, :, SSM_LANES:2 * SSM_LANES] = (cr * bim + ci * bre).astype(BF16)
            cblk_scr[d, :, 0:SSM_LANES] = _block_diag_lanes(cre_ref[d]).astype(BF16)
            cblk_scr[d, :, SSM_LANES:2 * SSM_LANES] = (-_block_diag_lanes(cim_ref[d])).astype(BF16)
        hst_scr[...] = h0_ref[...]

    for d, u_ref in enumerate((uf_ref, ub_ref)):
        u_tm = jnp.swapaxes(u_ref[...], 0, 1).reshape(rows, SSM_WIDTH)
        bu_scr[d] = _dot(u_tm.astype(BF16), bbar_scr[d])

    lw = 128
    if n_batch % 8 == 0:
        for j in range(SSM_LANES // lw):
            sre = slice(j * lw, (j + 1) * lw)
            sim = slice(SSM_LANES + j * lw, SSM_LANES + (j + 1) * lw)
            coef = [(jnp.broadcast_to(abar[d][0][:, sre], (n_batch, lw)),
                     jnp.broadcast_to(abar[d][1][:, sre], (n_batch, lw))) for d in range(2)]

            def body(i, carry, sre=sre, sim=sim, coef=coef):
                new = []
                for d in range(2):
                    hr, hi = carry[2 * d], carry[2 * d + 1]
                    ar, ai = coef[d]
                    t = i if d == 0 else t_chunk - 1 - i
                    r0 = pl.multiple_of(t * n_batch, n_batch)
                    nr = ar * hr - ai * hi + bu_scr[d, pl.ds(r0, n_batch), sre]
                    ni = ar * hi + ai * hr + bu_scr[d, pl.ds(r0, n_batch), sim]
                    hb_scr[d, pl.ds(r0, n_batch), sre] = nr.astype(BF16)
                    hb_scr[d, pl.ds(r0, n_batch), sim] = ni.astype(BF16)
                    new += [nr, ni]
                return tuple(new)

            init = (hst_scr[0, :, sre], hst_scr[0, :, sim], hst_scr[1, :, sre], hst_scr[1, :, sim])
            fin = lax.fori_loop(0, t_chunk, body, init, unroll=2)
            hst_scr[0, :, sre] = fin[0]
            hst_scr[0, :, sim] = fin[1]
            hst_scr[1, :, sre] = fin[2]
            hst_scr[1, :, sim] = fin[3]
    else:
        assert n_batch == 4
        n_blk = rows // 8
        low = lax.broadcasted_iota(jnp.int32, (8, lw), 0) < 4
        firsts = (low, jnp.logical_not(low))
        for j in range(SSM_LANES // lw):
            sre = slice(j * lw, (j + 1) * lw)
            sim = slice(SSM_LANES + j * lw, SSM_LANES + (j + 1) * lw)
            coef = []
            for d in range(2):
                ar = jnp.broadcast_to(abar[d][0][:, sre], (8, lw))
                ai = jnp.broadcast_to(abar[d][1][:, sre], (8, lw))
                coef.append((ar, ai, jnp.where(firsts[d], ar, ar * ar - ai * ai),
                             jnp.where(firsts[d], ai, 2.0 * ar * ai)))

            def body(i, carry, sre=sre, sim=sim, coef=coef):
                new = []
                for d in range(2):
                    pr, pi = carry[2 * d], carry[2 * d + 1]
                    ar, ai, cr, ci = coef[d]
                    first = firsts[d]
                    r16 = pl.multiple_of((2 * i if d == 0 else n_blk - 2 - 2 * i) * 8, 16)
                    tile_r, tile_i = [None, None], [None, None]
                    for half in ((0, 1) if d == 0 else (1, 0)):
                        vr = bu_scr[d, pl.ds(r16 + 8 * half, 8), sre]
                        vi = bu_scr[d, pl.ds(r16 + 8 * half, 8), sim]
                        sr = jnp.where(first, 0.0, pltpu.roll(vr, 4, axis=0))
                        si = jnp.where(first, 0.0, pltpu.roll(vi, 4, axis=0))
                        nr = cr * pr - ci * pi + (vr + (ar * sr - ai * si))
                        ni = cr * pi + ci * pr + (vi + (ar * si + ai * sr))
                        tile_r[half], tile_i[half] = nr, ni
                        pr = jnp.where(first, pltpu.roll(nr, 4, axis=0), nr)
                        pi = jnp.where(first, pltpu.roll(ni, 4, axis=0), ni)
                    hb_scr[d, pl.ds(r16, 16), sre] = jnp.concatenate(tile_r, axis=0).astype(BF16)
                    hb_scr[d, pl.ds(r16, 16), sim] = jnp.concatenate(tile_i, axis=0).astype(BF16)
                    new += [pr, pi]
                return tuple(new)

            init = (hst_scr[0, :, sre], hst_scr[0, :, sim], hst_scr[1, :, sre], hst_scr[1, :, sim])
            fin = lax.fori_loop(0, n_blk // 2, body, init, unroll=2)
            hst_scr[0, :, sre] = fin[0]
            hst_scr[0, :, sim] = fin[1]
            hst_scr[1, :, sre] = fin[2]
            hst_scr[1, :, sim] = fin[3]

    for d, y_ref in enumerate((yf_ref, yb_ref)):
        y = _dot_nt(hb_scr[d], cblk_scr[d])
        y_ref[...] = jnp.swapaxes(y.reshape(t_chunk, n_batch, SSM_WIDTH), 0, 1)

    if emit_state:
        @pl.when(step == pl.num_programs(0) - 1)
        def _():
            hfin_ref[...] = hst_scr[...]


def _ssm_scan(u_tok, h0, h0_spec, wts, layer, n_batch, seq, t_chunk, emit_state):
    n_steps = seq // t_chunk
    rows = n_batch * t_chunk
    h_rows = h0.shape[-2]
    fwd_blk = lambda k: (0, k, 0)
    bwd_blk = lambda k: (0, n_steps - 1 - k, 0)
    chunk = (n_batch, t_chunk, SSM_WIDTH)

    def par(shape):
        return _layer_spec((2,) + shape, layer)

    out_shape = [jax.ShapeDtypeStruct((n_batch, seq, SSM_WIDTH), F32)] * 2
    out_specs = [pl.BlockSpec(chunk, fwd_blk), pl.BlockSpec(chunk, bwd_blk)]
    if emit_state:
        out_shape.append(jax.ShapeDtypeStruct((2, h_rows, 2 * SSM_LANES), F32))
        out_specs.append(_const_spec((2, h_rows, 2 * SSM_LANES)))
    u3 = u_tok.reshape(n_batch, seq, SSM_WIDTH)
    return pl.pallas_call(
        functools.partial(_ssm_kernel, n_batch=n_batch, emit_state=emit_state),
        grid=(n_steps,),
        in_specs=[pl.BlockSpec(chunk, fwd_blk), pl.BlockSpec(chunk, bwd_blk), h0_spec, par((3, SSM_LANES)),
                  par((SSM_WIDTH, SSM_STATE)), par((SSM_WIDTH, SSM_STATE)),
                  par((SSM_WIDTH, SSM_STATE)), par((SSM_WIDTH, SSM_STATE))],
        out_specs=out_specs,
        out_shape=out_shape,
        scratch_shapes=[pltpu.VMEM((2, SSM_WIDTH, 2 * SSM_LANES), BF16),
                        pltpu.VMEM((2, SSM_WIDTH, 2 * SSM_LANES), BF16),
                        pltpu.VMEM((2, h_rows, 2 * SSM_LANES), F32),
                        pltpu.VMEM((2, rows, 2 * SSM_LANES), F32),
                        pltpu.VMEM((2, rows, 2 * SSM_LANES), BF16)],
        compiler_params=pltpu.CompilerParams(vmem_limit_bytes=VMEM_LIMIT),
        name="ssm_b%d" % n_batch,
    )(u3, u3, h0, wts["ssm_par"], wts["ssm_b_re"], wts["ssm_b_im"], wts["ssm_c_re"], wts["ssm_c_im"])


def _out_mlp_kernel(x_ref, oa_ref, yf_ref, yb_ref, u_ref, mod_ref, vec_ref, wout_ref, wglu_ref, w1_ref, w2_ref,
                    y_ref, *, final):
    g1 = mod_ref[:, 2 * D_MODEL:3 * D_MODEL]
    sh2 = mod_ref[:, 3 * D_MODEL:4 * D_MODEL]
    sc2 = mod_ref[:, 4 * D_MODEL:5 * D_MODEL]
    g2 = mod_ref[:, 5 * D_MODEL:6 * D_MODEL]
    ys = yf_ref[...] + yb_ref[...] + u_ref[...] * vec_ref[:, VEC_SSMD:VEC_SSMD + SSM_WIDTH]
    z = _dot(_gelu_tanh(ys).astype(BF16), wglu_ref[...])
    o_ssm = (z[:, 0:SSM_WIDTH] * _sigmoid(z[:, SSM_WIDTH:2 * SSM_WIDTH])).astype(BF16)
    mix = (_dot(oa_ref[:, 0:512], wout_ref[0:512, :]) + _dot(o_ssm, wout_ref[512:768, :])
           + _dot(oa_ref[:, 512:O_COLS], wout_ref[768:1024, :]))
    x1 = x_ref[...] + g1 * mix
    ms = jnp.mean(x1 * x1, axis=-1, keepdims=True)
    h = x1 * lax.rsqrt(ms + EPS) * vec_ref[:, VEC_N2G:VEC_N2G + D_MODEL]
    h = h * (1.0 + sc2) + sh2
    a = jnp.maximum(_dot(h.astype(BF16), w1_ref[...]), 0.0)
    x2 = x1 + g2 * _dot((a * a).astype(BF16), w2_ref[...])
    if final:
        ms2 = jnp.mean(x2 * x2, axis=-1, keepdims=True)
        x2 = x2 * lax.rsqrt(ms2 + EPS) * vec_ref[:, VEC_FNG:VEC_FNG + D_MODEL]
    y_ref[...] = x2


def _out_mlp(x2d, o_attn, y_fwd, y_bwd, u, mod3, layer, wts, n_batch, seq, latent, final):
    tm = TOKEN_TILE
    tpb = seq // tm
    n_tok = n_batch * seq
    if latent:
        mod_row = lambda i: (layer * MOD_ROWS + 1 + i // tpb, 0, 0)
    else:
        mod_row = lambda i: (layer * MOD_ROWS, 0, 0)
    tile = lambda i: (i, 0)
    return pl.pallas_call(
        functools.partial(_out_mlp_kernel, final=final),
        grid=(n_tok // tm,),
        in_specs=[
            pl.BlockSpec((tm, D_MODEL), tile),
            pl.BlockSpec((tm, O_COLS), tile),
            pl.BlockSpec((tm, SSM_WIDTH), tile),
            pl.BlockSpec((tm, SSM_WIDTH), tile),
            pl.BlockSpec((tm, SSM_WIDTH), tile),
            pl.BlockSpec((None, 1, N_MOD * D_MODEL), mod_row),
            _layer_spec((1, VEC_END), layer),
            _layer_spec((D_MODEL, D_MODEL), layer, single_buffer=True),
            _layer_spec((SSM_WIDTH, 2 * SSM_WIDTH), layer, single_buffer=True),
            _layer_spec((D_MODEL, D_FF), layer, single_buffer=True),
            _layer_spec((D_FF, D_MODEL), layer, single_buffer=True),
        ],
        out_specs=pl.BlockSpec((tm, D_MODEL), tile),
        out_shape=jax.ShapeDtypeStruct((n_tok, D_MODEL), F32),
        compiler_params=pltpu.CompilerParams(vmem_limit_bytes=VMEM_LIMIT),
        name="out_mlp_%s%s" % ("lat" if latent else "ctx", "_final" if final else ""),
    )(x2d, o_attn, y_fwd, y_bwd, u, mod3, wts["vecs"], wts["w_out"], wts["w_glu"], wts["w1"], wts["w2"])


def _rope_tables(t_len, chunk, n_chunks):
    n = chunk // 4
    rows = t_len // GRID_W
    row = np.repeat(np.arange(rows), GRID_W).astype(np.float32)
    col = np.tile(np.arange(GRID_W), rows).astype(np.float32)
    freq = (np.float32(ROPE_THETA) ** (-np.arange(n, dtype=np.float32) / np.float32(n))).astype(np.float32)
    ang_r = (row[:, None] * freq).astype(np.float32)
    ang_c = (col[:, None] * freq).astype(np.float32)
    cr, sr, cc, sc = np.cos(ang_r), np.sin(ang_r), np.cos(ang_c), np.sin(ang_c)
    z = np.zeros_like(cr)
    tabs = np.stack([np.concatenate([cr, cr, cc, cc], axis=-1),
                     np.concatenate([-sr, z, -sc, z], axis=-1),
                     np.concatenate([z, sr, z, sc], axis=-1)]).astype(np.float32)
    return np.tile(tabs, (1, 1, n_chunks))


def _mla_rope_tables(t_len):
    tabs = _rope_tables(t_len, MLA_ROPE, 1)
    ident = np.stack([np.ones((t_len, MLA_NOPE), np.float32), np.zeros((t_len, MLA_NOPE), np.float32),
                      np.zeros((t_len, MLA_NOPE), np.float32)])
    tail = ident[:, :, :MLA_BLK - MLA_QK]
    return np.concatenate([ident, tabs, tail], axis=-1)


def _pack_vectors(p, n_layers):
    def rows(a):
        return a.reshape(n_layers, 1, -1)

    fng = jnp.broadcast_to(p["final_norm_g"].reshape(1, 1, D_MODEL), (n_layers, 1, D_MODEL))
    return jnp.concatenate([
        rows(p["norm1_g"]), rows(p["norm2_g"]), fng,
        jnp.tile(rows(p["gqa_qn_g"]), (1, 1, GQA_HEADS)),
        jnp.tile(rows(p["gqa_kn_g"]), (1, 1, GQA_KV_HEADS)),
        rows(p["mla_qn_g"]), jnp.zeros((n_layers, 1, CQ_PAD - MLA_Q_RANK), F32),
        rows(p["mla_kvn_g"]),
        jnp.tile(rows(p["diff_subln_g"]), (1, 1, DIFF_HEADS)),
        rows(p["ssm_d"]),
        rows(p["diff_lq1"]), rows(p["diff_lk1"]), rows(p["diff_lq2"]), rows(p["diff_lk2"]),
    ], axis=-1)


def kernel(x_prompt, x_sample, cache_diff_k, cache_diff_v, cache_gqa_k, cache_gqa_v, cache_mla_ckv, cache_mla_krope, state_ssm_re, state_ssm_im, c, c_ctx, norm1_g, norm2_g, w_ada, b_ada, w_in, w_out, diff_lq1, diff_lk1, diff_lq2, diff_lk2, diff_subln_g, gqa_qn_g, gqa_kn_g, ssm_a_re, ssm_a_im, ssm_log_dt, ssm_b_re, ssm_b_im, ssm_c_re, ssm_c_im, ssm_d, ssm_w_glu, mla_qn_g, mla_kvn_g, mla_w_uq, mla_w_ukv, mlp_w1, mlp_w2, final_norm_g):
    n_layers = w_in.shape[0]
    n_ctx, seq, _ = x_prompt.shape
    n_lat, t_lat, _ = x_sample.shape
    past = cache_diff_k.shape[2]
    assert n_lat + 1 <= MOD_ROWS and n_lat == 4 and n_ctx % 8 == 0
    assert seq == TOKEN_TILE and t_lat % TOKEN_TILE == 0

    p = dict(norm1_g=norm1_g, norm2_g=norm2_g, final_norm_g=final_norm_g, gqa_qn_g=gqa_qn_g, gqa_kn_g=gqa_kn_g,
             mla_qn_g=mla_qn_g, mla_kvn_g=mla_kvn_g, diff_subln_g=diff_subln_g, ssm_d=ssm_d, diff_lq1=diff_lq1,
             diff_lk1=diff_lk1, diff_lq2=diff_lq2, diff_lk2=diff_lk2)
    ssm_rows = SSM_GROUPS * SSM_GROUP
    wts = {
        "vecs": _pack_vectors(p, n_layers),
        "w_in_t": jnp.swapaxes(w_in, 1, 2),
        "w_uq": mla_w_uq, "w_ukv": mla_w_ukv,
        "w_out": w_out.astype(BF16), "w1": mlp_w1.astype(BF16), "w2": mlp_w2.astype(BF16),
        "w_glu": ssm_w_glu.astype(BF16),
        "ssm_par": jnp.stack([ssm_a_re.reshape(n_layers, 2, SSM_LANES), ssm_a_im.reshape(n_layers, 2, SSM_LANES),
                              jnp.repeat(ssm_log_dt, SSM_STATE, axis=-1)], axis=2),
        "ssm_b_re": jnp.swapaxes(ssm_b_re, 3, 4).reshape(n_layers, 2, ssm_rows, SSM_STATE),
        "ssm_b_im": jnp.swapaxes(ssm_b_im, 3, 4).reshape(n_layers, 2, ssm_rows, SSM_STATE),
        "ssm_c_re": ssm_c_re.reshape(n_layers, 2, ssm_rows, SSM_STATE),
        "ssm_c_im": ssm_c_im.reshape(n_layers, 2, ssm_rows, SSM_STATE),
    }

    cond = jnp.concatenate([c_ctx[None], c, jnp.zeros((MOD_ROWS - 1 - n_lat, D_MODEL), F32)], axis=0)
    mod3 = _adaln(cond, w_ada, b_ada).reshape(n_layers * MOD_ROWS, 1, N_MOD * D_MODEL)

    rope_tabs = (jnp.asarray(_rope_tables(t_lat, DIFF_QK, 256 // DIFF_QK)),
                 jnp.asarray(_rope_tables(t_lat, GQA_HEAD_DIM, 256 // GQA_HEAD_DIM)),
                 jnp.asarray(_mla_rope_tables(t_lat)))
    caches = (jnp.transpose(cache_diff_k, (0, 1, 3, 4, 2)).reshape(n_lat, n_layers, 256, past),
              jnp.transpose(cache_diff_v, (0, 1, 3, 4, 2)).reshape(n_lat, n_layers, 256, past),
              jnp.transpose(cache_gqa_k, (0, 1, 3, 4, 2)).reshape(n_lat, n_layers, 128, past),
              jnp.transpose(cache_gqa_v, (0, 1, 3, 4, 2)).reshape(n_lat, n_layers, 128, past),
              cache_mla_ckv,
              jnp.swapaxes(cache_mla_krope, 2, 3))

    lam_inits = [0.8 - 0.6 * math.exp(-0.3 * l) for l in range(n_layers)]

    x = x_prompt.reshape(n_ctx * seq, D_MODEL)
    zero_state = jnp.zeros((2, n_ctx, 2 * SSM_LANES), F32)
    zero_spec = _const_spec((2, n_ctx, 2 * SSM_LANES))
    kept = None
    states = []
    for l in range(n_layers):
        res = _in_proj(x, mod3, l, wts, n_ctx, seq, latent=False, prev_caches=kept)
        q, k, v, u = res[:4]
        kept = res[4:]
        o_attn = _attention(q, k, v, wts["vecs"], l, lam_inits[l], n_ctx, seq)
        y_fwd, y_bwd, h_fin = _ssm_scan(u, zero_state, zero_spec, wts, l, n_ctx, seq, CTX_SSM_CHUNK,
                                        emit_state=True)
        x = _out_mlp(x, o_attn, y_fwd.reshape(n_ctx * seq, SSM_WIDTH), y_bwd.reshape(n_ctx * seq, SSM_WIDTH), u,
                     mod3, l, wts, n_ctx, seq, latent=False, final=(l == n_layers - 1))
        states.append(h_fin)
    y_prompt = x.reshape(n_ctx, seq, D_MODEL)

    x = x_sample.reshape(n_lat * t_lat, D_MODEL)
    h0 = jnp.concatenate([state_ssm_re.reshape(n_lat, n_layers, 2, SSM_LANES),
                          state_ssm_im.reshape(n_lat, n_layers, 2, SSM_LANES)], axis=-1)
    h0 = jnp.transpose(h0, (1, 2, 0, 3))
    h0 = jnp.concatenate([h0, h0], axis=2)
    for l in range(n_layers):
        h0_spec = _layer_spec((2, 2 * n_lat, 2 * SSM_LANES), l)
        q, k, v, u = _in_proj(x, mod3, l, wts, n_lat, t_lat, latent=True, rope_tabs=rope_tabs)
        kc, vc = _cache_prep(caches, l, wts["w_ukv"])
        o_attn = _attention(q, k, v, wts["vecs"], l, lam_inits[l], n_lat, t_lat, k_cache=kc, v_cache=vc)
        y_fwd, y_bwd = _ssm_scan(u, h0, h0_spec, wts, l, n_lat, t_lat, LAT_SSM_CHUNK, emit_state=False)
        x = _out_mlp(x, o_attn, y_fwd.reshape(n_lat * t_lat, SSM_WIDTH), y_bwd.reshape(n_lat * t_lat, SSM_WIDTH), u,
                     mod3, l, wts, n_lat, t_lat, latent=True, final=(l == n_layers - 1))
    y_sample = x.reshape(n_lat, t_lat, D_MODEL)

    cdk, cdv, cgk, cgv, cckv, ckr = kept

    def token_major(a, heads, dim):
        return jnp.transpose(a.reshape(n_ctx, n_layers, heads, dim, seq), (0, 1, 4, 2, 3))

    st = jnp.transpose(jnp.stack(states), (2, 0, 1, 3))
    return (y_prompt, y_sample,
            token_major(cdk, DIFF_HEADS, 2 * DIFF_QK), token_major(cdv, DIFF_HEADS, DIFF_V),
            token_major(cgk, GQA_KV_HEADS, GQA_HEAD_DIM), token_major(cgv, GQA_KV_HEADS, GQA_HEAD_DIM),
            cckv, jnp.swapaxes(ckr, 2, 3),
            st[..., :SSM_LANES].reshape(n_ctx, n_layers, 2, SSM_GROUPS, SSM_STATE),
            st[..., SSM_LANES:].reshape(n_ctx, n_layers, 2, SSM_GROUPS, SSM_STATE))
```

```python
import functools
import math

import numpy as np
import jax
import jax.numpy as jnp
from jax import lax
from jax.experimental import pallas as pl
from jax.experimental.pallas import tpu as pltpu

F32 = jnp.float32
BF16 = jnp.bfloat16

D_MODEL = 1024
GRID_W = 64
ROPE_THETA = 10000.0
EPS = 1e-6
DIFF_HEADS, DIFF_QK, DIFF_V = 4, 32, 64
GQA_HEADS, GQA_KV_HEADS, GQA_HEAD_DIM = 4, 2, 64
SSM_WIDTH, SSM_GROUP, SSM_STATE = 256, 16, 64
SSM_GROUPS = SSM_WIDTH // SSM_GROUP
SSM_LANES = SSM_GROUPS * SSM_STATE
MLA_HEADS, MLA_Q_RANK, MLA_KV_RANK, MLA_NOPE, MLA_ROPE, MLA_V = 4, 192, 128, 64, 32, 64
MLA_QK = MLA_NOPE + MLA_ROPE
D_FF = 4 * D_MODEL
N_MOD = 6
LOG2E = math.log2(math.e)
IN_COLS = 1888
MOD_ROWS = 8

P_DQ, P_DK, P_DV, P_GQ, P_GK, P_GV, P_U, P_CQ, P_CKV, P_KR, P_END = (
    0, 256, 512, 768, 1024, 1152, 1280, 1536, 1792, 1920, 2048)
S_CQ_END, S_CKV, S_KR = 1728, 1728, 1856
CQ_PAD = P_CKV - P_CQ
MLA_BLK = 128
Q_DIFF, Q_GQA, Q_MLA, Q_COLS = 0, 256, 512, 1024
K_DIFF, K_GQA, K_MLA, K_COLS = 0, 256, 512, 1024
V_DIFF, V_GQA, V_MLA, V_COLS = 0, 256, 512, 1024
O_COLS = 768

VEC_N1G, VEC_N2G, VEC_FNG, VEC_QNG, VEC_KNG, VEC_MQNG, VEC_KVNG, VEC_SUBG, VEC_SSMD, VEC_LAM, VEC_END = (
    0, 1024, 2048, 3072, 3328, 3456, 3712, 3840, 4096, 4352, 4480)

TOKEN_TILE = 256
CTX_SSM_CHUNK = 32
LAT_SSM_CHUNK = 256
STACK_MAX_KEYS = 512
VMEM_LIMIT = 48 * 1024 * 1024


def _dot(a, b):
    return jnp.dot(a, b, preferred_element_type=F32)


def _dot_nt(a, b):
    return lax.dot_general(a, b, (((1,), (1,)), ((), ())), preferred_element_type=F32)


def _block_ones(width, seg):
    shift = seg.bit_length() - 1
    r = jnp.right_shift(lax.broadcasted_iota(jnp.int32, (width, width), 0), shift)
    c = jnp.right_shift(lax.broadcasted_iota(jnp.int32, (width, width), 1), shift)
    return jnp.where(r == c, 1.0, 0.0).astype(BF16)


def _seg_sum_sq(y, seg):
    ones = _block_ones(y.shape[-1], seg)
    sq = y * y
    hi = sq.astype(BF16)
    lo = (sq - hi.astype(F32)).astype(BF16)
    return _dot(hi, ones) + _dot(lo, ones)


def _rope(x, tab_ref, quarter):
    outs = []
    for j in range(x.shape[-1] // 128):
        sl = slice(j * 128, (j + 1) * 128)
        xs = x[:, sl]
        up = pltpu.roll(xs, 128 - quarter, axis=1)
        dn = pltpu.roll(xs, quarter, axis=1)
        outs.append(xs * tab_ref[0, :, sl] + up * tab_ref[1, :, sl] + dn * tab_ref[2, :, sl])
    return outs[0] if len(outs) == 1 else jnp.concatenate(outs, axis=-1)


def _sigmoid(x):
    return 1.0 / (1.0 + jnp.exp(-x))


def _lane_lt(width, bound):
    return lax.broadcasted_iota(jnp.int32, (1, width), 1) < bound


def _dup_halves(x):
    swapped = pltpu.roll(x, 64, axis=1)
    left = _lane_lt(128, 64)
    return jnp.where(left, x, swapped), jnp.where(left, swapped, x)


def _const_spec(shape):
    nd = len(shape)
    return pl.BlockSpec(shape, lambda *_: (0,) * nd)


def _layer_spec(shape, layer, single_buffer=False):
    nd = len(shape)
    kw = {"pipeline_mode": pl.Buffered(1)} if single_buffer else {}
    return pl.BlockSpec((None,) + tuple(shape), lambda *_: (layer,) + (0,) * nd, **kw)


def _adaln_kernel(cond_ref, w_ref, b_ref, o_ref):
    c = cond_ref[...]
    s = c * _sigmoid(c)
    o_ref[...] = _dot(s.astype(BF16), w_ref[...].astype(BF16)) + b_ref[...]


def _adaln(cond, w_ada, b_ada):
    n_layers = w_ada.shape[0]
    tn = 1024
    return pl.pallas_call(
        _adaln_kernel,
        grid=(n_layers, N_MOD * D_MODEL // tn),
        in_specs=[
            pl.BlockSpec((MOD_ROWS, D_MODEL), lambda l, j: (0, 0)),
            pl.BlockSpec((None, D_MODEL, tn), lambda l, j: (l, 0, j)),
            pl.BlockSpec((None, 1, tn), lambda l, j: (l, 0, j)),
        ],
        out_specs=pl.BlockSpec((None, MOD_ROWS, tn), lambda l, j: (l, 0, j)),
        out_shape=jax.ShapeDtypeStruct((n_layers, MOD_ROWS, N_MOD * D_MODEL), F32),
        compiler_params=pltpu.CompilerParams(vmem_limit_bytes=VMEM_LIMIT),
        name="adaln",
    )(cond, w_ada, b_ada.reshape(n_layers, 1, N_MOD * D_MODEL))


N_CACHE = 6


def _in_proj_kernel(*refs, latent, n_alias, layer):
    x_ref, mod_ref, vec_ref, wint_ref, wuq_ref, wukv_ref = refs[:6]
    pos = 6
    if latent:
        rope_d_ref, rope_g_ref, rope_m_ref = refs[pos:pos + 3]
        pos += 3
    pos += n_alias
    q_ref, k_ref, v_ref, u_ref = refs[pos:pos + 4]
    pos += 4
    if not latent:
        cdk_ref, cdv_ref, cgk_ref, cgv_ref, cckv_ref, ckr_ref = refs[pos:pos + N_CACHE]
        pos += N_CACHE
    win_scr, wuq_scr, wukv_scr = refs[pos:]

    @pl.when(pl.program_id(0) == 0)
    def _():
        aligned = (S_CQ_END // 128) * 128
        for r0 in range(0, aligned, 256):
            r1 = min(r0 + 256, aligned)
            win_scr[:, r0:r1] = wint_ref[r0:r1, :].T.astype(BF16)
        tail = wint_ref[aligned:aligned + 128, :].T
        win_scr[:, aligned:aligned + 128] = jnp.where(_lane_lt(128, S_CQ_END - aligned), tail, 0.0).astype(BF16)
        win_scr[:, P_CKV:P_KR] = wint_ref[S_CKV:S_KR, :].T.astype(BF16)
        kr_rows = jnp.concatenate([jnp.zeros((MLA_NOPE, D_MODEL), F32), wint_ref[S_KR:IN_COLS, :],
                                   jnp.zeros((MLA_BLK - MLA_QK, D_MODEL), F32)], axis=0)
        win_scr[:, P_KR:P_END] = kr_rows.T.astype(BF16)
        wuq_scr[...] = jnp.zeros(wuq_scr.shape, BF16)
        wuq = wuq_ref[...]
        for hd in range(MLA_HEADS):
            wuq_scr[0:MLA_Q_RANK, hd * MLA_BLK:hd * MLA_BLK + MLA_QK] = (
                wuq[:, hd * MLA_QK:(hd + 1) * MLA_QK].astype(BF16))
        wukv_scr[...] = wukv_ref[...].astype(BF16)

    x = x_ref[...]
    sh1 = mod_ref[:, 0:D_MODEL]
    sc1 = mod_ref[:, D_MODEL:2 * D_MODEL]
    ms = jnp.mean(x * x, axis=-1, keepdims=True)
    h = x * lax.rsqrt(ms + EPS) * vec_ref[:, VEC_N1G:VEC_N1G + D_MODEL]
    h = h * (1.0 + sc1) + sh1
    proj = _dot(h.astype(BF16), win_scr[...])

    dq = proj[:, P_DQ:P_DK]
    dk = proj[:, P_DK:P_DV]
    dv = proj[:, P_DV:P_GQ]
    gq = proj[:, P_GQ:P_GK]
    gk = proj[:, P_GK:P_GV]
    gv = proj[:, P_GV:P_U]
    u_ref[...] = proj[:, P_U:P_CQ]
    cq = proj[:, P_CQ:P_CKV]
    ckv = proj[:, P_CKV:P_KR]
    krb = proj[:, P_KR:P_END]

    qng = vec_ref[:, VEC_QNG:VEC_QNG + 256]
    kng = vec_ref[:, VEC_KNG:VEC_KNG + 128]
    gq = gq * lax.rsqrt(_seg_sum_sq(gq, GQA_HEAD_DIM) * (1.0 / GQA_HEAD_DIM) + EPS) * qng
    gk = gk * lax.rsqrt(_seg_sum_sq(gk, GQA_HEAD_DIM) * (1.0 / GQA_HEAD_DIM) + EPS) * kng
    ckv_n = (ckv * lax.rsqrt(jnp.mean(ckv * ckv, axis=-1, keepdims=True) + EPS)
             * vec_ref[:, VEC_KVNG:VEC_KVNG + MLA_KV_RANK])
    cq_ms = jnp.sum(cq * cq, axis=-1, keepdims=True) * (1.0 / MLA_Q_RANK)
    cq_n = cq * lax.rsqrt(cq_ms + EPS) * vec_ref[:, VEC_MQNG:VEC_MQNG + CQ_PAD]
    mq = _dot(cq_n.astype(BF16), wuq_scr[...])
    kv = _dot(ckv_n.astype(BF16), wukv_scr[...])

    if not latent:
        def keep(ref, val):
            for l in range(ref.shape[0]):
                ref[l] = val if ref.shape[0] == 1 or l == layer else jnp.zeros_like(val)

        keep(cdk_ref, dk.T)
        keep(cdv_ref, dv.T)
        keep(cgk_ref, gk.T)
        keep(cgv_ref, gv.T)
        keep(cckv_ref, ckv_n)
        keep(ckr_ref, krb.T[MLA_NOPE:MLA_QK, :])
    else:
        dq = _rope(dq, rope_d_ref, DIFF_QK // 4)
        dk = _rope(dk, rope_d_ref, DIFF_QK // 4)
        gq = _rope(gq, rope_g_ref, GQA_HEAD_DIM // 4)
        gk = _rope(gk, rope_g_ref, GQA_HEAD_DIM // 4)
        mq = jnp.concatenate(
            [_rope(mq[:, hd * MLA_BLK:(hd + 1) * MLA_BLK], rope_m_ref, MLA_ROPE // 4)
             for hd in range(MLA_HEADS)], axis=-1)
        krb = _rope(krb, rope_m_ref, MLA_ROPE // 4)

    q_ref[:, Q_DIFF:Q_GQA] = (dq * (LOG2E * DIFF_QK ** -0.5)).astype(BF16)
    q_ref[:, Q_GQA:Q_MLA] = (gq * (LOG2E * GQA_HEAD_DIM ** -0.5)).astype(BF16)
    q_ref[:, Q_MLA:Q_COLS] = (mq * (LOG2E * MLA_QK ** -0.5)).astype(BF16)
    k_ref[:, K_DIFF:K_GQA] = dk.astype(BF16)
    gk0, gk1 = _dup_halves(gk)
    k_ref[:, K_GQA:K_GQA + 128] = gk0.astype(BF16)
    k_ref[:, K_GQA + 128:K_MLA] = gk1.astype(BF16)
    nope = _lane_lt(MLA_BLK, MLA_NOPE)
    for hd in range(MLA_HEADS):
        blk = slice(hd * MLA_BLK, (hd + 1) * MLA_BLK)
        k_ref[:, K_MLA + hd * MLA_BLK:K_MLA + (hd + 1) * MLA_BLK] = jnp.where(nope, kv[:, blk], krb).astype(BF16)
    v_ref[:, V_DIFF:V_GQA] = dv.astype(BF16)
    gv0, gv1 = _dup_halves(gv)
    v_ref[:, V_GQA:V_GQA + 128] = gv0.astype(BF16)
    v_ref[:, V_GQA + 128:V_MLA] = gv1.astype(BF16)
    v_ref[:, V_MLA:V_COLS] = kv.astype(BF16)


def _in_proj(x2d, mod3, layer, wts, n_batch, seq, latent, rope_tabs=None, prev_caches=None):
    tm = TOKEN_TILE
    tpb = seq // tm
    n_tok = n_batch * seq
    n_layers = wts["vecs"].shape[0]
    if latent:
        mod_row = lambda i: (layer * MOD_ROWS + 1 + i // tpb, 0, 0)
    else:
        mod_row = lambda i: (layer * MOD_ROWS, 0, 0)
    tile = lambda i: (i, 0)
    in_specs = [
        pl.BlockSpec((tm, D_MODEL), tile),
        pl.BlockSpec((None, 1, N_MOD * D_MODEL), mod_row),
        _layer_spec((1, VEC_END), layer),
        _layer_spec((IN_COLS, D_MODEL), layer, single_buffer=True),
        _layer_spec((MLA_Q_RANK, MLA_HEADS * MLA_QK), layer, single_buffer=True),
        _layer_spec((MLA_KV_RANK, MLA_HEADS * MLA_BLK), layer, single_buffer=True),
    ]
    args = [x2d, mod3, wts["vecs"], wts["w_in_t"], wts["w_uq"], wts["w_ukv"]]
    out_shape = [
        jax.ShapeDtypeStruct((n_tok, Q_COLS), BF16),
        jax.ShapeDtypeStruct((n_tok, K_COLS), BF16),
        jax.ShapeDtypeStruct((n_tok, V_COLS), BF16),
        jax.ShapeDtypeStruct((n_tok, SSM_WIDTH), F32),
    ]
    out_specs = [
        pl.BlockSpec((tm, Q_COLS), tile),
        pl.BlockSpec((tm, K_COLS), tile),
        pl.BlockSpec((tm, V_COLS), tile),
        pl.BlockSpec((tm, SSM_WIDTH), tile),
    ]
    aliases = {}
    n_alias = 0
    if latent:
        pos = lambda i: (0, i % tpb, 0)
        in_specs += [pl.BlockSpec((3, tm, 256), pos), pl.BlockSpec((3, tm, 256), pos),
                     pl.BlockSpec((3, tm, 128), pos)]
        args += list(rope_tabs)
    else:
        assert tpb == 1
        cache_dims = [(256, seq), (256, seq), (128, seq), (128, seq), (seq, MLA_KV_RANK), (MLA_ROPE, seq)]
        for dims in cache_dims:
            out_shape.append(jax.ShapeDtypeStruct((n_batch, n_layers) + dims, F32))
            if prev_caches is None:
                out_specs.append(pl.BlockSpec((None, n_layers) + dims, lambda i: (i, 0, 0, 0)))
            else:
                out_specs.append(pl.BlockSpec((None, 1) + dims, lambda i: (i, layer, 0, 0)))
        if prev_caches is not None:
            n_alias = N_CACHE
            for j, arr in enumerate(prev_caches):
                aliases[len(args)] = 4 + j
                in_specs.append(pl.BlockSpec(memory_space=pl.ANY))
                args.append(arr)
    return pl.pallas_call(
        functools.partial(_in_proj_kernel, latent=latent, n_alias=n_alias, layer=layer),
        grid=(n_tok // tm,),
        in_specs=in_specs,
        out_specs=out_specs,
        out_shape=out_shape,
        input_output_aliases=aliases,
        scratch_shapes=[pltpu.VMEM((D_MODEL, P_END), BF16),
                        pltpu.VMEM((CQ_PAD, MLA_HEADS * MLA_BLK), BF16),
                        pltpu.VMEM((MLA_KV_RANK, MLA_HEADS * MLA_BLK), BF16)],
        compiler_params=pltpu.CompilerParams(vmem_limit_bytes=VMEM_LIMIT),
        name="in_proj_lat" if latent else "in_proj_ctx",
    )(*args)


def _cache_prep_kernel(cdk_ref, cdv_ref, cgk_ref, cgv_ref, cckv_ref, ckr_ref, wukv_ref, k_ref, v_ref):
    past = k_ref.shape[0]
    kv = _dot(cckv_ref[...].astype(BF16), wukv_ref[...].astype(BF16))
    krb = jnp.concatenate([jnp.zeros((MLA_NOPE, past), F32), ckr_ref[...],
                           jnp.zeros((MLA_BLK - MLA_QK, past), F32)], axis=0).T
    k_ref[:, K_DIFF:K_GQA] = cdk_ref[...].T.astype(BF16)
    gk0, gk1 = _dup_halves(cgk_ref[...].T)
    k_ref[:, K_GQA:K_GQA + 128] = gk0.astype(BF16)
    k_ref[:, K_GQA + 128:K_MLA] = gk1.astype(BF16)
    nope = _lane_lt(MLA_BLK, MLA_NOPE)
    for hd in range(MLA_HEADS):
        blk = slice(hd * MLA_BLK, (hd + 1) * MLA_BLK)
        k_ref[:, K_MLA + hd * MLA_BLK:K_MLA + (hd + 1) * MLA_BLK] = jnp.where(nope, kv[:, blk], krb).astype(BF16)
    v_ref[:, V_DIFF:V_GQA] = cdv_ref[...].T.astype(BF16)
    gv0, gv1 = _dup_halves(cgv_ref[...].T)
    v_ref[:, V_GQA:V_GQA + 128] = gv0.astype(BF16)
    v_ref[:, V_GQA + 128:V_MLA] = gv1.astype(BF16)
    v_ref[:, V_MLA:V_COLS] = kv.astype(BF16)


def _cache_prep(caches, layer, w_ukv):
    cdk, cdv, cgk, cgv, cckv, ckr = caches
    n_batch, _, _, past = cdk.shape

    def spec(rows, cols):
        return pl.BlockSpec((None, None, rows, cols), lambda b: (b, layer, 0, 0))

    return pl.pallas_call(
        _cache_prep_kernel,
        grid=(n_batch,),
        in_specs=[spec(256, past), spec(256, past), spec(128, past), spec(128, past),
                  spec(past, MLA_KV_RANK), spec(MLA_ROPE, past),
                  _layer_spec((MLA_KV_RANK, MLA_HEADS * MLA_BLK), layer)],
        out_specs=[pl.BlockSpec((None, past, K_COLS), lambda b: (b, 0, 0)),
                   pl.BlockSpec((None, past, V_COLS), lambda b: (b, 0, 0))],
        out_shape=[jax.ShapeDtypeStruct((n_batch, past, K_COLS), BF16),
                   jax.ShapeDtypeStruct((n_batch, past, V_COLS), BF16)],
        name="cache_prep",
    )(cdk, cdv, cgk, cgv, cckv, ckr, w_ukv)


def _exp_scores(s):
    m = jnp.max(s, axis=-1, keepdims=True)
    e = jnp.exp2(s - m)
    return e.astype(BF16), 1.0 / jnp.sum(e, axis=-1, keepdims=True)


def _lane_mask(width, lo, hi):
    lane = lax.broadcasted_iota(jnp.int32, (1, width), 1)
    return jnp.where((lane >= lo) & (lane < hi), 1.0, 0.0).astype(BF16)


def _attn_kernel(*refs, lam_init, n_q, past):
    if past:
        q_ref, kc_ref, ko_ref, vc_ref, vo_ref, vec_ref, o_ref, k_ref, v_ref = refs

        @pl.when(pl.program_id(0) % n_q == 0)
        def _():
            k_ref[0:past, :] = kc_ref[...]
            k_ref[past:, :] = ko_ref[...]
            v_ref[0:past, :] = vc_ref[...]
            v_ref[past:, :] = vo_ref[...]
    else:
        q_ref, k_ref, v_ref, vec_ref, o_ref = refs
    tq = q_ref.shape[0]
    stack_rows = k_ref.shape[0] <= STACK_MAX_KEYS
    lq1 = vec_ref[:, VEC_LAM:VEC_LAM + 32]
    lk1 = vec_ref[:, VEC_LAM + 32:VEC_LAM + 64]
    lq2 = vec_ref[:, VEC_LAM + 64:VEC_LAM + 96]
    lk2 = vec_ref[:, VEC_LAM + 96:VEC_LAM + 128]
    lam = (jnp.exp(jnp.sum(lq1 * lk1, axis=-1, keepdims=True))
           - jnp.exp(jnp.sum(lq2 * lk2, axis=-1, keepdims=True)) + lam_init)

    def masked_heads(qv, k, v, masks):
        if stack_rows:
            e, r = _exp_scores(_dot_nt(jnp.concatenate([qv * m for m in masks], axis=0), k))
            o = _dot(e, v) * r
            return [o[j * tq:(j + 1) * tq] for j in range(len(masks))]
        outs = []
        for m in masks:
            e, r = _exp_scores(_dot_nt(qv * m, k))
            outs.append(_dot(e, v) * r)
        return outs

    parts = masked_heads(q_ref[:, Q_DIFF:Q_GQA], k_ref[:, K_DIFF:K_GQA], v_ref[:, V_DIFF:V_GQA],
                         [_lane_mask(256, j * DIFF_QK, (j + 1) * DIFF_QK) for j in range(2 * DIFF_HEADS)])
    lane = lax.broadcasted_iota(jnp.int32, (1, 256), 1)
    acc = jnp.zeros((tq, 256), F32)
    for hd in range(DIFF_HEADS):
        head = (lane >= hd * DIFF_V) & (lane < (hd + 1) * DIFF_V)
        acc = jnp.where(head, parts[2 * hd] - lam * parts[2 * hd + 1], acc)
    ms = _seg_sum_sq(acc, DIFF_V) * (1.0 / DIFF_V)
    subg = vec_ref[:, VEC_SUBG:VEC_SUBG + 256]
    o_ref[:, 0:256] = (acc * lax.rsqrt(ms + EPS) * subg * (1.0 - lam_init)).astype(o_ref.dtype)

    left = _lane_lt(128, GQA_HEAD_DIM)
    for blk in range(GQA_KV_HEADS):
        o_l, o_r = masked_heads(q_ref[:, Q_GQA + blk * 128:Q_GQA + (blk + 1) * 128],
                                k_ref[:, K_GQA + blk * 128:K_GQA + (blk + 1) * 128],
                                v_ref[:, V_GQA + blk * 128:V_GQA + (blk + 1) * 128],
                                [_lane_mask(128, 0, GQA_HEAD_DIM), _lane_mask(128, GQA_HEAD_DIM, 2 * GQA_HEAD_DIM)])
        o_ref[:, 256 + blk * 128:256 + (blk + 1) * 128] = jnp.where(left, o_l, o_r).astype(o_ref.dtype)

    upper = _lane_mask(MLA_BLK, MLA_NOPE, MLA_BLK)
    for pair in range(MLA_HEADS // 2):
        outs = []
        for hd in (2 * pair, 2 * pair + 1):
            mq = q_ref[:, Q_MLA + hd * MLA_BLK:Q_MLA + (hd + 1) * MLA_BLK]
            mk = k_ref[:, K_MLA + hd * MLA_BLK:K_MLA + (hd + 1) * MLA_BLK]
            mv = v_ref[:, V_MLA + hd * MLA_BLK:V_MLA + (hd + 1) * MLA_BLK]
            e, r = _exp_scores(_dot_nt(mq, mk))
            outs.append(_dot(e, mv * upper) * r)
        both = pltpu.roll(outs[0], MLA_V, axis=1) + outs[1]
        o_ref[:, 512 + pair * 128:512 + (pair + 1) * 128] = both.astype(o_ref.dtype)


def _attention(q, k_own, v_own, vecs, layer, lam_init, n_batch, tq_total, k_cache=None, v_cache=None):
    tq = TOKEN_TILE
    nq = tq_total // tq
    past = 0 if k_cache is None else k_cache.shape[1]
    tk = past + tq_total
    own = lambda cols: pl.BlockSpec((tq_total, cols), lambda i: (i // nq, 0))
    in_specs = [pl.BlockSpec((tq, Q_COLS), lambda i: (i, 0))]
    scratch = []
    if past:
        cached = lambda cols: pl.BlockSpec((None, past, cols), lambda i: (i // nq, 0, 0))
        in_specs += [cached(K_COLS), own(K_COLS), cached(V_COLS), own(V_COLS)]
        args = [q, k_cache, k_own, v_cache, v_own]
        scratch = [pltpu.VMEM((tk, K_COLS), BF16), pltpu.VMEM((tk, V_COLS), BF16)]
    else:
        in_specs += [own(K_COLS), own(V_COLS)]
        args = [q, k_own, v_own]
    in_specs.append(_layer_spec((1, VEC_END), layer))
    args.append(vecs)
    return pl.pallas_call(
        functools.partial(_attn_kernel, lam_init=lam_init, n_q=nq, past=past),
        grid=(n_batch * nq,),
        in_specs=in_specs,
        out_specs=pl.BlockSpec((tq, O_COLS), lambda i: (i, 0)),
        out_shape=jax.ShapeDtypeStruct((n_batch * tq_total, O_COLS), BF16),
        scratch_shapes=scratch,
        compiler_params=pltpu.CompilerParams(vmem_limit_bytes=VMEM_LIMIT),
        name="attention_tk%d" % tk,
    )(*args)


def _gelu_tanh(x):
    return 0.5 * x * (1.0 + jnp.tanh(math.sqrt(2.0 / math.pi) * (x + 0.044715 * (x * x * x))))


def _block_diag_lanes(blocks):
    pair = jnp.concatenate([blocks, blocks], axis=-1)
    tiled = jnp.concatenate([pair] * (SSM_LANES // 128), axis=-1)
    r = jnp.right_shift(lax.broadcasted_iota(jnp.int32, tiled.shape, 0), 4)
    c = jnp.right_shift(lax.broadcasted_iota(jnp.int32, tiled.shape, 1), 6)
    return jnp.where(r == c, tiled, 0.0)


def _ssm_kernel(*refs, n_batch, emit_state):
    uf_ref, ub_ref, h0_ref, par_ref, bre_ref, bim_ref, cre_ref, cim_ref, yf_ref, yb_ref = refs[:10]
    pos = 10
    if emit_state:
        hfin_ref = refs[pos]
        pos += 1
    bbar_scr, cblk_scr, hst_scr, bu_scr, hb_scr = refs[pos:]

    step = pl.program_id(0)
    t_chunk = uf_ref.shape[1]
    rows = n_batch * t_chunk
    abar = []
    for d in range(2):
        a_re = par_ref[d, 0:1, :]
        a_im = par_ref[d, 1:2, :]
        dt = jnp.exp(par_ref[d, 2:3, :])
        mag = jnp.exp(a_re * dt)
        abar.append((mag * jnp.cos(a_im * dt), mag * jnp.sin(a_im * dt)))

    @pl.when(step == 0)
    def _():
        for d in range(2):
            a_re = par_ref[d, 0:1, :]
            a_im = par_ref[d, 1:2, :]
            abr, abi = abar[d]
            den = a_re * a_re + a_im * a_im
            xr = abr - 1.0
            cr = (xr * a_re + abi * a_im) / den
            ci = (abi * a_re - xr * a_im) / den
            bre = _block_diag_lanes(bre_ref[d])
            bim = _block_diag_lanes(bim_ref[d])
            bbar_scr[d, :, 0:SSM_LANES] = (cr * bre - ci * bim).astype(BF16)
            bbar_scr[d, :, SSM_LANES:2 * SSM_LANES] = (cr * bim + ci * bre).astype(BF16)
            cblk_scr[d, :, 0:SSM_LANES] = _block_diag_lanes(cre_ref[d]).astype(BF16)
            cblk_scr[d, :, SSM_LANES:2 * SSM_LANES] = (-_block_diag_lanes(cim_ref[d])).astype(BF16)
        hst_scr[...] = h0_ref[...]

    for d, u_ref in enumerate((uf_ref, ub_ref)):
        u_tm = jnp.swapaxes(u_ref[...], 0, 1).reshape(rows, SSM_WIDTH)
        bu_scr[d] = _dot(u_tm.astype(BF16), bbar_scr[d])

    lw = 128
    if n_batch % 8 == 0:
        for j in range(SSM_LANES // lw):
            sre = slice(j * lw, (j + 1) * lw)
            sim = slice(SSM_LANES + j * lw, SSM_LANES + (j + 1) * lw)
            coef = [(jnp.broadcast_to(abar[d][0][:, sre], (n_batch, lw)),
                     jnp.broadcast_to(abar[d][1][:, sre], (n_batch, lw))) for d in range(2)]

            def body(i, carry, sre=sre, sim=sim, coef=coef):
                new = []
                for d in range(2):
                    hr, hi = carry[2 * d], carry[2 * d + 1]
                    ar, ai = coef[d]
                    t = i if d == 0 else t_chunk - 1 - i
                    r0 = pl.multiple_of(t * n_batch, n_batch)
                    nr = ar * hr - ai * hi + bu_scr[d, pl.ds(r0, n_batch), sre]
                    ni = ar * hi + ai * hr + bu_scr[d, pl.ds(r0, n_batch), sim]
                    hb_scr[d, pl.ds(r0, n_batch), sre] = nr.astype(BF16)
                    hb_scr[d, pl.ds(r0, n_batch), sim] = ni.astype(BF16)
                    new += [nr, ni]
                return tuple(new)

            init = (hst_scr[0, :, sre], hst_scr[0, :, sim], hst_scr[1, :, sre], hst_scr[1, :, sim])
            fin = lax.fori_loop(0, t_chunk, body, init, unroll=2)
            hst_scr[0, :, sre] = fin[0]
            hst_scr[0, :, sim] = fin[1]
            hst_scr[1, :, sre] = fin[2]
            hst_scr[1, :, sim] = fin[3]
    else:
        assert n_batch == 4
        n_blk = rows // 8
        low = lax.broadcasted_iota(jnp.int32, (8, lw), 0) < 4
        firsts = (low, jnp.logical_not(low))
        for j in range(SSM_LANES // lw):
            sre = slice(j * lw, (j + 1) * lw)
            sim = slice(SSM_LANES + j * lw, SSM_LANES + (j + 1) * lw)
            coef = []
            for d in range(2):
                ar = jnp.broadcast_to(abar[d][0][:, sre], (8, lw))
                ai = jnp.broadcast_to(abar[d][1][:, sre], (8, lw))
                coef.append((ar, ai, jnp.where(firsts[d], ar, ar * ar - ai * ai),
                             jnp.where(firsts[d], ai, 2.0 * ar * ai)))

            def body(i, carry, sre=sre, sim=sim, coef=coef):
                new = []
                for d in range(2):
                    pr, pi = carry[2 * d], carry[2 * d + 1]
                    ar, ai, cr, ci = coef[d]
                    first = firsts[d]
                    r16 = pl.multiple_of((2 * i if d == 0 else n_blk - 2 - 2 * i) * 8, 16)
                    tile_r, tile_i = [None, None], [None, None]
                    for half in ((0, 1) if d == 0 else (1, 0)):
                        vr = bu_scr[d, pl.ds(r16 + 8 * half, 8), sre]
                        vi = bu_scr[d, pl.ds(r16 + 8 * half, 8), sim]
                        sr = jnp.where(first, 0.0, pltpu.roll(vr, 4, axis=0))
                        si = jnp.where(first, 0.0, pltpu.roll(vi, 4, axis=0))
                        nr = cr * pr - ci * pi + (vr + (ar * sr - ai * si))
                        ni = cr * pi + ci * pr + (vi + (ar * si + ai * sr))
                        tile_r[half], tile_i[half] = nr, ni
                        pr = jnp.where(first, pltpu.roll(nr, 4, axis=0), nr)
                        pi = jnp.where(first, pltpu.roll(ni, 4, axis=0), ni)
                    hb_scr[d, pl.ds(r16, 16), sre] = jnp.concatenate(tile_r, axis=0).astype(BF16)
                    hb_scr[d, pl.ds(r16, 16), sim] = jnp.concatenate(tile_i, axis=0).astype(BF16)
                    new += [pr, pi]
                return tuple(new)

            init = (hst_scr[0, :, sre], hst_scr[0, :, sim], hst_scr[1, :, sre], hst_scr[1, :, sim])
            fin = lax.fori_loop(0, n_blk // 2, body, init, unroll=2)
            hst_scr[0, :, sre] = fin[0]
            hst_scr[0, :, sim] = fin[1]
            hst_scr[1, :, sre] = fin[2]
            hst_scr[1, :, sim] = fin[3]

    for d, y_ref in enumerate((yf_ref, yb_ref)):
        y = _dot_nt(hb_scr[d], cblk_scr[d])
        y_ref[...] = jnp.swapaxes(y.reshape(t_chunk, n_batch, SSM_WIDTH), 0, 1)

    if emit_state:
        @pl.when(step == pl.num_programs(0) - 1)
        def _():
            hfin_ref[...] = hst_scr[...]


def _ssm_scan(u_tok, h0, h0_spec, wts, layer, n_batch, seq, t_chunk, emit_state):
    n_steps = seq // t_chunk
    rows = n_batch * t_chunk
    h_rows = h0.shape[-2]
    fwd_blk = lambda k: (0, k, 0)
    bwd_blk = lambda k: (0, n_steps - 1 - k, 0)
    chunk = (n_batch, t_chunk, SSM_WIDTH)

    def par(shape):
        return _layer_spec((2,) + shape, layer)

    out_shape = [jax.ShapeDtypeStruct((n_batch, seq, SSM_WIDTH), F32)] * 2
    out_specs = [pl.BlockSpec(chunk, fwd_blk), pl.BlockSpec(chunk, bwd_blk)]
    if emit_state:
        out_shape.append(jax.ShapeDtypeStruct((2, h_rows, 2 * SSM_LANES), F32))
        out_specs.append(_const_spec((2, h_rows, 2 * SSM_LANES)))
    u3 = u_tok.reshape(n_batch, seq, SSM_WIDTH)
    return pl.pallas_call(
        functools.partial(_ssm_kernel, n_batch=n_batch, emit_state=emit_state),
        grid=(n_steps,),
        in_specs=[pl.BlockSpec(chunk, fwd_blk), pl.BlockSpec(chunk, bwd_blk), h0_spec, par((3, SSM_LANES)),
                  par((SSM_WIDTH, SSM_STATE)), par((SSM_WIDTH, SSM_STATE)),
                  par((SSM_WIDTH, SSM_STATE)), par((SSM_WIDTH, SSM_STATE))],
        out_specs=out_specs,
        out_shape=out_shape,
        scratch_shapes=[pltpu.VMEM((2, SSM_WIDTH, 2 * SSM_LANES), BF16),
                        pltpu.VMEM((2, SSM_WIDTH, 2 * SSM_LANES), BF16),
                        pltpu.VMEM((2, h_rows, 2 * SSM_LANES), F32),
                        pltpu.VMEM((2, rows, 2 * SSM_LANES), F32),
                        pltpu.VMEM((2, rows, 2 * SSM_LANES), BF16)],
        compiler_params=pltpu.CompilerParams(vmem_limit_bytes=VMEM_LIMIT),
        name="ssm_b%d" % n_batch,
    )(u3, u3, h0, wts["ssm_par"], wts["ssm_b_re"], wts["ssm_b_im"], wts["ssm_c_re"], wts["ssm_c_im"])


def _out_mlp_kernel(x_ref, oa_ref, yf_ref, yb_ref, u_ref, mod_ref, vec_ref, wout_ref, wglu_ref, w1_ref, w2_ref,
                    y_ref, *, final):
    g1 = mod_ref[:, 2 * D_MODEL:3 * D_MODEL]
    sh2 = mod_ref[:, 3 * D_MODEL:4 * D_MODEL]
    sc2 = mod_ref[:, 4 * D_MODEL:5 * D_MODEL]
    g2 = mod_ref[:, 5 * D_MODEL:6 * D_MODEL]
    ys = yf_ref[...] + yb_ref[...] + u_ref[...] * vec_ref[:, VEC_SSMD:VEC_SSMD + SSM_WIDTH]
    z = _dot(_gelu_tanh(ys).astype(BF16), wglu_ref[...])
    o_ssm = (z[:, 0:SSM_WIDTH] * _sigmoid(z[:, SSM_WIDTH:2 * SSM_WIDTH])).astype(BF16)
    mix = (_dot(oa_ref[:, 0:512], wout_ref[0:512, :]) + _dot(o_ssm, wout_ref[512:768, :])
           + _dot(oa_ref[:, 512:O_COLS], wout_ref[768:1024, :]))
    x1 = x_ref[...] + g1 * mix
    ms = jnp.mean(x1 * x1, axis=-1, keepdims=True)
    h = x1 * lax.rsqrt(ms + EPS) * vec_ref[:, VEC_N2G:VEC_N2G + D_MODEL]
    h = h * (1.0 + sc2) + sh2
    a = jnp.maximum(_dot(h.astype(BF16), w1_ref[...]), 0.0)
    x2 = x1 + g2 * _dot((a * a).astype(BF16), w2_ref[...])
    if final:
        ms2 = jnp.mean(x2 * x2, axis=-1, keepdims=True)
        x2 = x2 * lax.rsqrt(ms2 + EPS) * vec_ref[:, VEC_FNG:VEC_FNG + D_MODEL]
    y_ref[...] = x2


def _out_mlp(x2d, o_attn, y_fwd, y_bwd, u, mod3, layer, wts, n_batch, seq, latent, final):
    tm = TOKEN_TILE
    tpb = seq // tm
    n_tok = n_batch * seq
    if latent:
        mod_row = lambda i: (layer * MOD_ROWS + 1 + i // tpb, 0, 0)
    else:
        mod_row = lambda i: (layer * MOD_ROWS, 0, 0)
    tile = lambda i: (i, 0)
    return pl.pallas_call(
        functools.partial(_out_mlp_kernel, final=final),
        grid=(n_tok // tm,),
        in_specs=[
            pl.BlockSpec((tm, D_MODEL), tile),
            pl.BlockSpec((tm, O_COLS), tile),
            pl.BlockSpec((tm, SSM_WIDTH), tile),
            pl.BlockSpec((tm, SSM_WIDTH), tile),
            pl.BlockSpec((tm, SSM_WIDTH), tile),
            pl.BlockSpec((None, 1, N_MOD * D_MODEL), mod_row),
            _layer_spec((1, VEC_END), layer),
            _layer_spec((D_MODEL, D_MODEL), layer, single_buffer=True),
            _layer_spec((SSM_WIDTH, 2 * SSM_WIDTH), layer, single_buffer=True),
            _layer_spec((D_MODEL, D_FF), layer, single_buffer=True),
            _layer_spec((D_FF, D_MODEL), layer, single_buffer=True),
        ],
        out_specs=pl.BlockSpec((tm, D_MODEL), tile),
        out_shape=jax.ShapeDtypeStruct((n_tok, D_MODEL), F32),
        compiler_params=pltpu.CompilerParams(vmem_limit_bytes=VMEM_LIMIT),
        name="out_mlp_%s%s" % ("lat" if latent else "ctx", "_final" if final else ""),
    )(x2d, o_attn, y_fwd, y_bwd, u, mod3, wts["vecs"], wts["w_out"], wts["w_glu"], wts["w1"], wts["w2"])


def _rope_tables(t_len, chunk, n_chunks):
    n = chunk // 4
    rows = t_len // GRID_W
    row = np.repeat(np.arange(rows), GRID_W).astype(np.float32)
    col = np.tile(np.arange(GRID_W), rows).astype(np.float32)
    freq = (np.float32(ROPE_THETA) ** (-np.arange(n, dtype=np.float32) / np.float32(n))).astype(np.float32)
    ang_r = (row[:, None] * freq).astype(np.float32)
    ang_c = (col[:, None] * freq).astype(np.float32)
    cr, sr, cc, sc = np.cos(ang_r), np.sin(ang_r), np.cos(ang_c), np.sin(ang_c)
    z = np.zeros_like(cr)
    tabs = np.stack([np.concatenate([cr, cr, cc, cc], axis=-1),
                     np.concatenate([-sr, z, -sc, z], axis=-1),
                     np.concatenate([z, sr, z, sc], axis=-1)]).astype(np.float32)
    return np.tile(tabs, (1, 1, n_chunks))


def _mla_rope_tables(t_len):
    tabs = _rope_tables(t_len, MLA_ROPE, 1)
    ident = np.stack([np.ones((t_len, MLA_NOPE), np.float32), np.zeros((t_len, MLA_NOPE), np.float32),
                      np.zeros((t_len, MLA_NOPE), np.float32)])
    tail = ident[:, :, :MLA_BLK - MLA_QK]
    return np.concatenate([ident, tabs, tail], axis=-1)


def _pack_vectors(p, n_layers):
    def rows(a):
        return a.reshape(n_layers, 1, -1)

    fng = jnp.broadcast_to(p["final_norm_g"].reshape(1, 1, D_MODEL), (n_layers, 1, D_MODEL))
    return jnp.concatenate([
        rows(p["norm1_g"]), rows(p["norm2_g"]), fng,
        jnp.tile(rows(p["gqa_qn_g"]), (1, 1, GQA_HEADS)),
        jnp.tile(rows(p["gqa_kn_g"]), (1, 1, GQA_KV_HEADS)),
        rows(p["mla_qn_g"]), jnp.zeros((n_layers, 1, CQ_PAD - MLA_Q_RANK), F32),
        rows(p["mla_kvn_g"]),
        jnp.tile(rows(p["diff_subln_g"]), (1, 1, DIFF_HEADS)),
        rows(p["ssm_d"]),
        rows(p["diff_lq1"]), rows(p["diff_lk1"]), rows(p["diff_lq2"]), rows(p["diff_lk2"]),
    ], axis=-1)


def kernel(x_prompt, x_sample, cache_diff_k, cache_diff_v, cache_gqa_k, cache_gqa_v, cache_mla_ckv, cache_mla_krope, state_ssm_re, state_ssm_im, c, c_ctx, norm1_g, norm2_g, w_ada, b_ada, w_in, w_out, diff_lq1, diff_lk1, diff_lq2, diff_lk2, diff_subln_g, gqa_qn_g, gqa_kn_g, ssm_a_re, ssm_a_im, ssm_log_dt, ssm_b_re, ssm_b_im, ssm_c_re, ssm_c_im, ssm_d, ssm_w_glu, mla_qn_g, mla_kvn_g, mla_w_uq, mla_w_ukv, mlp_w1, mlp_w2, final_norm_g):
    n_layers = w_in.shape[0]
    n_ctx, seq, _ = x_prompt.shape
    n_lat, t_lat, _ = x_sample.shape
    past = cache_diff_k.shape[2]
    assert n_lat + 1 <= MOD_ROWS and n_lat == 4 and n_ctx % 8 == 0
    assert seq == TOKEN_TILE and t_lat % TOKEN_TILE == 0

    p = dict(norm1_g=norm1_g, norm2_g=norm2_g, final_norm_g=final_norm_g, gqa_qn_g=gqa_qn_g, gqa_kn_g=gqa_kn_g,
             mla_qn_g=mla_qn_g, mla_kvn_g=mla_kvn_g, diff_subln_g=diff_subln_g, ssm_d=ssm_d, diff_lq1=diff_lq1,
             diff_lk1=diff_lk1, diff_lq2=diff_lq2, diff_lk2=diff_lk2)
    ssm_rows = SSM_GROUPS * SSM_GROUP
    wts = {
        "vecs": _pack_vectors(p, n_layers),
        "w_in_t": jnp.swapaxes(w_in, 1, 2),
        "w_uq": mla_w_uq, "w_ukv": mla_w_ukv,
        "w_out": w_out.astype(BF16), "w1": mlp_w1.astype(BF16), "w2": mlp_w2.astype(BF16),
        "w_glu": ssm_w_glu.astype(BF16),
        "ssm_par": jnp.stack([ssm_a_re.reshape(n_layers, 2, SSM_LANES), ssm_a_im.reshape(n_layers, 2, SSM_LANES),
                              jnp.repeat(ssm_log_dt, SSM_STATE, axis=-1)], axis=2),
        "ssm_b_re": jnp.swapaxes(ssm_b_re, 3, 4).reshape(n_layers, 2, ssm_rows, SSM_STATE),
        "ssm_b_im": jnp.swapaxes(ssm_b_im, 3, 4).reshape(n_layers, 2, ssm_rows, SSM_STATE),
        "ssm_c_re": ssm_c_re.reshape(n_layers, 2, ssm_rows, SSM_STATE),
        "ssm_c_im": ssm_c_im.reshape(n_layers, 2, ssm_rows, SSM_STATE),
    }

    cond = jnp.concatenate([c_ctx[None], c, jnp.zeros((MOD_ROWS - 1 - n_lat, D_MODEL), F32)], axis=0)
    mod3 = _adaln(cond, w_ada, b_ada).reshape(n_layers * MOD_ROWS, 1, N_MOD * D_MODEL)

    rope_tabs = (jnp.asarray(_rope_tables(t_lat, DIFF_QK, 256 // DIFF_QK)),
                 jnp.asarray(_rope_tables(t_lat, GQA_HEAD_DIM, 256 // GQA_HEAD_DIM)),
                 jnp.asarray(_mla_rope_tables(t_lat)))
    caches = (jnp.transpose(cache_diff_k, (0, 1, 3, 4, 2)).reshape(n_lat, n_layers, 256, past),
              jnp.transpose(cache_diff_v, (0, 1, 3, 4, 2)).reshape(n_lat, n_layers, 256, past),
              jnp.transpose(cache_gqa_k, (0, 1, 3, 4, 2)).reshape(n_lat, n_layers, 128, past),
              jnp.transpose(cache_gqa_v, (0, 1, 3, 4, 2)).reshape(n_lat, n_layers, 128, past),
              cache_mla_ckv,
              jnp.swapaxes(cache_mla_krope, 2, 3))

    lam_inits = [0.8 - 0.6 * math.exp(-0.3 * l) for l in range(n_layers)]

    x = x_prompt.reshape(n_ctx * seq, D_MODEL)
    zero_state = jnp.zeros((2, n_ctx, 2 * SSM_LANES), F32)
    zero_spec = _const_spec((2, n_ctx, 2 * SSM_LANES))
    kept = None
    states = []
    for l in range(n_layers):
        res = _in_proj(x, mod3, l, wts, n_ctx, seq, latent=False, prev_caches=kept)
        q, k, v, u = res[:4]
        kept = res[4:]
        o_attn = _attention(q, k, v, wts["vecs"], l, lam_inits[l], n_ctx, seq)
        y_fwd, y_bwd, h_fin = _ssm_scan(u, zero_state, zero_spec, wts, l, n_ctx, seq, CTX_SSM_CHUNK,
                                        emit_state=True)
        x = _out_mlp(x, o_attn, y_fwd.reshape(n_ctx * seq, SSM_WIDTH), y_bwd.reshape(n_ctx * seq, SSM_WIDTH), u,
                     mod3, l, wts, n_ctx, seq, latent=False, final=(l == n_layers - 1))
        states.append(h_fin)
    y_prompt = x.reshape(n_ctx, seq, D_MODEL)

    x = x_sample.reshape(n_lat * t_lat, D_MODEL)
    h0 = jnp.concatenate([state_ssm_re.reshape(n_lat, n_layers, 2, SSM_LANES),
                          state_ssm_im.reshape(n_lat, n_layers, 2, SSM_LANES)], axis=-1)
    h0 = jnp.transpose(h0, (1, 2, 0, 3))
    h0 = jnp.concatenate([h0, h0], axis=2)
    for l in range(n_layers):
        h0_spec = _layer_spec((2, 2 * n_lat, 2 * SSM_LANES), l)
        q, k, v, u = _in_proj(x, mod3, l, wts, n_lat, t_lat, latent=True, rope_tabs=rope_tabs)
        kc, vc = _cache_prep(caches, l, wts["w_ukv"])
        o_attn = _attention(q, k, v, wts["vecs"], l, lam_inits[l], n_lat, t_lat, k_cache=kc, v_cache=vc)
        y_fwd, y_bwd = _ssm_scan(u, h0, h0_spec, wts, l, n_lat, t_lat, LAT_SSM_CHUNK, emit_state=False)
        x = _out_mlp(x, o_attn, y_fwd.reshape(n_lat * t_lat, SSM_WIDTH), y_bwd.reshape(n_lat * t_lat, SSM_WIDTH), u,
                     mod3, l, wts, n_lat, t_lat, latent=True, final=(l == n_layers - 1))
    y_sample = x.reshape(n_lat, t_lat, D_MODEL)

    cdk, cdv, cgk, cgv, cckv, ckr = kept

    def token_major(a, heads, dim):
        return jnp.transpose(a.reshape(n_ctx, n_layers, heads, dim, seq), (0, 1, 4, 2, 3))

    st = jnp.transpose(jnp.stack(states), (2, 0, 1, 3))
    return (y_prompt, y_sample,
            token_major(cdk, DIFF_HEADS, 2 * DIFF_QK), token_major(cdv, DIFF_HEADS, DIFF_V),
            token_major(cgk, GQA_KV_HEADS, GQA_HEAD_DIM), token_major(cgv, GQA_KV_HEADS, GQA_HEAD_DIM),
            cckv, jnp.swapaxes(ckr, 2, 3),
            st[..., :SSM_LANES].reshape(n_ctx, n_layers, 2, SSM_GROUPS, SSM_STATE),
            st[..., SSM_LANES:].reshape(n_ctx, n_layers, 2, SSM_GROUPS, SSM_STATE))
```

```python
import functools
import math

import numpy as np
import jax
import jax.numpy as jnp
from jax import lax
from jax.experimental import pallas as pl
from jax.experimental.pallas import tpu as pltpu

F32 = jnp.float32
BF16 = jnp.bfloat16

D_MODEL = 1024
GRID_W = 64
ROPE_THETA = 10000.0
EPS = 1e-6
DIFF_HEADS, DIFF_QK, DIFF_V = 4, 32, 64
GQA_HEADS, GQA_KV_HEADS, GQA_HEAD_DIM = 4, 2, 64
SSM_WIDTH, SSM_GROUP, SSM_STATE = 256, 16, 64
SSM_GROUPS = SSM_WIDTH // SSM_GROUP
SSM_LANES = SSM_GROUPS * SSM_STATE
MLA_HEADS, MLA_Q_RANK, MLA_KV_RANK, MLA_NOPE, MLA_ROPE, MLA_V = 4, 192, 128, 64, 32, 64
MLA_QK = MLA_NOPE + MLA_ROPE
D_FF = 4 * D_MODEL
N_MOD = 6
LOG2E = math.log2(math.e)
IN_COLS = 1888
MOD_ROWS = 8

P_DQ, P_DK, P_DV, P_GQ, P_GK, P_GV, P_U, P_CQ, P_CKV, P_KR, P_END = (
    0, 256, 512, 768, 1024, 1152, 1280, 1536, 1792, 1920, 2048)
S_CQ_END, S_CKV, S_KR = 1728, 1728, 1856
CQ_PAD = P_CKV - P_CQ
MLA_BLK = 128
Q_DIFF, Q_GQA, Q_MLA, Q_COLS = 0, 256, 512, 1024
K_DIFF, K_GQA, K_MLA, K_COLS = 0, 256, 512, 1024
V_DIFF, V_GQA, V_MLA, V_COLS = 0, 256, 512, 1024
O_COLS = 768

VEC_N1G, VEC_N2G, VEC_FNG, VEC_QNG, VEC_KNG, VEC_MQNG, VEC_KVNG, VEC_SUBG, VEC_SSMD, VEC_LAM, VEC_END = (
    0, 1024, 2048, 3072, 3328, 3456, 3712, 3840, 4096, 4352, 4480)

TOKEN_TILE = 256
PROJ_TILE = 512
MLP_TILE = 512
FF_CHUNK = 1024
CTX_SSM_CHUNK = 32
LAT_SSM_CHUNK = 256
STACK_MAX_KEYS = 512
VMEM_LIMIT = 48 * 1024 * 1024


def _dot(a, b):
    return jnp.dot(a, b, preferred_element_type=F32)


def _dot_nt(a, b):
    return lax.dot_general(a, b, (((1,), (1,)), ((), ())), preferred_element_type=F32)


def _block_ones(width, seg):
    shift = seg.bit_length() - 1
    r = jnp.right_shift(lax.broadcasted_iota(jnp.int32, (width, width), 0), shift)
    c = jnp.right_shift(lax.broadcasted_iota(jnp.int32, (width, width), 1), shift)
    return jnp.where(r == c, 1.0, 0.0).astype(BF16)


def _seg_sum_sq(y, seg):
    ones = _block_ones(y.shape[-1], seg)
    sq = y * y
    hi = sq.astype(BF16)
    lo = (sq - hi.astype(F32)).astype(BF16)
    return _dot(hi, ones) + _dot(lo, ones)


def _rope(x, tab_ref, quarter):
    outs = []
    for j in range(x.shape[-1] // 128):
        sl = slice(j * 128, (j + 1) * 128)
        xs = x[:, sl]
        up = pltpu.roll(xs, 128 - quarter, axis=1)
        dn = pltpu.roll(xs, quarter, axis=1)
        outs.append(xs * tab_ref[0, :, sl] + up * tab_ref[1, :, sl] + dn * tab_ref[2, :, sl])
    return outs[0] if len(outs) == 1 else jnp.concatenate(outs, axis=-1)


def _sigmoid(x):
    return 1.0 / (1.0 + jnp.exp(-x))


def _lane_lt(width, bound):
    return lax.broadcasted_iota(jnp.int32, (1, width), 1) < bound


def _dup_halves(x):
    swapped = pltpu.roll(x, 64, axis=1)
    left = _lane_lt(128, 64)
    return jnp.where(left, x, swapped), jnp.where(left, swapped, x)


def _const_spec(shape):
    nd = len(shape)
    return pl.BlockSpec(shape, lambda *_: (0,) * nd)


def _layer_spec(shape, layer, single_buffer=False):
    nd = len(shape)
    kw = {"pipeline_mode": pl.Buffered(1)} if single_buffer else {}
    return pl.BlockSpec((None,) + tuple(shape), lambda *_: (layer,) + (0,) * nd, **kw)


def _adaln_kernel(cond_ref, w_ref, b_ref, o_ref):
    c = cond_ref[...]
    s = c * _sigmoid(c)
    o_ref[...] = _dot(s.astype(BF16), w_ref[...].astype(BF16)) + b_ref[...]


def _adaln(cond, w_ada, b_ada):
    n_layers = w_ada.shape[0]
    tn = 1024
    return pl.pallas_call(
        _adaln_kernel,
        grid=(n_layers, N_MOD * D_MODEL // tn),
        in_specs=[
            pl.BlockSpec((MOD_ROWS, D_MODEL), lambda l, j: (0, 0)),
            pl.BlockSpec((None, D_MODEL, tn), lambda l, j: (l, 0, j)),
            pl.BlockSpec((None, 1, tn), lambda l, j: (l, 0, j)),
        ],
        out_specs=pl.BlockSpec((None, MOD_ROWS, tn), lambda l, j: (l, 0, j)),
        out_shape=jax.ShapeDtypeStruct((n_layers, MOD_ROWS, N_MOD * D_MODEL), F32),
        compiler_params=pltpu.CompilerParams(vmem_limit_bytes=VMEM_LIMIT),
        name="adaln",
    )(cond, w_ada, b_ada.reshape(n_layers, 1, N_MOD * D_MODEL))


N_CACHE = 6


def _in_proj_kernel(*refs, latent, n_alias, layer):
    x_ref, mod_ref, vec_ref, wint_ref, wuq_ref, wukv_ref = refs[:6]
    pos = 6
    if latent:
        rope_d_ref, rope_g_ref, rope_m_ref = refs[pos:pos + 3]
        pos += 3
    pos += n_alias
    q_ref, k_ref, v_ref, u_ref = refs[pos:pos + 4]
    pos += 4
    if not latent:
        cdk_ref, cdv_ref, cgk_ref, cgv_ref, cckv_ref, ckr_ref = refs[pos:pos + N_CACHE]
        pos += N_CACHE
    win_scr, wuq_scr, wukv_scr = refs[pos:]

    @pl.when(pl.program_id(0) == 0)
    def _():
        aligned = (S_CQ_END // 128) * 128
        for r0 in range(0, aligned, 256):
            r1 = min(r0 + 256, aligned)
            win_scr[:, r0:r1] = wint_ref[r0:r1, :].T.astype(BF16)
        tail = wint_ref[aligned:aligned + 128, :].T
        win_scr[:, aligned:aligned + 128] = jnp.where(_lane_lt(128, S_CQ_END - aligned), tail, 0.0).astype(BF16)
        win_scr[:, P_CKV:P_KR] = wint_ref[S_CKV:S_KR, :].T.astype(BF16)
        kr_rows = jnp.concatenate([jnp.zeros((MLA_NOPE, D_MODEL), F32), wint_ref[S_KR:IN_COLS, :],
                                   jnp.zeros((MLA_BLK - MLA_QK, D_MODEL), F32)], axis=0)
        win_scr[:, P_KR:P_END] = kr_rows.T.astype(BF16)
        wuq_scr[...] = jnp.zeros(wuq_scr.shape, BF16)
        wuq = wuq_ref[...]
        for hd in range(MLA_HEADS):
            wuq_scr[0:MLA_Q_RANK, hd * MLA_BLK:hd * MLA_BLK + MLA_QK] = (
                wuq[:, hd * MLA_QK:(hd + 1) * MLA_QK].astype(BF16))
        wukv_scr[...] = wukv_ref[...].astype(BF16)

    x = x_ref[...]
    sh1 = mod_ref[:, 0:D_MODEL]
    sc1 = mod_ref[:, D_MODEL:2 * D_MODEL]
    ms = jnp.mean(x * x, axis=-1, keepdims=True)
    h = x * lax.rsqrt(ms + EPS) * vec_ref[:, VEC_N1G:VEC_N1G + D_MODEL]
    h = h * (1.0 + sc1) + sh1
    proj = _dot(h.astype(BF16), win_scr[...])

    dq = proj[:, P_DQ:P_DK]
    dk = proj[:, P_DK:P_DV]
    dv = proj[:, P_DV:P_GQ]
    gq = proj[:, P_GQ:P_GK]
    gk = proj[:, P_GK:P_GV]
    gv = proj[:, P_GV:P_U]
    u_ref[...] = proj[:, P_U:P_CQ]
    cq = proj[:, P_CQ:P_CKV]
    ckv = proj[:, P_CKV:P_KR]
    krb = proj[:, P_KR:P_END]

    qng = vec_ref[:, VEC_QNG:VEC_QNG + 256]
    kng = vec_ref[:, VEC_KNG:VEC_KNG + 128]
    gq = gq * lax.rsqrt(_seg_sum_sq(gq, GQA_HEAD_DIM) * (1.0 / GQA_HEAD_DIM) + EPS) * qng
    gk = gk * lax.rsqrt(_seg_sum_sq(gk, GQA_HEAD_DIM) * (1.0 / GQA_HEAD_DIM) + EPS) * kng
    ckv_n = (ckv * lax.rsqrt(jnp.mean(ckv * ckv, axis=-1, keepdims=True) + EPS)
             * vec_ref[:, VEC_KVNG:VEC_KVNG + MLA_KV_RANK])
    cq_ms = jnp.sum(cq * cq, axis=-1, keepdims=True) * (1.0 / MLA_Q_RANK)
    cq_n = cq * lax.rsqrt(cq_ms + EPS) * vec_ref[:, VEC_MQNG:VEC_MQNG + CQ_PAD]
    mq = _dot(cq_n.astype(BF16), wuq_scr[...])
    kv = _dot(ckv_n.astype(BF16), wukv_scr[...])

    if not latent:
        seq = cckv_ref.shape[2]

        def keep(ref, val, feature_major=True, rows=None):
            for s in range(ref.shape[0]):
                piece = val[s * seq:(s + 1) * seq]
                if feature_major:
                    piece = piece.T
                if rows is not None:
                    piece = piece[rows]
                for l in range(ref.shape[1]):
                    ref[s, l] = piece if ref.shape[1] == 1 or l == layer else jnp.zeros_like(piece)

        keep(cdk_ref, dk)
        keep(cdv_ref, dv)
        keep(cgk_ref, gk)
        keep(cgv_ref, gv)
        keep(cckv_ref, ckv_n, feature_major=False)
        keep(ckr_ref, krb, rows=slice(MLA_NOPE, MLA_QK))
    else:
        dq = _rope(dq, rope_d_ref, DIFF_QK // 4)
        dk = _rope(dk, rope_d_ref, DIFF_QK // 4)
        gq = _rope(gq, rope_g_ref, GQA_HEAD_DIM // 4)
        gk = _rope(gk, rope_g_ref, GQA_HEAD_DIM // 4)
        mq = jnp.concatenate(
            [_rope(mq[:, hd * MLA_BLK:(hd + 1) * MLA_BLK], rope_m_ref, MLA_ROPE // 4)
             for hd in range(MLA_HEADS)], axis=-1)
        krb = _rope(krb, rope_m_ref, MLA_ROPE // 4)

    q_ref[:, Q_DIFF:Q_GQA] = (dq * (LOG2E * DIFF_QK ** -0.5)).astype(BF16)
    q_ref[:, Q_GQA:Q_MLA] = (gq * (LOG2E * GQA_HEAD_DIM ** -0.5)).astype(BF16)
    q_ref[:, Q_MLA:Q_COLS] = (mq * (LOG2E * MLA_QK ** -0.5)).astype(BF16)
    k_ref[:, K_DIFF:K_GQA] = dk.astype(BF16)
    gk0, gk1 = _dup_halves(gk)
    k_ref[:, K_GQA:K_GQA + 128] = gk0.astype(BF16)
    k_ref[:, K_GQA + 128:K_MLA] = gk1.astype(BF16)
    nope = _lane_lt(MLA_BLK, MLA_NOPE)
    for hd in range(MLA_HEADS):
        blk = slice(hd * MLA_BLK, (hd + 1) * MLA_BLK)
        k_ref[:, K_MLA + hd * MLA_BLK:K_MLA + (hd + 1) * MLA_BLK] = jnp.where(nope, kv[:, blk], krb).astype(BF16)
    v_ref[:, V_DIFF:V_GQA] = dv.astype(BF16)
    gv0, gv1 = _dup_halves(gv)
    v_ref[:, V_GQA:V_GQA + 128] = gv0.astype(BF16)
    v_ref[:, V_GQA + 128:V_MLA] = gv1.astype(BF16)
    v_ref[:, V_MLA:V_COLS] = kv.astype(BF16)


def _in_proj(x2d, mod3, layer, wts, n_batch, seq, latent, rope_tabs=None, prev_caches=None):
    tm = PROJ_TILE
    n_tok = n_batch * seq
    n_layers = wts["vecs"].shape[0]
    if latent:
        tpb = seq // tm
        mod_row = lambda i: (layer * MOD_ROWS + 1 + i // tpb, 0, 0)
    else:
        mod_row = lambda i: (layer * MOD_ROWS, 0, 0)
    tile = lambda i: (i, 0)
    in_specs = [
        pl.BlockSpec((tm, D_MODEL), tile),
        pl.BlockSpec((None, 1, N_MOD * D_MODEL), mod_row),
        _layer_spec((1, VEC_END), layer),
        _layer_spec((IN_COLS, D_MODEL), layer, single_buffer=True),
        _layer_spec((MLA_Q_RANK, MLA_HEADS * MLA_QK), layer, single_buffer=True),
        _layer_spec((MLA_KV_RANK, MLA_HEADS * MLA_BLK), layer, single_buffer=True),
    ]
    args = [x2d, mod3, wts["vecs"], wts["w_in_t"], wts["w_uq"], wts["w_ukv"]]
    out_shape = [
        jax.ShapeDtypeStruct((n_tok, Q_COLS), BF16),
        jax.ShapeDtypeStruct((n_tok, K_COLS), BF16),
        jax.ShapeDtypeStruct((n_tok, V_COLS), BF16),
        jax.ShapeDtypeStruct((n_tok, SSM_WIDTH), F32),
    ]
    out_specs = [
        pl.BlockSpec((tm, Q_COLS), tile),
        pl.BlockSpec((tm, K_COLS), tile),
        pl.BlockSpec((tm, V_COLS), tile),
        pl.BlockSpec((tm, SSM_WIDTH), tile),
    ]
    aliases = {}
    n_alias = 0
    if latent:
        pos = lambda i: (0, i % tpb, 0)
        in_specs += [pl.BlockSpec((3, tm, 256), pos), pl.BlockSpec((3, tm, 256), pos),
                     pl.BlockSpec((3, tm, 128), pos)]
        args += list(rope_tabs)
    else:
        spt = tm // seq
        cache_dims = [(256, seq), (256, seq), (128, seq), (128, seq), (seq, MLA_KV_RANK), (MLA_ROPE, seq)]
        for dims in cache_dims:
            out_shape.append(jax.ShapeDtypeStruct((n_batch, n_layers) + dims, F32))
            if prev_caches is None:
                out_specs.append(pl.BlockSpec((spt, n_layers) + dims, lambda i: (i, 0, 0, 0)))
            else:
                out_specs.append(pl.BlockSpec((spt, 1) + dims, lambda i: (i, layer, 0, 0)))
        if prev_caches is not None:
            n_alias = N_CACHE
            for j, arr in enumerate(prev_caches):
                aliases[len(args)] = 4 + j
                in_specs.append(pl.BlockSpec(memory_space=pl.ANY))
                args.append(arr)
    return pl.pallas_call(
        functools.partial(_in_proj_kernel, latent=latent, n_alias=n_alias, layer=layer),
        grid=(n_tok // tm,),
        in_specs=in_specs,
        out_specs=out_specs,
        out_shape=out_shape,
        input_output_aliases=aliases,
        scratch_shapes=[pltpu.VMEM((D_MODEL, P_END), BF16),
                        pltpu.VMEM((CQ_PAD, MLA_HEADS * MLA_BLK), BF16),
                        pltpu.VMEM((MLA_KV_RANK, MLA_HEADS * MLA_BLK), BF16)],
        compiler_params=pltpu.CompilerParams(vmem_limit_bytes=VMEM_LIMIT),
        name="in_proj_lat" if latent else "in_proj_ctx",
    )(*args)


def _cache_prep_kernel(cdk_ref, cdv_ref, cgk_ref, cgv_ref, cckv_ref, ckr_ref, wukv_ref, k_ref, v_ref):
    past = k_ref.shape[0]
    kv = _dot(cckv_ref[...].astype(BF16), wukv_ref[...].astype(BF16))
    krb = jnp.concatenate([jnp.zeros((MLA_NOPE, past), F32), ckr_ref[...],
                           jnp.zeros((MLA_BLK - MLA_QK, past), F32)], axis=0).T
    k_ref[:, K_DIFF:K_GQA] = cdk_ref[...].T.astype(BF16)
    gk0, gk1 = _dup_halves(cgk_ref[...].T)
    k_ref[:, K_GQA:K_GQA + 128] = gk0.astype(BF16)
    k_ref[:, K_GQA + 128:K_MLA] = gk1.astype(BF16)
    nope = _lane_lt(MLA_BLK, MLA_NOPE)
    for hd in range(MLA_HEADS):
        blk = slice(hd * MLA_BLK, (hd + 1) * MLA_BLK)
        k_ref[:, K_MLA + hd * MLA_BLK:K_MLA + (hd + 1) * MLA_BLK] = jnp.where(nope, kv[:, blk], krb).astype(BF16)
    v_ref[:, V_DIFF:V_GQA] = cdv_ref[...].T.astype(BF16)
    gv0, gv1 = _dup_halves(cgv_ref[...].T)
    v_ref[:, V_GQA:V_GQA + 128] = gv0.astype(BF16)
    v_ref[:, V_GQA + 128:V_MLA] = gv1.astype(BF16)
    v_ref[:, V_MLA:V_COLS] = kv.astype(BF16)


def _cache_prep(caches, layer, w_ukv):
    cdk, cdv, cgk, cgv, cckv, ckr = caches
    n_batch, _, _, past = cdk.shape

    def spec(rows, cols):
        return pl.BlockSpec((None, None, rows, cols), lambda b: (b, layer, 0, 0))

    return pl.pallas_call(
        _cache_prep_kernel,
        grid=(n_batch,),
        in_specs=[spec(256, past), spec(256, past), spec(128, past), spec(128, past),
                  spec(past, MLA_KV_RANK), spec(MLA_ROPE, past),
                  _layer_spec((MLA_KV_RANK, MLA_HEADS * MLA_BLK), layer)],
        out_specs=[pl.BlockSpec((None, past, K_COLS), lambda b: (b, 0, 0)),
                   pl.BlockSpec((None, past, V_COLS), lambda b: (b, 0, 0))],
        out_shape=[jax.ShapeDtypeStruct((n_batch, past, K_COLS), BF16),
                   jax.ShapeDtypeStruct((n_batch, past, V_COLS), BF16)],
        name="cache_prep",
    )(cdk, cdv, cgk, cgv, cckv, ckr, w_ukv)


def _exp_scores(s):
    m = jnp.max(s, axis=-1, keepdims=True)
    e = jnp.exp2(s - m)
    return e.astype(BF16), 1.0 / jnp.sum(e, axis=-1, keepdims=True)


def _lane_mask(width, lo, hi):
    lane = lax.broadcasted_iota(jnp.int32, (1, width), 1)
    return jnp.where((lane >= lo) & (lane < hi), 1.0, 0.0).astype(BF16)


def _attn_kernel(*refs, lam_init, n_q, past):
    if past:
        q_ref, kc_ref, ko_ref, vc_ref, vo_ref, vec_ref, o_ref, k_ref, v_ref = refs

        @pl.when(pl.program_id(0) % n_q == 0)
        def _():
            k_ref[0:past, :] = kc_ref[...]
            k_ref[past:, :] = ko_ref[...]
            v_ref[0:past, :] = vc_ref[...]
            v_ref[past:, :] = vo_ref[...]
    else:
        q_ref, k_ref, v_ref, vec_ref, o_ref = refs
    tq = q_ref.shape[0]
    stack_rows = k_ref.shape[0] <= STACK_MAX_KEYS
    lq1 = vec_ref[:, VEC_LAM:VEC_LAM + 32]
    lk1 = vec_ref[:, VEC_LAM + 32:VEC_LAM + 64]
    lq2 = vec_ref[:, VEC_LAM + 64:VEC_LAM + 96]
    lk2 = vec_ref[:, VEC_LAM + 96:VEC_LAM + 128]
    lam = (jnp.exp(jnp.sum(lq1 * lk1, axis=-1, keepdims=True))
           - jnp.exp(jnp.sum(lq2 * lk2, axis=-1, keepdims=True)) + lam_init)

    def masked_heads(qv, k, v, masks):
        if stack_rows:
            e, r = _exp_scores(_dot_nt(jnp.concatenate([qv * m for m in masks], axis=0), k))
            o = _dot(e, v) * r
            return [o[j * tq:(j + 1) * tq] for j in range(len(masks))]
        outs = []
        for m in masks:
            e, r = _exp_scores(_dot_nt(qv * m, k))
            outs.append(_dot(e, v) * r)
        return outs

    parts = masked_heads(q_ref[:, Q_DIFF:Q_GQA], k_ref[:, K_DIFF:K_GQA], v_ref[:, V_DIFF:V_GQA],
                         [_lane_mask(256, j * DIFF_QK, (j + 1) * DIFF_QK) for j in range(2 * DIFF_HEADS)])
    lane = lax.broadcasted_iota(jnp.int32, (1, 256), 1)
    acc = jnp.zeros((tq, 256), F32)
    for hd in range(DIFF_HEADS):
        head = (lane >= hd * DIFF_V) & (lane < (hd + 1) * DIFF_V)
        acc = jnp.where(head, parts[2 * hd] - lam * parts[2 * hd + 1], acc)
    ms = _seg_sum_sq(acc, DIFF_V) * (1.0 / DIFF_V)
    subg = vec_ref[:, VEC_SUBG:VEC_SUBG + 256]
    o_ref[:, 0:256] = (acc * lax.rsqrt(ms + EPS) * subg * (1.0 - lam_init)).astype(o_ref.dtype)

    left = _lane_lt(128, GQA_HEAD_DIM)
    for blk in range(GQA_KV_HEADS):
        o_l, o_r = masked_heads(q_ref[:, Q_GQA + blk * 128:Q_GQA + (blk + 1) * 128],
                                k_ref[:, K_GQA + blk * 128:K_GQA + (blk + 1) * 128],
                                v_ref[:, V_GQA + blk * 128:V_GQA + (blk + 1) * 128],
                                [_lane_mask(128, 0, GQA_HEAD_DIM), _lane_mask(128, GQA_HEAD_DIM, 2 * GQA_HEAD_DIM)])
        o_ref[:, 256 + blk * 128:256 + (blk + 1) * 128] = jnp.where(left, o_l, o_r).astype(o_ref.dtype)

    upper = _lane_mask(MLA_BLK, MLA_NOPE, MLA_BLK)
    for pair in range(MLA_HEADS // 2):
        outs = []
        for hd in (2 * pair, 2 * pair + 1):
            mq = q_ref[:, Q_MLA + hd * MLA_BLK:Q_MLA + (hd + 1) * MLA_BLK]
            mk = k_ref[:, K_MLA + hd * MLA_BLK:K_MLA + (hd + 1) * MLA_BLK]
            mv = v_ref[:, V_MLA + hd * MLA_BLK:V_MLA + (hd + 1) * MLA_BLK]
            e, r = _exp_scores(_dot_nt(mq, mk))
            outs.append(_dot(e, mv * upper) * r)
        both = pltpu.roll(outs[0], MLA_V, axis=1) + outs[1]
        o_ref[:, 512 + pair * 128:512 + (pair + 1) * 128] = both.astype(o_ref.dtype)


def _attention(q, k_own, v_own, vecs, layer, lam_init, n_batch, tq_total, k_cache=None, v_cache=None):
    tq = TOKEN_TILE
    nq = tq_total // tq
    past = 0 if k_cache is None else k_cache.shape[1]
    tk = past + tq_total
    own = lambda cols: pl.BlockSpec((tq_total, cols), lambda i: (i // nq, 0))
    in_specs = [pl.BlockSpec((tq, Q_COLS), lambda i: (i, 0))]
    scratch = []
    if past:
        cached = lambda cols: pl.BlockSpec((None, past, cols), lambda i: (i // nq, 0, 0))
        in_specs += [cached(K_COLS), own(K_COLS), cached(V_COLS), own(V_COLS)]
        args = [q, k_cache, k_own, v_cache, v_own]
        scratch = [pltpu.VMEM((tk, K_COLS), BF16), pltpu.VMEM((tk, V_COLS), BF16)]
    else:
        in_specs += [own(K_COLS), own(V_COLS)]
        args = [q, k_own, v_own]
    in_specs.append(_layer_spec((1, VEC_END), layer))
    args.append(vecs)
    return pl.pallas_call(
        functools.partial(_attn_kernel, lam_init=lam_init, n_q=nq, past=past),
        grid=(n_batch * nq,),
        in_specs=in_specs,
        out_specs=pl.BlockSpec((tq, O_COLS), lambda i: (i, 0)),
        out_shape=jax.ShapeDtypeStruct((n_batch * tq_total, O_COLS), BF16),
        scratch_shapes=scratch,
        compiler_params=pltpu.CompilerParams(vmem_limit_bytes=VMEM_LIMIT),
        name="attention_tk%d" % tk,
    )(*args)


def _gelu_tanh(x):
    return 0.5 * x * (1.0 + jnp.tanh(math.sqrt(2.0 / math.pi) * (x + 0.044715 * (x * x * x))))


def _block_diag_lanes(blocks):
    pair = jnp.concatenate([blocks, blocks], axis=-1)
    tiled = jnp.concatenate([pair] * (SSM_LANES // 128), axis=-1)
    r = jnp.right_shift(lax.broadcasted_iota(jnp.int32, tiled.shape, 0), 4)
    c = jnp.right_shift(lax.broadcasted_iota(jnp.int32, tiled.shape, 1), 6)
    return jnp.where(r == c, tiled, 0.0)


def _ssm_kernel(*refs, n_batch, emit_state):
    uf_ref, ub_ref, h0_ref, par_ref, bre_ref, bim_ref, cre_ref, cim_ref, yf_ref, yb_ref = refs[:10]
    pos = 10
    if emit_state:
        hfin_ref = refs[pos]
        pos += 1
    bbar_scr, cblk_scr, hst_scr, bu_scr, hb_scr = refs[pos:]

    step = pl.program_id(0)
    t_chunk = uf_ref.shape[1]
    rows = n_batch * t_chunk
    abar = []
    for d in range(2):
        a_re = par_ref[d, 0:1, :]
        a_im = par_ref[d, 1:2, :]
        dt = jnp.exp(par_ref[d, 2:3, :])
        mag = jnp.exp(a_re * dt)
        abar.append((mag * jnp.cos(a_im * dt), mag * jnp.sin(a_im * dt)))

    @pl.when(step == 0)
    def _():
        for d in range(2):
            a_re = par_ref[d, 0:1, :]
            a_im = par_ref[d, 1:2, :]
            abr, abi = abar[d]
            den = a_re * a_re + a_im * a_im
            xr = abr - 1.0
            cr = (xr * a_re + abi * a_im) / den
            ci = (abi * a_re - xr * a_im) / den
            bre = _block_diag_lanes(bre_ref[d])
            bim = _block_diag_lanes(bim_ref[d])
            bbar_scr[d, :, 0:SSM_LANES] = (cr * bre - ci * bim).astype(BF16)
            bbar_scr[d, :, SSM_LANES:2 * SSM_LANES] = (cr * bim + ci * bre).astype(BF16)
            cblk_scr[d, :, 0:SSM_LANES] = _block_diag_lanes(cre_ref[d]).astype(BF16)
            cblk_scr[d, :, SSM_LANES:2 * SSM_LANES] = (-_block_diag_lanes(cim_ref[d])).astype(BF16)
        hst_scr[...] = h0_ref[...]

    for d, u_ref in enumerate((uf_ref, ub_ref)):
        u_tm = jnp.swapaxes(u_ref[...], 0, 1).reshape(rows, SSM_WIDTH)
        bu_scr[d] = _dot(u_tm.astype(BF16), bbar_scr[d])

    lw = 128
    if n_batch % 8 == 0:
        for j in range(SSM_LANES // lw):
            sre = slice(j * lw, (j + 1) * lw)
            sim = slice(SSM_LANES + j * lw, SSM_LANES + (j + 1) * lw)
            coef = [(jnp.broadcast_to(abar[d][0][:, sre], (n_batch, lw)),
                     jnp.broadcast_to(abar[d][1][:, sre], (n_batch, lw))) for d in range(2)]

            def body(i, carry, sre=sre, sim=sim, coef=coef):
                new = []
                for d in range(2):
                    hr, hi = carry[2 * d], carry[2 * d + 1]
                    ar, ai = coef[d]
                    t = i if d == 0 else t_chunk - 1 - i
                    r0 = pl.multiple_of(t * n_batch, n_batch)
                    nr = ar * hr - ai * hi + bu_scr[d, pl.ds(r0, n_batch), sre]
                    ni = ar * hi + ai * hr + bu_scr[d, pl.ds(r0, n_batch), sim]
                    hb_scr[d, pl.ds(r0, n_batch), sre] = nr.astype(BF16)
                    hb_scr[d, pl.ds(r0, n_batch), sim] = ni.astype(BF16)
                    new += [nr, ni]
                return tuple(new)

            init = (hst_scr[0, :, sre], hst_scr[0, :, sim], hst_scr[1, :, sre], hst_scr[1, :, sim])
            fin = lax.fori_loop(0, t_chunk, body, init, unroll=2)
            hst_scr[0, :, sre] = fin[0]
            hst_scr[0, :, sim] = fin[1]
            hst_scr[1, :, sre] = fin[2]
            hst_scr[1, :, sim] = fin[3]
    else:
        assert n_batch == 4
        n_blk = rows // 8
        low = lax.broadcasted_iota(jnp.int32, (8, lw), 0) < 4
        firsts = (low, jnp.logical_not(low))
        for j in range(SSM_LANES // lw):
            sre = slice(j * lw, (j + 1) * lw)
            sim = slice(SSM_LANES + j * lw, SSM_LANES + (j + 1) * lw)
            coef = []
            for d in range(2):
                ar = jnp.broadcast_to(abar[d][0][:, sre], (8, lw))
                ai = jnp.broadcast_to(abar[d][1][:, sre], (8, lw))
                coef.append((ar, ai, jnp.where(firsts[d], ar, ar * ar - ai * ai),
                             jnp.where(firsts[d], ai, 2.0 * ar * ai)))

            def body(i, carry, sre=sre, sim=sim, coef=coef):
                new = []
                for d in range(2):
                    pr, pi = carry[2 * d], carry[2 * d + 1]
                    ar, ai, cr, ci = coef[d]
                    first = firsts[d]
                    r16 = pl.multiple_of((2 * i if d == 0 else n_blk - 2 - 2 * i) * 8, 16)
                    tile_r, tile_i = [None, None], [None, None]
                    for half in ((0, 1) if d == 0 else (1, 0)):
                        vr = bu_scr[d, pl.ds(r16 + 8 * half, 8), sre]
                        vi = bu_scr[d, pl.ds(r16 + 8 * half, 8), sim]
                        sr = jnp.where(first, 0.0, pltpu.roll(vr, 4, axis=0))
                        si = jnp.where(first, 0.0, pltpu.roll(vi, 4, axis=0))
                        nr = cr * pr - ci * pi + (vr + (ar * sr - ai * si))
                        ni = cr * pi + ci * pr + (vi + (ar * si + ai * sr))
                        tile_r[half], tile_i[half] = nr, ni
                        pr = jnp.where(first, pltpu.roll(nr, 4, axis=0), nr)
                        pi = jnp.where(first, pltpu.roll(ni, 4, axis=0), ni)
                    hb_scr[d, pl.ds(r16, 16), sre] = jnp.concatenate(tile_r, axis=0).astype(BF16)
                    hb_scr[d, pl.ds(r16, 16), sim] = jnp.concatenate(tile_i, axis=0).astype(BF16)
                    new += [pr, pi]
                return tuple(new)

            init = (hst_scr[0, :, sre], hst_scr[0, :, sim], hst_scr[1, :, sre], hst_scr[1, :, sim])
            fin = lax.fori_loop(0, n_blk // 2, body, init, unroll=2)
            hst_scr[0, :, sre] = fin[0]
            hst_scr[0, :, sim] = fin[1]
            hst_scr[1, :, sre] = fin[2]
            hst_scr[1, :, sim] = fin[3]

    for d, y_ref in enumerate((yf_ref, yb_ref)):
        y = _dot_nt(hb_scr[d], cblk_scr[d])
        y_ref[...] = jnp.swapaxes(y.reshape(t_chunk, n_batch, SSM_WIDTH), 0, 1)

    if emit_state:
        @pl.when(step == pl.num_programs(0) - 1)
        def _():
            hfin_ref[...] = hst_scr[...]


def _ssm_scan(u_tok, h0, h0_spec, wts, layer, n_batch, seq, t_chunk, emit_state):
    n_steps = seq // t_chunk
    rows = n_batch * t_chunk
    h_rows = h0.shape[-2]
    fwd_blk = lambda k: (0, k, 0)
    bwd_blk = lambda k: (0, n_steps - 1 - k, 0)
    chunk = (n_batch, t_chunk, SSM_WIDTH)

    def par(shape):
        return _layer_spec((2,) + shape, layer)

    out_shape = [jax.ShapeDtypeStruct((n_batch, seq, SSM_WIDTH), F32)] * 2
    out_specs = [pl.BlockSpec(chunk, fwd_blk), pl.BlockSpec(chunk, bwd_blk)]
    if emit_state:
        out_shape.append(jax.ShapeDtypeStruct((2, h_rows, 2 * SSM_LANES), F32))
        out_specs.append(_const_spec((2, h_rows, 2 * SSM_LANES)))
    u3 = u_tok.reshape(n_batch, seq, SSM_WIDTH)
    return pl.pallas_call(
        functools.partial(_ssm_kernel, n_batch=n_batch, emit_state=emit_state),
        grid=(n_steps,),
        in_specs=[pl.BlockSpec(chunk, fwd_blk), pl.BlockSpec(chunk, bwd_blk), h0_spec, par((3, SSM_LANES)),
                  par((SSM_WIDTH, SSM_STATE)), par((SSM_WIDTH, SSM_STATE)),
                  par((SSM_WIDTH, SSM_STATE)), par((SSM_WIDTH, SSM_STATE))],
        out_specs=out_specs,
        out_shape=out_shape,
        scratch_shapes=[pltpu.VMEM((2, SSM_WIDTH, 2 * SSM_LANES), BF16),
                        pltpu.VMEM((2, SSM_WIDTH, 2 * SSM_LANES), BF16),
                        pltpu.VMEM((2, h_rows, 2 * SSM_LANES), F32),
                        pltpu.VMEM((2, rows, 2 * SSM_LANES), F32),
                        pltpu.VMEM((2, rows, 2 * SSM_LANES), BF16)],
        compiler_params=pltpu.CompilerParams(vmem_limit_bytes=VMEM_LIMIT),
        name="ssm_b%d" % n_batch,
    )(u3, u3, h0, wts["ssm_par"], wts["ssm_b_re"], wts["ssm_b_im"], wts["ssm_c_re"], wts["ssm_c_im"])


def _out_mlp_kernel(x_ref, oa_ref, yf_ref, yb_ref, u_ref, mod_ref, vec_ref, wout_ref, wglu_ref, w1_ref, w2_ref,
                    y_ref, *, final):
    g1 = mod_ref[:, 2 * D_MODEL:3 * D_MODEL]
    sh2 = mod_ref[:, 3 * D_MODEL:4 * D_MODEL]
    sc2 = mod_ref[:, 4 * D_MODEL:5 * D_MODEL]
    g2 = mod_ref[:, 5 * D_MODEL:6 * D_MODEL]
    ys = yf_ref[...] + yb_ref[...] + u_ref[...] * vec_ref[:, VEC_SSMD:VEC_SSMD + SSM_WIDTH]
    z = _dot(_gelu_tanh(ys).astype(BF16), wglu_ref[...])
    o_ssm = (z[:, 0:SSM_WIDTH] * _sigmoid(z[:, SSM_WIDTH:2 * SSM_WIDTH])).astype(BF16)
    mix = (_dot(oa_ref[:, 0:512], wout_ref[0:512, :]) + _dot(o_ssm, wout_ref[512:768, :])
           + _dot(oa_ref[:, 512:O_COLS], wout_ref[768:1024, :]))
    x1 = x_ref[...] + g1 * mix
    ms = jnp.mean(x1 * x1, axis=-1, keepdims=True)
    h = x1 * lax.rsqrt(ms + EPS) * vec_ref[:, VEC_N2G:VEC_N2G + D_MODEL]
    h = h * (1.0 + sc2) + sh2
    hb = h.astype(BF16)
    mlp = jnp.zeros_like(x1)
    for c0 in range(0, D_FF, FF_CHUNK):
        a = jnp.maximum(_dot(hb, w1_ref[:, c0:c0 + FF_CHUNK]), 0.0)
        mlp = mlp + _dot((a * a).astype(BF16), w2_ref[c0:c0 + FF_CHUNK, :])
    x2 = x1 + g2 * mlp
    if final:
        ms2 = jnp.mean(x2 * x2, axis=-1, keepdims=True)
        x2 = x2 * lax.rsqrt(ms2 + EPS) * vec_ref[:, VEC_FNG:VEC_FNG + D_MODEL]
    y_ref[...] = x2


def _out_mlp(x2d, o_attn, y_fwd, y_bwd, u, mod3, layer, wts, n_batch, seq, latent, final):
    tm = MLP_TILE
    n_tok = n_batch * seq
    if latent:
        tpb = seq // tm
        mod_row = lambda i: (layer * MOD_ROWS + 1 + i // tpb, 0, 0)
    else:
        mod_row = lambda i: (layer * MOD_ROWS, 0, 0)
    tile = lambda i: (i, 0)
    return pl.pallas_call(
        functools.partial(_out_mlp_kernel, final=final),
        grid=(n_tok // tm,),
        in_specs=[
            pl.BlockSpec((tm, D_MODEL), tile),
            pl.BlockSpec((tm, O_COLS), tile),
            pl.BlockSpec((tm, SSM_WIDTH), tile),
            pl.BlockSpec((tm, SSM_WIDTH), tile),
            pl.BlockSpec((tm, SSM_WIDTH), tile),
            pl.BlockSpec((None, 1, N_MOD * D_MODEL), mod_row),
            _layer_spec((1, VEC_END), layer),
            _layer_spec((D_MODEL, D_MODEL), layer, single_buffer=True),
            _layer_spec((SSM_WIDTH, 2 * SSM_WIDTH), layer, single_buffer=True),
            _layer_spec((D_MODEL, D_FF), layer, single_buffer=True),
            _layer_spec((D_FF, D_MODEL), layer, single_buffer=True),
        ],
        out_specs=pl.BlockSpec((tm, D_MODEL), tile),
        out_shape=jax.ShapeDtypeStruct((n_tok, D_MODEL), F32),
        compiler_params=pltpu.CompilerParams(vmem_limit_bytes=VMEM_LIMIT),
        name="out_mlp_%s%s" % ("lat" if latent else "ctx", "_final" if final else ""),
    )(x2d, o_attn, y_fwd, y_bwd, u, mod3, wts["vecs"], wts["w_out"], wts["w_glu"], wts["w1"], wts["w2"])


def _rope_tables(t_len, chunk, n_chunks):
    n = chunk // 4
    rows = t_len // GRID_W
    row = np.repeat(np.arange(rows), GRID_W).astype(np.float32)
    col = np.tile(np.arange(GRID_W), rows).astype(np.float32)
    freq = (np.float32(ROPE_THETA) ** (-np.arange(n, dtype=np.float32) / np.float32(n))).astype(np.float32)
    ang_r = (row[:, None] * freq).astype(np.float32)
    ang_c = (col[:, None] * freq).astype(np.float32)
    cr, sr, cc, sc = np.cos(ang_r), np.sin(ang_r), np.cos(ang_c), np.sin(ang_c)
    z = np.zeros_like(cr)
    tabs = np.stack([np.concatenate([cr, cr, cc, cc], axis=-1),
                     np.concatenate([-sr, z, -sc, z], axis=-1),
                     np.concatenate([z, sr, z, sc], axis=-1)]).astype(np.float32)
    return np.tile(tabs, (1, 1, n_chunks))


def _mla_rope_tables(t_len):
    tabs = _rope_tables(t_len, MLA_ROPE, 1)
    ident = np.stack([np.ones((t_len, MLA_NOPE), np.float32), np.zeros((t_len, MLA_NOPE), np.float32),
                      np.zeros((t_len, MLA_NOPE), np.float32)])
    tail = ident[:, :, :MLA_BLK - MLA_QK]
    return np.concatenate([ident, tabs, tail], axis=-1)


def _pack_vectors(p, n_layers):
    def rows(a):
        return a.reshape(n_layers, 1, -1)

    fng = jnp.broadcast_to(p["final_norm_g"].reshape(1, 1, D_MODEL), (n_layers, 1, D_MODEL))
    return jnp.concatenate([
        rows(p["norm1_g"]), rows(p["norm2_g"]), fng,
        jnp.tile(rows(p["gqa_qn_g"]), (1, 1, GQA_HEADS)),
        jnp.tile(rows(p["gqa_kn_g"]), (1, 1, GQA_KV_HEADS)),
        rows(p["mla_qn_g"]), jnp.zeros((n_layers, 1, CQ_PAD - MLA_Q_RANK), F32),
        rows(p["mla_kvn_g"]),
        jnp.tile(rows(p["diff_subln_g"]), (1, 1, DIFF_HEADS)),
        rows(p["ssm_d"]),
        rows(p["diff_lq1"]), rows(p["diff_lk1"]), rows(p["diff_lq2"]), rows(p["diff_lk2"]),
    ], axis=-1)


def kernel(x_prompt, x_sample, cache_diff_k, cache_diff_v, cache_gqa_k, cache_gqa_v, cache_mla_ckv, cache_mla_krope, state_ssm_re, state_ssm_im, c, c_ctx, norm1_g, norm2_g, w_ada, b_ada, w_in, w_out, diff_lq1, diff_lk1, diff_lq2, diff_lk2, diff_subln_g, gqa_qn_g, gqa_kn_g, ssm_a_re, ssm_a_im, ssm_log_dt, ssm_b_re, ssm_b_im, ssm_c_re, ssm_c_im, ssm_d, ssm_w_glu, mla_qn_g, mla_kvn_g, mla_w_uq, mla_w_ukv, mlp_w1, mlp_w2, final_norm_g):
    n_layers = w_in.shape[0]
    n_ctx, seq, _ = x_prompt.shape
    n_lat, t_lat, _ = x_sample.shape
    past = cache_diff_k.shape[2]
    assert n_lat + 1 <= MOD_ROWS and n_lat == 4 and n_ctx % 8 == 0
    assert seq == TOKEN_TILE and t_lat % TOKEN_TILE == 0
    assert (n_ctx * seq) % MLP_TILE == 0 and t_lat % MLP_TILE == 0
    assert PROJ_TILE % seq == 0 and n_ctx % (PROJ_TILE // seq) == 0 and t_lat % PROJ_TILE == 0

    p = dict(norm1_g=norm1_g, norm2_g=norm2_g, final_norm_g=final_norm_g, gqa_qn_g=gqa_qn_g, gqa_kn_g=gqa_kn_g,
             mla_qn_g=mla_qn_g, mla_kvn_g=mla_kvn_g, diff_subln_g=diff_subln_g, ssm_d=ssm_d, diff_lq1=diff_lq1,
             diff_lk1=diff_lk1, diff_lq2=diff_lq2, diff_lk2=diff_lk2)
    ssm_rows = SSM_GROUPS * SSM_GROUP
    wts = {
        "vecs": _pack_vectors(p, n_layers),
        "w_in_t": jnp.swapaxes(w_in, 1, 2),
        "w_uq": mla_w_uq, "w_ukv": mla_w_ukv,
        "w_out": w_out.astype(BF16), "w1": mlp_w1.astype(BF16), "w2": mlp_w2.astype(BF16),
        "w_glu": ssm_w_glu.astype(BF16),
        "ssm_par": jnp.stack([ssm_a_re.reshape(n_layers, 2, SSM_LANES), ssm_a_im.reshape(n_layers, 2, SSM_LANES),
                              jnp.repeat(ssm_log_dt, SSM_STATE, axis=-1)], axis=2),
        "ssm_b_re": jnp.swapaxes(ssm_b_re, 3, 4).reshape(n_layers, 2, ssm_rows, SSM_STATE),
        "ssm_b_im": jnp.swapaxes(ssm_b_im, 3, 4).reshape(n_layers, 2, ssm_rows, SSM_STATE),
        "ssm_c_re": ssm_c_re.reshape(n_layers, 2, ssm_rows, SSM_STATE),
        "ssm_c_im": ssm_c_im.reshape(n_layers, 2, ssm_rows, SSM_STATE),
    }

    cond = jnp.concatenate([c_ctx[None], c, jnp.zeros((MOD_ROWS - 1 - n_lat, D_MODEL), F32)], axis=0)
    mod3 = _adaln(cond, w_ada, b_ada).reshape(n_layers * MOD_ROWS, 1, N_MOD * D_MODEL)

    rope_tabs = (jnp.asarray(_rope_tables(t_lat, DIFF_QK, 256 // DIFF_QK)),
                 jnp.asarray(_rope_tables(t_lat, GQA_HEAD_DIM, 256 // GQA_HEAD_DIM)),
                 jnp.asarray(_mla_rope_tables(t_lat)))
    caches = (jnp.transpose(cache_diff_k, (0, 1, 3, 4, 2)).reshape(n_lat, n_layers, 256, past),
              jnp.transpose(cache_diff_v, (0, 1, 3, 4, 2)).reshape(n_lat, n_layers, 256, past),
              jnp.transpose(cache_gqa_k, (0, 1, 3, 4, 2)).reshape(n_lat, n_layers, 128, past),
              jnp.transpose(cache_gqa_v, (0, 1, 3, 4, 2)).reshape(n_lat, n_layers, 128, past),
              cache_mla_ckv,
              jnp.swapaxes(cache_mla_krope, 2, 3))

    lam_inits = [0.8 - 0.6 * math.exp(-0.3 * l) for l in range(n_layers)]

    x = x_prompt.reshape(n_ctx * seq, D_MODEL)
    zero_state = jnp.zeros((2, n_ctx, 2 * SSM_LANES), F32)
    zero_spec = _const_spec((2, n_ctx, 2 * SSM_LANES))
    kept = None
    states = []
    for l in range(n_layers):
        res = _in_proj(x, mod3, l, wts, n_ctx, seq, latent=False, prev_caches=kept)
        q, k, v, u = res[:4]
        kept = res[4:]
        o_attn = _attention(q, k, v, wts["vecs"], l, lam_inits[l], n_ctx, seq)
        y_fwd, y_bwd, h_fin = _ssm_scan(u, zero_state, zero_spec, wts, l, n_ctx, seq, CTX_SSM_CHUNK,
                                        emit_state=True)
        x = _out_mlp(x, o_attn, y_fwd.reshape(n_ctx * seq, SSM_WIDTH), y_bwd.reshape(n_ctx * seq, SSM_WIDTH), u,
                     mod3, l, wts, n_ctx, seq, latent=False, final=(l == n_layers - 1))
        states.append(h_fin)
    y_prompt = x.reshape(n_ctx, seq, D_MODEL)

    x = x_sample.reshape(n_lat * t_lat, D_MODEL)
    h0 = jnp.concatenate([state_ssm_re.reshape(n_lat, n_layers, 2, SSM_LANES),
                          state_ssm_im.reshape(n_lat, n_layers, 2, SSM_LANES)], axis=-1)
    h0 = jnp.transpose(h0, (1, 2, 0, 3))
    h0 = jnp.concatenate([h0, h0], axis=2)
    for l in range(n_layers):
        h0_spec = _layer_spec((2, 2 * n_lat, 2 * SSM_LANES), l)
        q, k, v, u = _in_proj(x, mod3, l, wts, n_lat, t_lat, latent=True, rope_tabs=rope_tabs)
        kc, vc = _cache_prep(caches, l, wts["w_ukv"])
        o_attn = _attention(q, k, v, wts["vecs"], l, lam_inits[l], n_lat, t_lat, k_cache=kc, v_cache=vc)
        y_fwd, y_bwd = _ssm_scan(u, h0, h0_spec, wts, l, n_lat, t_lat, LAT_SSM_CHUNK, emit_state=False)
        x = _out_mlp(x, o_attn, y_fwd.reshape(n_lat * t_lat, SSM_WIDTH), y_bwd.reshape(n_lat * t_lat, SSM_WIDTH), u,
                     mod3, l, wts, n_lat, t_lat, latent=True, final=(l == n_layers - 1))
    y_sample = x.reshape(n_lat, t_lat, D_MODEL)

    cdk, cdv, cgk, cgv, cckv, ckr = kept

    def token_major(a, heads, dim):
        return jnp.transpose(a.reshape(n_ctx, n_layers, heads, dim, seq), (0, 1, 4, 2, 3))

    st = jnp.transpose(jnp.stack(states), (2, 0, 1, 3))
    return (y_prompt, y_sample,
            token_major(cdk, DIFF_HEADS, 2 * DIFF_QK), token_major(cdv, DIFF_HEADS, DIFF_V),
            token_major(cgk, GQA_KV_HEADS, GQA_HEAD_DIM), token_major(cgv, GQA_KV_HEADS, GQA_HEAD_DIM),
            cckv, jnp.swapaxes(ckr, 2, 3),
            st[..., :SSM_LANES].reshape(n_ctx, n_layers, 2, SSM_GROUPS, SSM_STATE),
            st[..., SSM_LANES:].reshape(n_ctx, n_layers, 2, SSM_GROUPS, SSM_STATE))
```

```python
import functools
import math

import numpy as np
import jax
import jax.numpy as jnp
from jax import lax
from jax.experimental import pallas as pl
from jax.experimental.pallas import tpu as pltpu

F32 = jnp.float32
BF16 = jnp.bfloat16

D_MODEL = 1024
GRID_W = 64
ROPE_THETA = 10000.0
EPS = 1e-6
DIFF_HEADS, DIFF_QK, DIFF_V = 4, 32, 64
GQA_HEADS, GQA_KV_HEADS, GQA_HEAD_DIM = 4, 2, 64
SSM_WIDTH, SSM_GROUP, SSM_STATE = 256, 16, 64
SSM_GROUPS = SSM_WIDTH // SSM_GROUP
SSM_LANES = SSM_GROUPS * SSM_STATE
MLA_HEADS, MLA_Q_RANK, MLA_KV_RANK, MLA_NOPE, MLA_ROPE, MLA_V = 4, 192, 128, 64, 32, 64
MLA_QK = MLA_NOPE + MLA_ROPE
D_FF = 4 * D_MODEL
N_MOD = 6
LOG2E = math.log2(math.e)
IN_COLS = 1888
MOD_ROWS = 8

P_DQ, P_DK, P_DV, P_GQ, P_GK, P_GV, P_U, P_CQ, P_CKV, P_KR, P_END = (
    0, 256, 512, 768, 1024, 1152, 1280, 1536, 1792, 1920, 2048)
S_CQ_END, S_CKV, S_KR = 1728, 1728, 1856
CQ_PAD = P_CKV - P_CQ
MLA_BLK = 128
Q_DIFF, Q_GQA, Q_MLA, Q_COLS = 0, 256, 512, 1024
K_DIFF, K_GQA, K_MLA, K_COLS = 0, 256, 512, 1024
V_DIFF, V_GQA, V_MLA, V_COLS = 0, 256, 512, 1024
O_COLS = 768

VEC_N1G, VEC_N2G, VEC_FNG, VEC_QNG, VEC_KNG, VEC_MQNG, VEC_KVNG, VEC_SUBG, VEC_SSMD, VEC_LAM, VEC_END = (
    0, 1024, 2048, 3072, 3328, 3456, 3712, 3840, 4096, 4352, 4480)

TOKEN_TILE = 256
PROJ_TILE = 512
MLP_TILE = 512
FF_CHUNK = 1024
CTX_SSM_CHUNK = 32
LAT_SSM_CHUNK = 256
STACK_MAX_KEYS = 512
VMEM_LIMIT = 48 * 1024 * 1024


def _dot(a, b):
    return jnp.dot(a, b, preferred_element_type=F32)


def _dot_nt(a, b):
    return lax.dot_general(a, b, (((1,), (1,)), ((), ())), preferred_element_type=F32)


def _block_ones(width, seg):
    shift = seg.bit_length() - 1
    r = jnp.right_shift(lax.broadcasted_iota(jnp.int32, (width, width), 0), shift)
    c = jnp.right_shift(lax.broadcasted_iota(jnp.int32, (width, width), 1), shift)
    return jnp.where(r == c, 1.0, 0.0).astype(BF16)


def _seg_sum_sq(y, seg):
    ones = _block_ones(y.shape[-1], seg)
    sq = y * y
    hi = sq.astype(BF16)
    lo = (sq - hi.astype(F32)).astype(BF16)
    return _dot(hi, ones) + _dot(lo, ones)


def _rope(x, tab_ref, quarter):
    outs = []
    for j in range(x.shape[-1] // 128):
        sl = slice(j * 128, (j + 1) * 128)
        xs = x[:, sl]
        up = pltpu.roll(xs, 128 - quarter, axis=1)
        dn = pltpu.roll(xs, quarter, axis=1)
        outs.append(xs * tab_ref[0, :, sl] + up * tab_ref[1, :, sl] + dn * tab_ref[2, :, sl])
    return outs[0] if len(outs) == 1 else jnp.concatenate(outs, axis=-1)


def _sigmoid(x):
    return 1.0 / (1.0 + jnp.exp(-x))


def _lane_lt(width, bound):
    return lax.broadcasted_iota(jnp.int32, (1, width), 1) < bound


def _dup_halves(x):
    swapped = pltpu.roll(x, 64, axis=1)
    left = _lane_lt(128, 64)
    return jnp.where(left, x, swapped), jnp.where(left, swapped, x)


def _const_spec(shape):
    nd = len(shape)
    return pl.BlockSpec(shape, lambda *_: (0,) * nd)


def _layer_spec(shape, layer, single_buffer=False):
    nd = len(shape)
    kw = {"pipeline_mode": pl.Buffered(1)} if single_buffer else {}
    return pl.BlockSpec((None,) + tuple(shape), lambda *_: (layer,) + (0,) * nd, **kw)


def _adaln_kernel(cond_ref, w_ref, b_ref, o_ref):
    c = cond_ref[...]
    s = c * _sigmoid(c)
    o_ref[...] = _dot(s.astype(BF16), w_ref[...].astype(BF16)) + b_ref[...]


def _adaln(cond, w_ada, b_ada):
    n_layers = w_ada.shape[0]
    tn = 1024
    return pl.pallas_call(
        _adaln_kernel,
        grid=(n_layers, N_MOD * D_MODEL // tn),
        in_specs=[
            pl.BlockSpec((MOD_ROWS, D_MODEL), lambda l, j: (0, 0)),
            pl.BlockSpec((None, D_MODEL, tn), lambda l, j: (l, 0, j)),
            pl.BlockSpec((None, 1, tn), lambda l, j: (l, 0, j)),
        ],
        out_specs=pl.BlockSpec((None, MOD_ROWS, tn), lambda l, j: (l, 0, j)),
        out_shape=jax.ShapeDtypeStruct((n_layers, MOD_ROWS, N_MOD * D_MODEL), F32),
        compiler_params=pltpu.CompilerParams(vmem_limit_bytes=VMEM_LIMIT),
        name="adaln",
    )(cond, w_ada, b_ada.reshape(n_layers, 1, N_MOD * D_MODEL))


N_CACHE = 6


def _in_proj_kernel(*refs, latent, n_alias, layer):
    x_ref, mod_ref, vec_ref, wint_ref, wuq_ref, wukv_ref = refs[:6]
    pos = 6
    if latent:
        rope_d_ref, rope_g_ref, rope_m_ref = refs[pos:pos + 3]
        pos += 3
    pos += n_alias
    q_ref, k_ref, v_ref, u_ref = refs[pos:pos + 4]
    pos += 4
    if not latent:
        cdk_ref, cdv_ref, cgk_ref, cgv_ref, cckv_ref, ckr_ref = refs[pos:pos + N_CACHE]
        pos += N_CACHE
    win_scr, wuq_scr, wukv_scr = refs[pos:]

    @pl.when(pl.program_id(0) == 0)
    def _():
        aligned = (S_CQ_END // 128) * 128
        for r0 in range(0, aligned, 256):
            r1 = min(r0 + 256, aligned)
            win_scr[:, r0:r1] = wint_ref[r0:r1, :].T.astype(BF16)
        tail = wint_ref[aligned:aligned + 128, :].T
        win_scr[:, aligned:aligned + 128] = jnp.where(_lane_lt(128, S_CQ_END - aligned), tail, 0.0).astype(BF16)
        win_scr[:, P_CKV:P_KR] = wint_ref[S_CKV:S_KR, :].T.astype(BF16)
        kr_rows = jnp.concatenate([jnp.zeros((MLA_NOPE, D_MODEL), F32), wint_ref[S_KR:IN_COLS, :],
                                   jnp.zeros((MLA_BLK - MLA_QK, D_MODEL), F32)], axis=0)
        win_scr[:, P_KR:P_END] = kr_rows.T.astype(BF16)
        wuq_scr[...] = jnp.zeros(wuq_scr.shape, BF16)
        wuq = wuq_ref[...]
        for hd in range(MLA_HEADS):
            wuq_scr[0:MLA_Q_RANK, hd * MLA_BLK:hd * MLA_BLK + MLA_QK] = (
                wuq[:, hd * MLA_QK:(hd + 1) * MLA_QK].astype(BF16))
        wukv_scr[...] = wukv_ref[...].astype(BF16)

    x = x_ref[...]
    sh1 = mod_ref[:, 0:D_MODEL]
    sc1 = mod_ref[:, D_MODEL:2 * D_MODEL]
    ms = jnp.mean(x * x, axis=-1, keepdims=True)
    h = x * lax.rsqrt(ms + EPS) * vec_ref[:, VEC_N1G:VEC_N1G + D_MODEL]
    h = h * (1.0 + sc1) + sh1
    proj = _dot(h.astype(BF16), win_scr[...])

    dq = proj[:, P_DQ:P_DK]
    dk = proj[:, P_DK:P_DV]
    dv = proj[:, P_DV:P_GQ]
    gq = proj[:, P_GQ:P_GK]
    gk = proj[:, P_GK:P_GV]
    gv = proj[:, P_GV:P_U]
    u_ref[...] = proj[:, P_U:P_CQ]
    cq = proj[:, P_CQ:P_CKV]
    ckv = proj[:, P_CKV:P_KR]
    krb = proj[:, P_KR:P_END]

    qng = vec_ref[:, VEC_QNG:VEC_QNG + 256]
    kng = vec_ref[:, VEC_KNG:VEC_KNG + 128]
    gq = gq * lax.rsqrt(_seg_sum_sq(gq, GQA_HEAD_DIM) * (1.0 / GQA_HEAD_DIM) + EPS) * qng
    gk = gk * lax.rsqrt(_seg_sum_sq(gk, GQA_HEAD_DIM) * (1.0 / GQA_HEAD_DIM) + EPS) * kng
    ckv_n = (ckv * lax.rsqrt(jnp.mean(ckv * ckv, axis=-1, keepdims=True) + EPS)
             * vec_ref[:, VEC_KVNG:VEC_KVNG + MLA_KV_RANK])
    cq_ms = jnp.sum(cq * cq, axis=-1, keepdims=True) * (1.0 / MLA_Q_RANK)
    cq_n = cq * lax.rsqrt(cq_ms + EPS) * vec_ref[:, VEC_MQNG:VEC_MQNG + CQ_PAD]
    mq = _dot(cq_n.astype(BF16), wuq_scr[...])
    kv = _dot(ckv_n.astype(BF16), wukv_scr[...])

    if not latent:
        seq = cckv_ref.shape[2]

        def keep(ref, val, feature_major=True, rows=None):
            for s in range(ref.shape[0]):
                piece = val[s * seq:(s + 1) * seq]
                if feature_major:
                    piece = piece.T
                if rows is not None:
                    piece = piece[rows]
                for l in range(ref.shape[1]):
                    ref[s, l] = piece if ref.shape[1] == 1 or l == layer else jnp.zeros_like(piece)

        keep(cdk_ref, dk)
        keep(cdv_ref, dv)
        keep(cgk_ref, gk)
        keep(cgv_ref, gv)
        keep(cckv_ref, ckv_n, feature_major=False)
        keep(ckr_ref, krb, rows=slice(MLA_NOPE, MLA_QK))
    else:
        dq = _rope(dq, rope_d_ref, DIFF_QK // 4)
        dk = _rope(dk, rope_d_ref, DIFF_QK // 4)
        gq = _rope(gq, rope_g_ref, GQA_HEAD_DIM // 4)
        gk = _rope(gk, rope_g_ref, GQA_HEAD_DIM // 4)
        mq = jnp.concatenate(
            [_rope(mq[:, hd * MLA_BLK:(hd + 1) * MLA_BLK], rope_m_ref, MLA_ROPE // 4)
             for hd in range(MLA_HEADS)], axis=-1)
        krb = _rope(krb, rope_m_ref, MLA_ROPE // 4)

    q_ref[:, Q_DIFF:Q_GQA] = (dq * (LOG2E * DIFF_QK ** -0.5)).astype(BF16)
    q_ref[:, Q_GQA:Q_MLA] = (gq * (LOG2E * GQA_HEAD_DIM ** -0.5)).astype(BF16)
    q_ref[:, Q_MLA:Q_COLS] = (mq * (LOG2E * MLA_QK ** -0.5)).astype(BF16)
    k_ref[:, K_DIFF:K_GQA] = dk.astype(BF16)
    gk0, gk1 = _dup_halves(gk)
    k_ref[:, K_GQA:K_GQA + 128] = gk0.astype(BF16)
    k_ref[:, K_GQA + 128:K_MLA] = gk1.astype(BF16)
    nope = _lane_lt(MLA_BLK, MLA_NOPE)
    for hd in range(MLA_HEADS):
        blk = slice(hd * MLA_BLK, (hd + 1) * MLA_BLK)
        k_ref[:, K_MLA + hd * MLA_BLK:K_MLA + (hd + 1) * MLA_BLK] = jnp.where(nope, kv[:, blk], krb).astype(BF16)
    v_ref[:, V_DIFF:V_GQA] = dv.astype(BF16)
    gv0, gv1 = _dup_halves(gv)
    v_ref[:, V_GQA:V_GQA + 128] = gv0.astype(BF16)
    v_ref[:, V_GQA + 128:V_MLA] = gv1.astype(BF16)
    v_ref[:, V_MLA:V_COLS] = kv.astype(BF16)


def _in_proj(x2d, mod3, layer, wts, n_batch, seq, latent, rope_tabs=None, prev_caches=None):
    tm = PROJ_TILE
    n_tok = n_batch * seq
    n_layers = wts["vecs"].shape[0]
    if latent:
        tpb = seq // tm
        mod_row = lambda i: (layer * MOD_ROWS + 1 + i // tpb, 0, 0)
    else:
        mod_row = lambda i: (layer * MOD_ROWS, 0, 0)
    tile = lambda i: (i, 0)
    in_specs = [
        pl.BlockSpec((tm, D_MODEL), tile),
        pl.BlockSpec((None, 1, N_MOD * D_MODEL), mod_row),
        _layer_spec((1, VEC_END), layer),
        _layer_spec((IN_COLS, D_MODEL), layer, single_buffer=True),
        _layer_spec((MLA_Q_RANK, MLA_HEADS * MLA_QK), layer, single_buffer=True),
        _layer_spec((MLA_KV_RANK, MLA_HEADS * MLA_BLK), layer, single_buffer=True),
    ]
    args = [x2d, mod3, wts["vecs"], wts["w_in_t"], wts["w_uq"], wts["w_ukv"]]
    out_shape = [
        jax.ShapeDtypeStruct((n_tok, Q_COLS), BF16),
        jax.ShapeDtypeStruct((n_tok, K_COLS), BF16),
        jax.ShapeDtypeStruct((n_tok, V_COLS), BF16),
        jax.ShapeDtypeStruct((n_tok, SSM_WIDTH), F32),
    ]
    out_specs = [
        pl.BlockSpec((tm, Q_COLS), tile),
        pl.BlockSpec((tm, K_COLS), tile),
        pl.BlockSpec((tm, V_COLS), tile),
        pl.BlockSpec((tm, SSM_WIDTH), tile),
    ]
    aliases = {}
    n_alias = 0
    if latent:
        pos = lambda i: (0, i % tpb, 0)
        in_specs += [pl.BlockSpec((3, tm, 256), pos), pl.BlockSpec((3, tm, 256), pos),
                     pl.BlockSpec((3, tm, 128), pos)]
        args += list(rope_tabs)
    else:
        spt = tm // seq
        cache_dims = [(256, seq), (256, seq), (128, seq), (128, seq), (seq, MLA_KV_RANK), (MLA_ROPE, seq)]
        for dims in cache_dims:
            out_shape.append(jax.ShapeDtypeStruct((n_batch, n_layers) + dims, F32))
            if prev_caches is None:
                out_specs.append(pl.BlockSpec((spt, n_layers) + dims, lambda i: (i, 0, 0, 0)))
            else:
                out_specs.append(pl.BlockSpec((spt, 1) + dims, lambda i: (i, layer, 0, 0)))
        if prev_caches is not None:
            n_alias = N_CACHE
            for j, arr in enumerate(prev_caches):
                aliases[len(args)] = 4 + j
                in_specs.append(pl.BlockSpec(memory_space=pl.ANY))
                args.append(arr)
    return pl.pallas_call(
        functools.partial(_in_proj_kernel, latent=latent, n_alias=n_alias, layer=layer),
        grid=(n_tok // tm,),
        in_specs=in_specs,
        out_specs=out_specs,
        out_shape=out_shape,
        input_output_aliases=aliases,
        scratch_shapes=[pltpu.VMEM((D_MODEL, P_END), BF16),
                        pltpu.VMEM((CQ_PAD, MLA_HEADS * MLA_BLK), BF16),
                        pltpu.VMEM((MLA_KV_RANK, MLA_HEADS * MLA_BLK), BF16)],
        compiler_params=pltpu.CompilerParams(vmem_limit_bytes=VMEM_LIMIT),
        name="in_proj_lat" if latent else "in_proj_ctx",
    )(*args)


def _cache_prep_kernel(cdk_ref, cdv_ref, cgk_ref, cgv_ref, cckv_ref, ckr_ref, wukv_ref, k_ref, v_ref):
    past = k_ref.shape[0]
    kv = _dot(cckv_ref[...].astype(BF16), wukv_ref[...].astype(BF16))
    krb = jnp.concatenate([jnp.zeros((MLA_NOPE, past), F32), ckr_ref[...],
                           jnp.zeros((MLA_BLK - MLA_QK, past), F32)], axis=0).T
    k_ref[:, K_DIFF:K_GQA] = cdk_ref[...].T.astype(BF16)
    gk0, gk1 = _dup_halves(cgk_ref[...].T)
    k_ref[:, K_GQA:K_GQA + 128] = gk0.astype(BF16)
    k_ref[:, K_GQA + 128:K_MLA] = gk1.astype(BF16)
    nope = _lane_lt(MLA_BLK, MLA_NOPE)
    for hd in range(MLA_HEADS):
        blk = slice(hd * MLA_BLK, (hd + 1) * MLA_BLK)
        k_ref[:, K_MLA + hd * MLA_BLK:K_MLA + (hd + 1) * MLA_BLK] = jnp.where(nope, kv[:, blk], krb).astype(BF16)
    v_ref[:, V_DIFF:V_GQA] = cdv_ref[...].T.astype(BF16)
    gv0, gv1 = _dup_halves(cgv_ref[...].T)
    v_ref[:, V_GQA:V_GQA + 128] = gv0.astype(BF16)
    v_ref[:, V_GQA + 128:V_MLA] = gv1.astype(BF16)
    v_ref[:, V_MLA:V_COLS] = kv.astype(BF16)


def _cache_prep(caches, layer, w_ukv):
    cdk, cdv, cgk, cgv, cckv, ckr = caches
    n_batch, _, _, past = cdk.shape

    def spec(rows, cols):
        return pl.BlockSpec((None, None, rows, cols), lambda b: (b, layer, 0, 0))

    return pl.pallas_call(
        _cache_prep_kernel,
        grid=(n_batch,),
        in_specs=[spec(256, past), spec(256, past), spec(128, past), spec(128, past),
                  spec(past, MLA_KV_RANK), spec(MLA_ROPE, past),
                  _layer_spec((MLA_KV_RANK, MLA_HEADS * MLA_BLK), layer)],
        out_specs=[pl.BlockSpec((None, past, K_COLS), lambda b: (b, 0, 0)),
                   pl.BlockSpec((None, past, V_COLS), lambda b: (b, 0, 0))],
        out_shape=[jax.ShapeDtypeStruct((n_batch, past, K_COLS), BF16),
                   jax.ShapeDtypeStruct((n_batch, past, V_COLS), BF16)],
        name="cache_prep",
    )(cdk, cdv, cgk, cgv, cckv, ckr, w_ukv)


def _exp_scores(s):
    m = jnp.max(s, axis=-1, keepdims=True)
    e = jnp.exp2(s - m)
    return e.astype(BF16), 1.0 / jnp.sum(e, axis=-1, keepdims=True)


def _lane_mask(width, lo, hi):
    lane = lax.broadcasted_iota(jnp.int32, (1, width), 1)
    return jnp.where((lane >= lo) & (lane < hi), 1.0, 0.0).astype(BF16)


def _attn_kernel(*refs, lam_init, n_q, past, n_cast):
    if past:
        q_ref, kc_ref, ko_ref, vc_ref, vo_ref, vec_ref, o_ref, k_ref, v_ref = refs

        @pl.when(pl.program_id(0) % n_q == 0)
        def _():
            k_ref[0:past, :] = kc_ref[...]
            k_ref[past:, :] = ko_ref[...]
            v_ref[0:past, :] = vc_ref[...]
            v_ref[past:, :] = vo_ref[...]
    else:
        q_ref, k_ref, v_ref, vec_ref = refs[:4]
        o_ref = refs[4 + n_cast]
        for src, dst in zip(refs[4:4 + n_cast], refs[5 + n_cast:]):
            dst[...] = src[...].astype(BF16)
    tq = q_ref.shape[0]
    stack_rows = k_ref.shape[0] <= STACK_MAX_KEYS
    lq1 = vec_ref[:, VEC_LAM:VEC_LAM + 32]
    lk1 = vec_ref[:, VEC_LAM + 32:VEC_LAM + 64]
    lq2 = vec_ref[:, VEC_LAM + 64:VEC_LAM + 96]
    lk2 = vec_ref[:, VEC_LAM + 96:VEC_LAM + 128]
    lam = (jnp.exp(jnp.sum(lq1 * lk1, axis=-1, keepdims=True))
           - jnp.exp(jnp.sum(lq2 * lk2, axis=-1, keepdims=True)) + lam_init)

    def masked_heads(qv, k, v, masks):
        if stack_rows:
            e, r = _exp_scores(_dot_nt(jnp.concatenate([qv * m for m in masks], axis=0), k))
            o = _dot(e, v) * r
            return [o[j * tq:(j + 1) * tq] for j in range(len(masks))]
        outs = []
        for m in masks:
            e, r = _exp_scores(_dot_nt(qv * m, k))
            outs.append(_dot(e, v) * r)
        return outs

    parts = masked_heads(q_ref[:, Q_DIFF:Q_GQA], k_ref[:, K_DIFF:K_GQA], v_ref[:, V_DIFF:V_GQA],
                         [_lane_mask(256, j * DIFF_QK, (j + 1) * DIFF_QK) for j in range(2 * DIFF_HEADS)])
    lane = lax.broadcasted_iota(jnp.int32, (1, 256), 1)
    acc = jnp.zeros((tq, 256), F32)
    for hd in range(DIFF_HEADS):
        head = (lane >= hd * DIFF_V) & (lane < (hd + 1) * DIFF_V)
        acc = jnp.where(head, parts[2 * hd] - lam * parts[2 * hd + 1], acc)
    ms = _seg_sum_sq(acc, DIFF_V) * (1.0 / DIFF_V)
    subg = vec_ref[:, VEC_SUBG:VEC_SUBG + 256]
    o_ref[:, 0:256] = (acc * lax.rsqrt(ms + EPS) * subg * (1.0 - lam_init)).astype(o_ref.dtype)

    left = _lane_lt(128, GQA_HEAD_DIM)
    for blk in range(GQA_KV_HEADS):
        o_l, o_r = masked_heads(q_ref[:, Q_GQA + blk * 128:Q_GQA + (blk + 1) * 128],
                                k_ref[:, K_GQA + blk * 128:K_GQA + (blk + 1) * 128],
                                v_ref[:, V_GQA + blk * 128:V_GQA + (blk + 1) * 128],
                                [_lane_mask(128, 0, GQA_HEAD_DIM), _lane_mask(128, GQA_HEAD_DIM, 2 * GQA_HEAD_DIM)])
        o_ref[:, 256 + blk * 128:256 + (blk + 1) * 128] = jnp.where(left, o_l, o_r).astype(o_ref.dtype)

    upper = _lane_mask(MLA_BLK, MLA_NOPE, MLA_BLK)
    for pair in range(MLA_HEADS // 2):
        outs = []
        for hd in (2 * pair, 2 * pair + 1):
            mq = q_ref[:, Q_MLA + hd * MLA_BLK:Q_MLA + (hd + 1) * MLA_BLK]
            mk = k_ref[:, K_MLA + hd * MLA_BLK:K_MLA + (hd + 1) * MLA_BLK]
            mv = v_ref[:, V_MLA + hd * MLA_BLK:V_MLA + (hd + 1) * MLA_BLK]
            e, r = _exp_scores(_dot_nt(mq, mk))
            outs.append(_dot(e, mv * upper) * r)
        both = pltpu.roll(outs[0], MLA_V, axis=1) + outs[1]
        o_ref[:, 512 + pair * 128:512 + (pair + 1) * 128] = both.astype(o_ref.dtype)


def _attention(q, k_own, v_own, vecs, layer, lam_init, n_batch, tq_total, k_cache=None, v_cache=None, to_bf16=()):
    tq = TOKEN_TILE
    nq = tq_total // tq
    n_steps = n_batch * nq
    past = 0 if k_cache is None else k_cache.shape[1]
    tk = past + tq_total
    own = lambda cols: pl.BlockSpec((tq_total, cols), lambda i: (i // nq, 0))
    in_specs = [pl.BlockSpec((tq, Q_COLS), lambda i: (i, 0))]
    scratch = []
    if past:
        assert not to_bf16
        cached = lambda cols: pl.BlockSpec((None, past, cols), lambda i: (i // nq, 0, 0))
        in_specs += [cached(K_COLS), own(K_COLS), cached(V_COLS), own(V_COLS)]
        args = [q, k_cache, k_own, v_cache, v_own]
        scratch = [pltpu.VMEM((tk, K_COLS), BF16), pltpu.VMEM((tk, V_COLS), BF16)]
    else:
        in_specs += [own(K_COLS), own(V_COLS)]
        args = [q, k_own, v_own]
    in_specs.append(_layer_spec((1, VEC_END), layer))
    args.append(vecs)
    out_specs = [pl.BlockSpec((tq, O_COLS), lambda i: (i, 0))]
    out_shape = [jax.ShapeDtypeStruct((n_batch * tq_total, O_COLS), BF16)]
    for w in to_bf16:
        _, rows, cols = w.shape
        slab = rows // n_steps
        assert slab * n_steps == rows and slab % 16 == 0
        in_specs.append(pl.BlockSpec((None, slab, cols), lambda i: (layer, i, 0)))
        args.append(w)
        out_specs.append(pl.BlockSpec((slab, cols), lambda i: (i, 0)))
        out_shape.append(jax.ShapeDtypeStruct((rows, cols), BF16))
    return pl.pallas_call(
        functools.partial(_attn_kernel, lam_init=lam_init, n_q=nq, past=past, n_cast=len(to_bf16)),
        grid=(n_steps,),
        in_specs=in_specs,
        out_specs=out_specs,
        out_shape=out_shape,
        scratch_shapes=scratch,
        compiler_params=pltpu.CompilerParams(vmem_limit_bytes=VMEM_LIMIT),
        name="attention_tk%d" % tk,
    )(*args)


def _gelu_tanh(x):
    return 0.5 * x * (1.0 + jnp.tanh(math.sqrt(2.0 / math.pi) * (x + 0.044715 * (x * x * x))))


def _block_diag_lanes(blocks):
    pair = jnp.concatenate([blocks, blocks], axis=-1)
    tiled = jnp.concatenate([pair] * (SSM_LANES // 128), axis=-1)
    r = jnp.right_shift(lax.broadcasted_iota(jnp.int32, tiled.shape, 0), 4)
    c = jnp.right_shift(lax.broadcasted_iota(jnp.int32, tiled.shape, 1), 6)
    return jnp.where(r == c, tiled, 0.0)


def _ssm_kernel(*refs, n_batch, emit_state):
    uf_ref, ub_ref, h0_ref, par_ref, bre_ref, bim_ref, cre_ref, cim_ref, yf_ref, yb_ref = refs[:10]
    pos = 10
    if emit_state:
        hfin_ref = refs[pos]
        pos += 1
    bbar_scr, cblk_scr, hst_scr, bu_scr, hb_scr = refs[pos:]

    step = pl.program_id(0)
    t_chunk = uf_ref.shape[1]
    rows = n_batch * t_chunk
    abar = []
    for d in range(2):
        a_re = par_ref[d, 0:1, :]
        a_im = par_ref[d, 1:2, :]
        dt = jnp.exp(par_ref[d, 2:3, :])
        mag = jnp.exp(a_re * dt)
        abar.append((mag * jnp.cos(a_im * dt), mag * jnp.sin(a_im * dt)))

    @pl.when(step == 0)
    def _():
        for d in range(2):
            a_re = par_ref[d, 0:1, :]
            a_im = par_ref[d, 1:2, :]
            abr, abi = abar[d]
            den = a_re * a_re + a_im * a_im
            xr = abr - 1.0
            cr = (xr * a_re + abi * a_im) / den
            ci = (abi * a_re - xr * a_im) / den
            bre = _block_diag_lanes(bre_ref[d])
            bim = _block_diag_lanes(bim_ref[d])
            bbar_scr[d, :, 0:SSM_LANES] = (cr * bre - ci * bim).astype(BF16)
            bbar_scr[d, :, SSM_LANES:2 * SSM_LANES] = (cr * bim + ci * bre).astype(BF16)
            cblk_scr[d, :, 0:SSM_LANES] = _block_diag_lanes(cre_ref[d]).astype(BF16)
            cblk_scr[d, :, SSM_LANES:2 * SSM_LANES] = (-_block_diag_lanes(cim_ref[d])).astype(BF16)
        hst_scr[...] = h0_ref[...]

    for d, u_ref in enumerate((uf_ref, ub_ref)):
        u_tm = jnp.swapaxes(u_ref[...], 0, 1).reshape(rows, SSM_WIDTH)
        bu_scr[d] = _dot(u_tm.astype(BF16), bbar_scr[d])

    lw = 128
    if n_batch % 8 == 0:
        for j in range(SSM_LANES // lw):
            sre = slice(j * lw, (j + 1) * lw)
            sim = slice(SSM_LANES + j * lw, SSM_LANES + (j + 1) * lw)
            coef = [(jnp.broadcast_to(abar[d][0][:, sre], (n_batch, lw)),
                     jnp.broadcast_to(abar[d][1][:, sre], (n_batch, lw))) for d in range(2)]

            def body(i, carry, sre=sre, sim=sim, coef=coef):
                new = []
                for d in range(2):
                    hr, hi = carry[2 * d], carry[2 * d + 1]
                    ar, ai = coef[d]
                    t = i if d == 0 else t_chunk - 1 - i
                    r0 = pl.multiple_of(t * n_batch, n_batch)
                    nr = ar * hr - ai * hi + bu_scr[d, pl.ds(r0, n_batch), sre]
                    ni = ar * hi + ai * hr + bu_scr[d, pl.ds(r0, n_batch), sim]
                    hb_scr[d, pl.ds(r0, n_batch), sre] = nr.astype(BF16)
                    hb_scr[d, pl.ds(r0, n_batch), sim] = ni.astype(BF16)
                    new += [nr, ni]
                return tuple(new)

            init = (hst_scr[0, :, sre], hst_scr[0, :, sim], hst_scr[1, :, sre], hst_scr[1, :, sim])
            fin = lax.fori_loop(0, t_chunk, body, init, unroll=2)
            hst_scr[0, :, sre] = fin[0]
            hst_scr[0, :, sim] = fin[1]
            hst_scr[1, :, sre] = fin[2]
            hst_scr[1, :, sim] = fin[3]
    else:
        assert n_batch == 4
        n_blk = rows // 8
        low = lax.broadcasted_iota(jnp.int32, (8, lw), 0) < 4
        firsts = (low, jnp.logical_not(low))
        for j in range(SSM_LANES // lw):
            sre = slice(j * lw, (j + 1) * lw)
            sim = slice(SSM_LANES + j * lw, SSM_LANES + (j + 1) * lw)
            coef = []
            for d in range(2):
                ar = jnp.broadcast_to(abar[d][0][:, sre], (8, lw))
                ai = jnp.broadcast_to(abar[d][1][:, sre], (8, lw))
                coef.append((ar, ai, jnp.where(firsts[d], ar, ar * ar - ai * ai),
                             jnp.where(firsts[d], ai, 2.0 * ar * ai)))

            def body(i, carry, sre=sre, sim=sim, coef=coef):
                new = []
                for d in range(2):
                    pr, pi = carry[2 * d], carry[2 * d + 1]
                    ar, ai, cr, ci = coef[d]
                    first = firsts[d]
                    r16 = pl.multiple_of((2 * i if d == 0 else n_blk - 2 - 2 * i) * 8, 16)
                    tile_r, tile_i = [None, None], [None, None]
                    for half in ((0, 1) if d == 0 else (1, 0)):
                        vr = bu_scr[d, pl.ds(r16 + 8 * half, 8), sre]
                        vi = bu_scr[d, pl.ds(r16 + 8 * half, 8), sim]
                        sr = jnp.where(first, 0.0, pltpu.roll(vr, 4, axis=0))
                        si = jnp.where(first, 0.0, pltpu.roll(vi, 4, axis=0))
                        nr = cr * pr - ci * pi + (vr + (ar * sr - ai * si))
                        ni = cr * pi + ci * pr + (vi + (ar * si + ai * sr))
                        tile_r[half], tile_i[half] = nr, ni
                        pr = jnp.where(first, pltpu.roll(nr, 4, axis=0), nr)
                        pi = jnp.where(first, pltpu.roll(ni, 4, axis=0), ni)
                    hb_scr[d, pl.ds(r16, 16), sre] = jnp.concatenate(tile_r, axis=0).astype(BF16)
                    hb_scr[d, pl.ds(r16, 16), sim] = jnp.concatenate(tile_i, axis=0).astype(BF16)
                    new += [pr, pi]
                return tuple(new)

            init = (hst_scr[0, :, sre], hst_scr[0, :, sim], hst_scr[1, :, sre], hst_scr[1, :, sim])
            fin = lax.fori_loop(0, n_blk // 2, body, init, unroll=2)
            hst_scr[0, :, sre] = fin[0]
            hst_scr[0, :, sim] = fin[1]
            hst_scr[1, :, sre] = fin[2]
            hst_scr[1, :, sim] = fin[3]

    for d, y_ref in enumerate((yf_ref, yb_ref)):
        y = _dot_nt(hb_scr[d], cblk_scr[d])
        y_ref[...] = jnp.swapaxes(y.reshape(t_chunk, n_batch, SSM_WIDTH), 0, 1)

    if emit_state:
        @pl.when(step == pl.num_programs(0) - 1)
        def _():
            hfin_ref[...] = hst_scr[...]


def _ssm_scan(u_tok, h0, h0_spec, wts, layer, n_batch, seq, t_chunk, emit_state):
    n_steps = seq // t_chunk
    rows = n_batch * t_chunk
    h_rows = h0.shape[-2]
    fwd_blk = lambda k: (0, k, 0)
    bwd_blk = lambda k: (0, n_steps - 1 - k, 0)
    chunk = (n_batch, t_chunk, SSM_WIDTH)

    def par(shape):
        return _layer_spec((2,) + shape, layer)

    out_shape = [jax.ShapeDtypeStruct((n_batch, seq, SSM_WIDTH), F32)] * 2
    out_specs = [pl.BlockSpec(chunk, fwd_blk), pl.BlockSpec(chunk, bwd_blk)]
    if emit_state:
        out_shape.append(jax.ShapeDtypeStruct((2, h_rows, 2 * SSM_LANES), F32))
        out_specs.append(_const_spec((2, h_rows, 2 * SSM_LANES)))
    u3 = u_tok.reshape(n_batch, seq, SSM_WIDTH)
    return pl.pallas_call(
        functools.partial(_ssm_kernel, n_batch=n_batch, emit_state=emit_state),
        grid=(n_steps,),
        in_specs=[pl.BlockSpec(chunk, fwd_blk), pl.BlockSpec(chunk, bwd_blk), h0_spec, par((3, SSM_LANES)),
                  par((SSM_WIDTH, SSM_STATE)), par((SSM_WIDTH, SSM_STATE)),
                  par((SSM_WIDTH, SSM_STATE)), par((SSM_WIDTH, SSM_STATE))],
        out_specs=out_specs,
        out_shape=out_shape,
        scratch_shapes=[pltpu.VMEM((2, SSM_WIDTH, 2 * SSM_LANES), BF16),
                        pltpu.VMEM((2, SSM_WIDTH, 2 * SSM_LANES), BF16),
                        pltpu.VMEM((2, h_rows, 2 * SSM_LANES), F32),
                        pltpu.VMEM((2, rows, 2 * SSM_LANES), F32),
                        pltpu.VMEM((2, rows, 2 * SSM_LANES), BF16)],
        compiler_params=pltpu.CompilerParams(vmem_limit_bytes=VMEM_LIMIT),
        name="ssm_b%d" % n_batch,
    )(u3, u3, h0, wts["ssm_par"], wts["ssm_b_re"], wts["ssm_b_im"], wts["ssm_c_re"], wts["ssm_c_im"])


def _out_mlp_kernel(x_ref, oa_ref, yf_ref, yb_ref, u_ref, mod_ref, vec_ref, wout_ref, wglu_ref, w1_ref, w2_ref,
                    y_ref, *, final):
    g1 = mod_ref[:, 2 * D_MODEL:3 * D_MODEL]
    sh2 = mod_ref[:, 3 * D_MODEL:4 * D_MODEL]
    sc2 = mod_ref[:, 4 * D_MODEL:5 * D_MODEL]
    g2 = mod_ref[:, 5 * D_MODEL:6 * D_MODEL]
    ys = yf_ref[...] + yb_ref[...] + u_ref[...] * vec_ref[:, VEC_SSMD:VEC_SSMD + SSM_WIDTH]
    z = _dot(_gelu_tanh(ys).astype(BF16), wglu_ref[...])
    o_ssm = (z[:, 0:SSM_WIDTH] * _sigmoid(z[:, SSM_WIDTH:2 * SSM_WIDTH])).astype(BF16)
    mix = (_dot(oa_ref[:, 0:512], wout_ref[0:512, :]) + _dot(o_ssm, wout_ref[512:768, :])
           + _dot(oa_ref[:, 512:O_COLS], wout_ref[768:1024, :]))
    x1 = x_ref[...] + g1 * mix
    ms = jnp.mean(x1 * x1, axis=-1, keepdims=True)
    h = x1 * lax.rsqrt(ms + EPS) * vec_ref[:, VEC_N2G:VEC_N2G + D_MODEL]
    h = h * (1.0 + sc2) + sh2
    hb = h.astype(BF16)
    mlp = jnp.zeros_like(x1)
    for c0 in range(0, D_FF, FF_CHUNK):
        a = jnp.maximum(_dot(hb, w1_ref[:, c0:c0 + FF_CHUNK]), 0.0)
        mlp = mlp + _dot((a * a).astype(BF16), w2_ref[c0:c0 + FF_CHUNK, :])
    x2 = x1 + g2 * mlp
    if final:
        ms2 = jnp.mean(x2 * x2, axis=-1, keepdims=True)
        x2 = x2 * lax.rsqrt(ms2 + EPS) * vec_ref[:, VEC_FNG:VEC_FNG + D_MODEL]
    y_ref[...] = x2


def _out_mlp(x2d, o_attn, y_fwd, y_bwd, u, mod3, layer, wts, mlp_weights, n_batch, seq, latent, final):
    tm = MLP_TILE
    n_tok = n_batch * seq
    w_out, w1, w2 = mlp_weights
    resident = lambda shape: pl.BlockSpec(shape, lambda i: (0, 0), pipeline_mode=pl.Buffered(1))
    if latent:
        tpb = seq // tm
        mod_row = lambda i: (layer * MOD_ROWS + 1 + i // tpb, 0, 0)
    else:
        mod_row = lambda i: (layer * MOD_ROWS, 0, 0)
    tile = lambda i: (i, 0)
    return pl.pallas_call(
        functools.partial(_out_mlp_kernel, final=final),
        grid=(n_tok // tm,),
        in_specs=[
            pl.BlockSpec((tm, D_MODEL), tile),
            pl.BlockSpec((tm, O_COLS), tile),
            pl.BlockSpec((tm, SSM_WIDTH), tile),
            pl.BlockSpec((tm, SSM_WIDTH), tile),
            pl.BlockSpec((tm, SSM_WIDTH), tile),
            pl.BlockSpec((None, 1, N_MOD * D_MODEL), mod_row),
            _layer_spec((1, VEC_END), layer),
            resident((D_MODEL, D_MODEL)),
            _layer_spec((SSM_WIDTH, 2 * SSM_WIDTH), layer, single_buffer=True),
            resident((D_MODEL, D_FF)),
            resident((D_FF, D_MODEL)),
        ],
        out_specs=pl.BlockSpec((tm, D_MODEL), tile),
        out_shape=jax.ShapeDtypeStruct((n_tok, D_MODEL), F32),
        compiler_params=pltpu.CompilerParams(vmem_limit_bytes=VMEM_LIMIT),
        name="out_mlp_%s%s" % ("lat" if latent else "ctx", "_final" if final else ""),
    )(x2d, o_attn, y_fwd, y_bwd, u, mod3, wts["vecs"], w_out, wts["w_glu"], w1, w2)


def _rope_tables(t_len, chunk, n_chunks):
    n = chunk // 4
    rows = t_len // GRID_W
    row = np.repeat(np.arange(rows), GRID_W).astype(np.float32)
    col = np.tile(np.arange(GRID_W), rows).astype(np.float32)
    freq = (np.float32(ROPE_THETA) ** (-np.arange(n, dtype=np.float32) / np.float32(n))).astype(np.float32)
    ang_r = (row[:, None] * freq).astype(np.float32)
    ang_c = (col[:, None] * freq).astype(np.float32)
    cr, sr, cc, sc = np.cos(ang_r), np.sin(ang_r), np.cos(ang_c), np.sin(ang_c)
    z = np.zeros_like(cr)
    tabs = np.stack([np.concatenate([cr, cr, cc, cc], axis=-1),
                     np.concatenate([-sr, z, -sc, z], axis=-1),
                     np.concatenate([z, sr, z, sc], axis=-1)]).astype(np.float32)
    return np.tile(tabs, (1, 1, n_chunks))


def _mla_rope_tables(t_len):
    tabs = _rope_tables(t_len, MLA_ROPE, 1)
    ident = np.stack([np.ones((t_len, MLA_NOPE), np.float32), np.zeros((t_len, MLA_NOPE), np.float32),
                      np.zeros((t_len, MLA_NOPE), np.float32)])
    tail = ident[:, :, :MLA_BLK - MLA_QK]
    return np.concatenate([ident, tabs, tail], axis=-1)


def _pack_vectors(p, n_layers):
    def rows(a):
        return a.reshape(n_layers, 1, -1)

    fng = jnp.broadcast_to(p["final_norm_g"].reshape(1, 1, D_MODEL), (n_layers, 1, D_MODEL))
    return jnp.concatenate([
        rows(p["norm1_g"]), rows(p["norm2_g"]), fng,
        jnp.tile(rows(p["gqa_qn_g"]), (1, 1, GQA_HEADS)),
        jnp.tile(rows(p["gqa_kn_g"]), (1, 1, GQA_KV_HEADS)),
        rows(p["mla_qn_g"]), jnp.zeros((n_layers, 1, CQ_PAD - MLA_Q_RANK), F32),
        rows(p["mla_kvn_g"]),
        jnp.tile(rows(p["diff_subln_g"]), (1, 1, DIFF_HEADS)),
        rows(p["ssm_d"]),
        rows(p["diff_lq1"]), rows(p["diff_lk1"]), rows(p["diff_lq2"]), rows(p["diff_lk2"]),
    ], axis=-1)


def kernel(x_prompt, x_sample, cache_diff_k, cache_diff_v, cache_gqa_k, cache_gqa_v, cache_mla_ckv, cache_mla_krope, state_ssm_re, state_ssm_im, c, c_ctx, norm1_g, norm2_g, w_ada, b_ada, w_in, w_out, diff_lq1, diff_lk1, diff_lq2, diff_lk2, diff_subln_g, gqa_qn_g, gqa_kn_g, ssm_a_re, ssm_a_im, ssm_log_dt, ssm_b_re, ssm_b_im, ssm_c_re, ssm_c_im, ssm_d, ssm_w_glu, mla_qn_g, mla_kvn_g, mla_w_uq, mla_w_ukv, mlp_w1, mlp_w2, final_norm_g):
    n_layers = w_in.shape[0]
    n_ctx, seq, _ = x_prompt.shape
    n_lat, t_lat, _ = x_sample.shape
    past = cache_diff_k.shape[2]
    assert n_lat + 1 <= MOD_ROWS and n_lat == 4 and n_ctx % 8 == 0
    assert seq == TOKEN_TILE and t_lat % TOKEN_TILE == 0
    assert (n_ctx * seq) % MLP_TILE == 0 and t_lat % MLP_TILE == 0
    assert PROJ_TILE % seq == 0 and n_ctx % (PROJ_TILE // seq) == 0 and t_lat % PROJ_TILE == 0

    p = dict(norm1_g=norm1_g, norm2_g=norm2_g, final_norm_g=final_norm_g, gqa_qn_g=gqa_qn_g, gqa_kn_g=gqa_kn_g,
             mla_qn_g=mla_qn_g, mla_kvn_g=mla_kvn_g, diff_subln_g=diff_subln_g, ssm_d=ssm_d, diff_lq1=diff_lq1,
             diff_lk1=diff_lk1, diff_lq2=diff_lq2, diff_lk2=diff_lk2)
    ssm_rows = SSM_GROUPS * SSM_GROUP
    wts = {
        "vecs": _pack_vectors(p, n_layers),
        "w_in_t": jnp.swapaxes(w_in, 1, 2),
        "w_uq": mla_w_uq, "w_ukv": mla_w_ukv,
        "w_glu": ssm_w_glu.astype(BF16),
        "ssm_par": jnp.stack([ssm_a_re.reshape(n_layers, 2, SSM_LANES), ssm_a_im.reshape(n_layers, 2, SSM_LANES),
                              jnp.repeat(ssm_log_dt, SSM_STATE, axis=-1)], axis=2),
        "ssm_b_re": jnp.swapaxes(ssm_b_re, 3, 4).reshape(n_layers, 2, ssm_rows, SSM_STATE),
        "ssm_b_im": jnp.swapaxes(ssm_b_im, 3, 4).reshape(n_layers, 2, ssm_rows, SSM_STATE),
        "ssm_c_re": ssm_c_re.reshape(n_layers, 2, ssm_rows, SSM_STATE),
        "ssm_c_im": ssm_c_im.reshape(n_layers, 2, ssm_rows, SSM_STATE),
    }

    cond = jnp.concatenate([c_ctx[None], c, jnp.zeros((MOD_ROWS - 1 - n_lat, D_MODEL), F32)], axis=0)
    mod3 = _adaln(cond, w_ada, b_ada).reshape(n_layers * MOD_ROWS, 1, N_MOD * D_MODEL)

    rope_tabs = (jnp.asarray(_rope_tables(t_lat, DIFF_QK, 256 // DIFF_QK)),
                 jnp.asarray(_rope_tables(t_lat, GQA_HEAD_DIM, 256 // GQA_HEAD_DIM)),
                 jnp.asarray(_mla_rope_tables(t_lat)))
    caches = (jnp.transpose(cache_diff_k, (0, 1, 3, 4, 2)).reshape(n_lat, n_layers, 256, past),
              jnp.transpose(cache_diff_v, (0, 1, 3, 4, 2)).reshape(n_lat, n_layers, 256, past),
              jnp.transpose(cache_gqa_k, (0, 1, 3, 4, 2)).reshape(n_lat, n_layers, 128, past),
              jnp.transpose(cache_gqa_v, (0, 1, 3, 4, 2)).reshape(n_lat, n_layers, 128, past),
              cache_mla_ckv,
              jnp.swapaxes(cache_mla_krope, 2, 3))

    lam_inits = [0.8 - 0.6 * math.exp(-0.3 * l) for l in range(n_layers)]

    x = x_prompt.reshape(n_ctx * seq, D_MODEL)
    zero_state = jnp.zeros((2, n_ctx, 2 * SSM_LANES), F32)
    zero_spec = _const_spec((2, n_ctx, 2 * SSM_LANES))
    kept = None
    states = []
    mlp_weights = []
    for l in range(n_layers):
        res = _in_proj(x, mod3, l, wts, n_ctx, seq, latent=False, prev_caches=kept)
        q, k, v, u = res[:4]
        kept = res[4:]
        o_attn, *mlp_w = _attention(q, k, v, wts["vecs"], l, lam_inits[l], n_ctx, seq,
                                    to_bf16=(w_out, mlp_w1, mlp_w2))
        mlp_weights.append(mlp_w)
        y_fwd, y_bwd, h_fin = _ssm_scan(u, zero_state, zero_spec, wts, l, n_ctx, seq, CTX_SSM_CHUNK,
                                        emit_state=True)
        x = _out_mlp(x, o_attn, y_fwd.reshape(n_ctx * seq, SSM_WIDTH), y_bwd.reshape(n_ctx * seq, SSM_WIDTH), u,
                     mod3, l, wts, mlp_w, n_ctx, seq, latent=False, final=(l == n_layers - 1))
        states.append(h_fin)
    y_prompt = x.reshape(n_ctx, seq, D_MODEL)

    x = x_sample.reshape(n_lat * t_lat, D_MODEL)
    h0 = jnp.concatenate([state_ssm_re.reshape(n_lat, n_layers, 2, SSM_LANES),
                          state_ssm_im.reshape(n_lat, n_layers, 2, SSM_LANES)], axis=-1)
    h0 = jnp.transpose(h0, (1, 2, 0, 3))
    h0 = jnp.concatenate([h0, h0], axis=2)
    for l in range(n_layers):
        h0_spec = _layer_spec((2, 2 * n_lat, 2 * SSM_LANES), l)
        q, k, v, u = _in_proj(x, mod3, l, wts, n_lat, t_lat, latent=True, rope_tabs=rope_tabs)
        kc, vc = _cache_prep(caches, l, wts["w_ukv"])
        o_attn, = _attention(q, k, v, wts["vecs"], l, lam_inits[l], n_lat, t_lat, k_cache=kc, v_cache=vc)
        y_fwd, y_bwd = _ssm_scan(u, h0, h0_spec, wts, l, n_lat, t_lat, LAT_SSM_CHUNK, emit_state=False)
        x = _out_mlp(x, o_attn, y_fwd.reshape(n_lat * t_lat, SSM_WIDTH), y_bwd.reshape(n_lat * t_lat, SSM_WIDTH), u,
                     mod3, l, wts, mlp_weights[l], n_lat, t_lat, latent=True, final=(l == n_layers - 1))
    y_sample = x.reshape(n_lat, t_lat, D_MODEL)

    cdk, cdv, cgk, cgv, cckv, ckr = kept

    def token_major(a, heads, dim):
        return jnp.transpose(a.reshape(n_ctx, n_layers, heads, dim, seq), (0, 1, 4, 2, 3))

    st = jnp.transpose(jnp.stack(states), (2, 0, 1, 3))
    return (y_prompt, y_sample,
            token_major(cdk, DIFF_HEADS, 2 * DIFF_QK), token_major(cdv, DIFF_HEADS, DIFF_V),
            token_major(cgk, GQA_KV_HEADS, GQA_HEAD_DIM), token_major(cgv, GQA_KV_HEADS, GQA_HEAD_DIM),
            cckv, jnp.swapaxes(ckr, 2, 3),
            st[..., :SSM_LANES].reshape(n_ctx, n_layers, 2, SSM_GROUPS, SSM_STATE),
            st[..., SSM_LANES:].reshape(n_ctx, n_layers, 2, SSM_GROUPS, SSM_STATE))
```

```python
import functools
import math

import numpy as np
import jax
import jax.numpy as jnp
from jax import lax
from jax.experimental import pallas as pl
from jax.experimental.pallas import tpu as pltpu

F32 = jnp.float32
BF16 = jnp.bfloat16

D_MODEL = 1024
GRID_W = 64
ROPE_THETA = 10000.0
EPS = 1e-6
DIFF_HEADS, DIFF_QK, DIFF_V = 4, 32, 64
GQA_HEADS, GQA_KV_HEADS, GQA_HEAD_DIM = 4, 2, 64
SSM_WIDTH, SSM_GROUP, SSM_STATE = 256, 16, 64
SSM_GROUPS = SSM_WIDTH // SSM_GROUP
SSM_LANES = SSM_GROUPS * SSM_STATE
MLA_HEADS, MLA_Q_RANK, MLA_KV_RANK, MLA_NOPE, MLA_ROPE, MLA_V = 4, 192, 128, 64, 32, 64
MLA_QK = MLA_NOPE + MLA_ROPE
D_FF = 4 * D_MODEL
N_MOD = 6
LOG2E = math.log2(math.e)
IN_COLS = 1888
MOD_ROWS = 8

P_DQ, P_DK, P_DV, P_GQ, P_GK, P_GV, P_U, P_CQ, P_CKV, P_KR, P_END = (
    0, 256, 512, 768, 1024, 1152, 1280, 1536, 1792, 1920, 2048)
S_CQ_END, S_CKV, S_KR = 1728, 1728, 1856
CQ_PAD = P_CKV - P_CQ
MLA_BLK = 128
Q_DIFF, Q_GQA, Q_MLA, Q_COLS = 0, 256, 512, 1024
K_DIFF, K_GQA, K_MLA, K_COLS = 0, 256, 512, 1024
V_DIFF, V_GQA, V_MLA, V_COLS = 0, 256, 512, 1024
O_COLS = 768

VEC_N1G, VEC_N2G, VEC_FNG, VEC_QNG, VEC_KNG, VEC_MQNG, VEC_KVNG, VEC_SUBG, VEC_SSMD, VEC_LAM, VEC_END = (
    0, 1024, 2048, 3072, 3328, 3456, 3712, 3840, 4096, 4352, 4480)

TOKEN_TILE = 256
LAT_QUERY_TILE = 512
PROJ_TILE = 512
MLP_TILE = 512
FF_CHUNK = 1024
CTX_SSM_CHUNK = 32
LAT_SSM_CHUNK = 256
STACK_MAX_KEYS = 512
VMEM_LIMIT = 48 * 1024 * 1024


def _dot(a, b):
    return jnp.dot(a, b, preferred_element_type=F32)


def _dot_nt(a, b):
    return lax.dot_general(a, b, (((1,), (1,)), ((), ())), preferred_element_type=F32)


def _block_ones(width, seg):
    shift = seg.bit_length() - 1
    r = jnp.right_shift(lax.broadcasted_iota(jnp.int32, (width, width), 0), shift)
    c = jnp.right_shift(lax.broadcasted_iota(jnp.int32, (width, width), 1), shift)
    return jnp.where(r == c, 1.0, 0.0).astype(BF16)


def _seg_sum_sq(y, seg):
    ones = _block_ones(y.shape[-1], seg)
    sq = y * y
    hi = sq.astype(BF16)
    lo = (sq - hi.astype(F32)).astype(BF16)
    return _dot(hi, ones) + _dot(lo, ones)


def _rope(x, tab_ref, quarter):
    outs = []
    for j in range(x.shape[-1] // 128):
        sl = slice(j * 128, (j + 1) * 128)
        xs = x[:, sl]
        up = pltpu.roll(xs, 128 - quarter, axis=1)
        dn = pltpu.roll(xs, quarter, axis=1)
        outs.append(xs * tab_ref[0, :, sl] + up * tab_ref[1, :, sl] + dn * tab_ref[2, :, sl])
    return outs[0] if len(outs) == 1 else jnp.concatenate(outs, axis=-1)


def _sigmoid(x):
    return 1.0 / (1.0 + jnp.exp(-x))


def _lane_lt(width, bound):
    return lax.broadcasted_iota(jnp.int32, (1, width), 1) < bound


def _dup_halves(x):
    swapped = pltpu.roll(x, 64, axis=1)
    left = _lane_lt(128, 64)
    return jnp.where(left, x, swapped), jnp.where(left, swapped, x)


def _const_spec(shape):
    nd = len(shape)
    return pl.BlockSpec(shape, lambda *_: (0,) * nd)


def _layer_spec(shape, layer, single_buffer=False):
    nd = len(shape)
    kw = {"pipeline_mode": pl.Buffered(1)} if single_buffer else {}
    return pl.BlockSpec((None,) + tuple(shape), lambda *_: (layer,) + (0,) * nd, **kw)


def _adaln_kernel(cond_ref, w_ref, b_ref, o_ref):
    c = cond_ref[...]
    s = c * _sigmoid(c)
    o_ref[...] = _dot(s.astype(BF16), w_ref[...].astype(BF16)) + b_ref[...]


def _adaln(cond, w_ada, b_ada):
    n_layers = w_ada.shape[0]
    tn = 1024
    return pl.pallas_call(
        _adaln_kernel,
        grid=(n_layers, N_MOD * D_MODEL // tn),
        in_specs=[
            pl.BlockSpec((MOD_ROWS, D_MODEL), lambda l, j: (0, 0)),
            pl.BlockSpec((None, D_MODEL, tn), lambda l, j: (l, 0, j)),
            pl.BlockSpec((None, 1, tn), lambda l, j: (l, 0, j)),
        ],
        out_specs=pl.BlockSpec((None, MOD_ROWS, tn), lambda l, j: (l, 0, j)),
        out_shape=jax.ShapeDtypeStruct((n_layers, MOD_ROWS, N_MOD * D_MODEL), F32),
        compiler_params=pltpu.CompilerParams(vmem_limit_bytes=VMEM_LIMIT),
        name="adaln",
    )(cond, w_ada, b_ada.reshape(n_layers, 1, N_MOD * D_MODEL))


N_CACHE = 6


def _in_proj_kernel(*refs, latent, n_alias, layer):
    x_ref, mod_ref, vec_ref, wint_ref, wuq_ref, wukv_ref = refs[:6]
    pos = 6
    if latent:
        rope_d_ref, rope_g_ref, rope_m_ref = refs[pos:pos + 3]
        pos += 3
    pos += n_alias
    q_ref, k_ref, v_ref, u_ref = refs[pos:pos + 4]
    pos += 4
    if not latent:
        cdk_ref, cdv_ref, cgk_ref, cgv_ref, cckv_ref, ckr_ref = refs[pos:pos + N_CACHE]
        pos += N_CACHE
    win_scr, wuq_scr, wukv_scr = refs[pos:]

    @pl.when(pl.program_id(0) == 0)
    def _():
        aligned = (S_CQ_END // 128) * 128
        for r0 in range(0, aligned, 256):
            r1 = min(r0 + 256, aligned)
            win_scr[:, r0:r1] = wint_ref[r0:r1, :].T.astype(BF16)
        tail = wint_ref[aligned:aligned + 128, :].T
        win_scr[:, aligned:aligned + 128] = jnp.where(_lane_lt(128, S_CQ_END - aligned), tail, 0.0).astype(BF16)
        win_scr[:, P_CKV:P_KR] = wint_ref[S_CKV:S_KR, :].T.astype(BF16)
        kr_rows = jnp.concatenate([jnp.zeros((MLA_NOPE, D_MODEL), F32), wint_ref[S_KR:IN_COLS, :],
                                   jnp.zeros((MLA_BLK - MLA_QK, D_MODEL), F32)], axis=0)
        win_scr[:, P_KR:P_END] = kr_rows.T.astype(BF16)
        wuq_scr[...] = jnp.zeros(wuq_scr.shape, BF16)
        wuq = wuq_ref[...]
        for hd in range(MLA_HEADS):
            wuq_scr[0:MLA_Q_RANK, hd * MLA_BLK:hd * MLA_BLK + MLA_QK] = (
                wuq[:, hd * MLA_QK:(hd + 1) * MLA_QK].astype(BF16))
        wukv_scr[...] = wukv_ref[...].astype(BF16)

    x = x_ref[...]
    sh1 = mod_ref[:, 0:D_MODEL]
    sc1 = mod_ref[:, D_MODEL:2 * D_MODEL]
    ms = jnp.mean(x * x, axis=-1, keepdims=True)
    h = x * lax.rsqrt(ms + EPS) * vec_ref[:, VEC_N1G:VEC_N1G + D_MODEL]
    h = h * (1.0 + sc1) + sh1
    proj = _dot(h.astype(BF16), win_scr[...])

    dq = proj[:, P_DQ:P_DK]
    dk = proj[:, P_DK:P_DV]
    dv = proj[:, P_DV:P_GQ]
    gq = proj[:, P_GQ:P_GK]
    gk = proj[:, P_GK:P_GV]
    gv = proj[:, P_GV:P_U]
    u_ref[...] = proj[:, P_U:P_CQ]
    cq = proj[:, P_CQ:P_CKV]
    ckv = proj[:, P_CKV:P_KR]
    krb = proj[:, P_KR:P_END]

    qng = vec_ref[:, VEC_QNG:VEC_QNG + 256]
    kng = vec_ref[:, VEC_KNG:VEC_KNG + 128]
    gq = gq * lax.rsqrt(_seg_sum_sq(gq, GQA_HEAD_DIM) * (1.0 / GQA_HEAD_DIM) + EPS) * qng
    gk = gk * lax.rsqrt(_seg_sum_sq(gk, GQA_HEAD_DIM) * (1.0 / GQA_HEAD_DIM) + EPS) * kng
    ckv_n = (ckv * lax.rsqrt(jnp.mean(ckv * ckv, axis=-1, keepdims=True) + EPS)
             * vec_ref[:, VEC_KVNG:VEC_KVNG + MLA_KV_RANK])
    cq_ms = jnp.sum(cq * cq, axis=-1, keepdims=True) * (1.0 / MLA_Q_RANK)
    cq_n = cq * lax.rsqrt(cq_ms + EPS) * vec_ref[:, VEC_MQNG:VEC_MQNG + CQ_PAD]
    mq = _dot(cq_n.astype(BF16), wuq_scr[...])
    kv = _dot(ckv_n.astype(BF16), wukv_scr[...])

    if not latent:
        seq = cckv_ref.shape[2]

        def keep(ref, val, feature_major=True, rows=None):
            for s in range(ref.shape[0]):
                piece = val[s * seq:(s + 1) * seq]
                if feature_major:
                    piece = piece.T
                if rows is not None:
                    piece = piece[rows]
                for l in range(ref.shape[1]):
                    ref[s, l] = piece if ref.shape[1] == 1 or l == layer else jnp.zeros_like(piece)

        keep(cdk_ref, dk)
        keep(cdv_ref, dv)
        keep(cgk_ref, gk)
        keep(cgv_ref, gv)
        keep(cckv_ref, ckv_n, feature_major=False)
        keep(ckr_ref, krb, rows=slice(MLA_NOPE, MLA_QK))
    else:
        dq = _rope(dq, rope_d_ref, DIFF_QK // 4)
        dk = _rope(dk, rope_d_ref, DIFF_QK // 4)
        gq = _rope(gq, rope_g_ref, GQA_HEAD_DIM // 4)
        gk = _rope(gk, rope_g_ref, GQA_HEAD_DIM // 4)
        mq = jnp.concatenate(
            [_rope(mq[:, hd * MLA_BLK:(hd + 1) * MLA_BLK], rope_m_ref, MLA_ROPE // 4)
             for hd in range(MLA_HEADS)], axis=-1)
        krb = _rope(krb, rope_m_ref, MLA_ROPE // 4)

    q_ref[:, Q_DIFF:Q_GQA] = (dq * (LOG2E * DIFF_QK ** -0.5)).astype(BF16)
    q_ref[:, Q_GQA:Q_MLA] = (gq * (LOG2E * GQA_HEAD_DIM ** -0.5)).astype(BF16)
    q_ref[:, Q_MLA:Q_COLS] = (mq * (LOG2E * MLA_QK ** -0.5)).astype(BF16)
    k_ref[:, K_DIFF:K_GQA] = dk.astype(BF16)
    gk0, gk1 = _dup_halves(gk)
    k_ref[:, K_GQA:K_GQA + 128] = gk0.astype(BF16)
    k_ref[:, K_GQA + 128:K_MLA] = gk1.astype(BF16)
    nope = _lane_lt(MLA_BLK, MLA_NOPE)
    for hd in range(MLA_HEADS):
        blk = slice(hd * MLA_BLK, (hd + 1) * MLA_BLK)
        k_ref[:, K_MLA + hd * MLA_BLK:K_MLA + (hd + 1) * MLA_BLK] = jnp.where(nope, kv[:, blk], krb).astype(BF16)
    v_ref[:, V_DIFF:V_GQA] = dv.astype(BF16)
    gv0, gv1 = _dup_halves(gv)
    v_ref[:, V_GQA:V_GQA + 128] = gv0.astype(BF16)
    v_ref[:, V_GQA + 128:V_MLA] = gv1.astype(BF16)
    v_ref[:, V_MLA:V_COLS] = kv.astype(BF16)


def _in_proj(x2d, mod3, layer, wts, n_batch, seq, latent, rope_tabs=None, prev_caches=None):
    tm = PROJ_TILE
    n_tok = n_batch * seq
    n_layers = wts["vecs"].shape[0]
    if latent:
        tpb = seq // tm
        mod_row = lambda i: (layer * MOD_ROWS + 1 + i // tpb, 0, 0)
    else:
        mod_row = lambda i: (layer * MOD_ROWS, 0, 0)
    tile = lambda i: (i, 0)
    in_specs = [
        pl.BlockSpec((tm, D_MODEL), tile),
        pl.BlockSpec((None, 1, N_MOD * D_MODEL), mod_row),
        _layer_spec((1, VEC_END), layer),
        _layer_spec((IN_COLS, D_MODEL), layer, single_buffer=True),
        _layer_spec((MLA_Q_RANK, MLA_HEADS * MLA_QK), layer, single_buffer=True),
        _layer_spec((MLA_KV_RANK, MLA_HEADS * MLA_BLK), layer, single_buffer=True),
    ]
    args = [x2d, mod3, wts["vecs"], wts["w_in_t"], wts["w_uq"], wts["w_ukv"]]
    out_shape = [
        jax.ShapeDtypeStruct((n_tok, Q_COLS), BF16),
        jax.ShapeDtypeStruct((n_tok, K_COLS), BF16),
        jax.ShapeDtypeStruct((n_tok, V_COLS), BF16),
        jax.ShapeDtypeStruct((n_tok, SSM_WIDTH), F32),
    ]
    out_specs = [
        pl.BlockSpec((tm, Q_COLS), tile),
        pl.BlockSpec((tm, K_COLS), tile),
        pl.BlockSpec((tm, V_COLS), tile),
        pl.BlockSpec((tm, SSM_WIDTH), tile),
    ]
    aliases = {}
    n_alias = 0
    if latent:
        pos = lambda i: (0, i % tpb, 0)
        in_specs += [pl.BlockSpec((3, tm, 256), pos), pl.BlockSpec((3, tm, 256), pos),
                     pl.BlockSpec((3, tm, 128), pos)]
        args += list(rope_tabs)
    else:
        spt = tm // seq
        cache_dims = [(256, seq), (256, seq), (128, seq), (128, seq), (seq, MLA_KV_RANK), (MLA_ROPE, seq)]
        for dims in cache_dims:
            out_shape.append(jax.ShapeDtypeStruct((n_batch, n_layers) + dims, F32))
            if prev_caches is None:
                out_specs.append(pl.BlockSpec((spt, n_layers) + dims, lambda i: (i, 0, 0, 0)))
            else:
                out_specs.append(pl.BlockSpec((spt, 1) + dims, lambda i: (i, layer, 0, 0)))
        if prev_caches is not None:
            n_alias = N_CACHE
            for j, arr in enumerate(prev_caches):
                aliases[len(args)] = 4 + j
                in_specs.append(pl.BlockSpec(memory_space=pl.ANY))
                args.append(arr)
    return pl.pallas_call(
        functools.partial(_in_proj_kernel, latent=latent, n_alias=n_alias, layer=layer),
        grid=(n_tok // tm,),
        in_specs=in_specs,
        out_specs=out_specs,
        out_shape=out_shape,
        input_output_aliases=aliases,
        scratch_shapes=[pltpu.VMEM((D_MODEL, P_END), BF16),
                        pltpu.VMEM((CQ_PAD, MLA_HEADS * MLA_BLK), BF16),
                        pltpu.VMEM((MLA_KV_RANK, MLA_HEADS * MLA_BLK), BF16)],
        compiler_params=pltpu.CompilerParams(vmem_limit_bytes=VMEM_LIMIT),
        name="in_proj_lat" if latent else "in_proj_ctx",
    )(*args)


def _cache_prep_kernel(cdk_ref, cdv_ref, cgk_ref, cgv_ref, cckv_ref, ckr_ref, wukv_ref, k_ref, v_ref):
    past = k_ref.shape[0]
    kv = _dot(cckv_ref[...].astype(BF16), wukv_ref[...].astype(BF16))
    krb = jnp.concatenate([jnp.zeros((MLA_NOPE, past), F32), ckr_ref[...],
                           jnp.zeros((MLA_BLK - MLA_QK, past), F32)], axis=0).T
    k_ref[:, K_DIFF:K_GQA] = cdk_ref[...].T.astype(BF16)
    gk0, gk1 = _dup_halves(cgk_ref[...].T)
    k_ref[:, K_GQA:K_GQA + 128] = gk0.astype(BF16)
    k_ref[:, K_GQA + 128:K_MLA] = gk1.astype(BF16)
    nope = _lane_lt(MLA_BLK, MLA_NOPE)
    for hd in range(MLA_HEADS):
        blk = slice(hd * MLA_BLK, (hd + 1) * MLA_BLK)
        k_ref[:, K_MLA + hd * MLA_BLK:K_MLA + (hd + 1) * MLA_BLK] = jnp.where(nope, kv[:, blk], krb).astype(BF16)
    v_ref[:, V_DIFF:V_GQA] = cdv_ref[...].T.astype(BF16)
    gv0, gv1 = _dup_halves(cgv_ref[...].T)
    v_ref[:, V_GQA:V_GQA + 128] = gv0.astype(BF16)
    v_ref[:, V_GQA + 128:V_MLA] = gv1.astype(BF16)
    v_ref[:, V_MLA:V_COLS] = kv.astype(BF16)


def _cache_prep(caches, layer, w_ukv):
    cdk, cdv, cgk, cgv, cckv, ckr = caches
    n_batch, _, _, past = cdk.shape

    def spec(rows, cols):
        return pl.BlockSpec((None, None, rows, cols), lambda b: (b, layer, 0, 0))

    return pl.pallas_call(
        _cache_prep_kernel,
        grid=(n_batch,),
        in_specs=[spec(256, past), spec(256, past), spec(128, past), spec(128, past),
                  spec(past, MLA_KV_RANK), spec(MLA_ROPE, past),
                  _layer_spec((MLA_KV_RANK, MLA_HEADS * MLA_BLK), layer)],
        out_specs=[pl.BlockSpec((None, past, K_COLS), lambda b: (b, 0, 0)),
                   pl.BlockSpec((None, past, V_COLS), lambda b: (b, 0, 0))],
        out_shape=[jax.ShapeDtypeStruct((n_batch, past, K_COLS), BF16),
                   jax.ShapeDtypeStruct((n_batch, past, V_COLS), BF16)],
        name="cache_prep",
    )(cdk, cdv, cgk, cgv, cckv, ckr, w_ukv)


def _attend(qm, ks, vs):
    ss = [_dot_nt(qm, k) for k in ks]
    m = functools.reduce(jnp.maximum, [jnp.max(s, axis=-1, keepdims=True) for s in ss])
    es = [jnp.exp2(s - m) for s in ss]
    denom = functools.reduce(jnp.add, [jnp.sum(e, axis=-1, keepdims=True) for e in es])
    o = functools.reduce(jnp.add, [_dot(e.astype(BF16), v) for e, v in zip(es, vs)])
    return o * (1.0 / denom)


def _lane_mask(width, lo, hi):
    lane = lax.broadcasted_iota(jnp.int32, (1, width), 1)
    return jnp.where((lane >= lo) & (lane < hi), 1.0, 0.0).astype(BF16)


def _attn_kernel(*refs, lam_init, past, n_cast):
    if past:
        q_ref, kc_ref, ko_ref, vc_ref, vo_ref, vec_ref, o_ref = refs
        k_refs, v_refs = (kc_ref, ko_ref), (vc_ref, vo_ref)
    else:
        q_ref, k_ref, v_ref, vec_ref = refs[:4]
        k_refs, v_refs = (k_ref,), (v_ref,)
        o_ref = refs[4 + n_cast]
        for src, dst in zip(refs[4:4 + n_cast], refs[5 + n_cast:]):
            dst[...] = src[...].astype(BF16)
    tq = q_ref.shape[0]
    stack_rows = sum(r.shape[0] for r in k_refs) <= STACK_MAX_KEYS

    def keys(lo, hi):
        return [r[:, lo:hi] for r in k_refs]

    def values(lo, hi, mask=None):
        return [r[:, lo:hi] if mask is None else r[:, lo:hi] * mask for r in v_refs]
    lq1 = vec_ref[:, VEC_LAM:VEC_LAM + 32]
    lk1 = vec_ref[:, VEC_LAM + 32:VEC_LAM + 64]
    lq2 = vec_ref[:, VEC_LAM + 64:VEC_LAM + 96]
    lk2 = vec_ref[:, VEC_LAM + 96:VEC_LAM + 128]
    lam = (jnp.exp(jnp.sum(lq1 * lk1, axis=-1, keepdims=True))
           - jnp.exp(jnp.sum(lq2 * lk2, axis=-1, keepdims=True)) + lam_init)

    def masked_heads(qv, ks, vs, masks):
        if stack_rows:
            o = _attend(jnp.concatenate([qv * m for m in masks], axis=0), ks, vs)
            return [o[j * tq:(j + 1) * tq] for j in range(len(masks))]
        return [_attend(qv * m, ks, vs) for m in masks]

    parts = masked_heads(q_ref[:, Q_DIFF:Q_GQA], keys(K_DIFF, K_GQA), values(V_DIFF, V_GQA),
                         [_lane_mask(256, j * DIFF_QK, (j + 1) * DIFF_QK) for j in range(2 * DIFF_HEADS)])
    lane = lax.broadcasted_iota(jnp.int32, (1, 256), 1)
    acc = jnp.zeros((tq, 256), F32)
    for hd in range(DIFF_HEADS):
        head = (lane >= hd * DIFF_V) & (lane < (hd + 1) * DIFF_V)
        acc = jnp.where(head, parts[2 * hd] - lam * parts[2 * hd + 1], acc)
    ms = _seg_sum_sq(acc, DIFF_V) * (1.0 / DIFF_V)
    subg = vec_ref[:, VEC_SUBG:VEC_SUBG + 256]
    o_ref[:, 0:256] = (acc * lax.rsqrt(ms + EPS) * subg * (1.0 - lam_init)).astype(o_ref.dtype)

    left = _lane_lt(128, GQA_HEAD_DIM)
    for blk in range(GQA_KV_HEADS):
        o_l, o_r = masked_heads(q_ref[:, Q_GQA + blk * 128:Q_GQA + (blk + 1) * 128],
                                keys(K_GQA + blk * 128, K_GQA + (blk + 1) * 128),
                                values(V_GQA + blk * 128, V_GQA + (blk + 1) * 128),
                                [_lane_mask(128, 0, GQA_HEAD_DIM), _lane_mask(128, GQA_HEAD_DIM, 2 * GQA_HEAD_DIM)])
        o_ref[:, 256 + blk * 128:256 + (blk + 1) * 128] = jnp.where(left, o_l, o_r).astype(o_ref.dtype)

    upper = _lane_mask(MLA_BLK, MLA_NOPE, MLA_BLK)
    for pair in range(MLA_HEADS // 2):
        outs = []
        for hd in (2 * pair, 2 * pair + 1):
            outs.append(_attend(q_ref[:, Q_MLA + hd * MLA_BLK:Q_MLA + (hd + 1) * MLA_BLK],
                                keys(K_MLA + hd * MLA_BLK, K_MLA + (hd + 1) * MLA_BLK),
                                values(V_MLA + hd * MLA_BLK, V_MLA + (hd + 1) * MLA_BLK, upper)))
        both = pltpu.roll(outs[0], MLA_V, axis=1) + outs[1]
        o_ref[:, 512 + pair * 128:512 + (pair + 1) * 128] = both.astype(o_ref.dtype)


def _attention(q, k_own, v_own, vecs, layer, lam_init, n_batch, tq_total, k_cache=None, v_cache=None, to_bf16=()):
    tq = min(tq_total, LAT_QUERY_TILE)
    nq = tq_total // tq
    n_steps = n_batch * nq
    past = 0 if k_cache is None else k_cache.shape[1]
    tk = past + tq_total
    own = lambda cols: pl.BlockSpec((tq_total, cols), lambda i: (i // nq, 0))
    in_specs = [pl.BlockSpec((tq, Q_COLS), lambda i: (i, 0))]
    if past:
        assert not to_bf16
        cached = lambda cols: pl.BlockSpec((None, past, cols), lambda i: (i // nq, 0, 0))
        in_specs += [cached(K_COLS), own(K_COLS), cached(V_COLS), own(V_COLS)]
        args = [q, k_cache, k_own, v_cache, v_own]
    else:
        in_specs += [own(K_COLS), own(V_COLS)]
        args = [q, k_own, v_own]
    in_specs.append(_layer_spec((1, VEC_END), layer))
    args.append(vecs)
    out_specs = [pl.BlockSpec((tq, O_COLS), lambda i: (i, 0))]
    out_shape = [jax.ShapeDtypeStruct((n_batch * tq_total, O_COLS), BF16)]
    for w in to_bf16:
        _, rows, cols = w.shape
        slab = rows // n_steps
        assert slab * n_steps == rows and slab % 16 == 0
        in_specs.append(pl.BlockSpec((None, slab, cols), lambda i: (layer, i, 0)))
        args.append(w)
        out_specs.append(pl.BlockSpec((slab, cols), lambda i: (i, 0)))
        out_shape.append(jax.ShapeDtypeStruct((rows, cols), BF16))
    return pl.pallas_call(
        functools.partial(_attn_kernel, lam_init=lam_init, past=past, n_cast=len(to_bf16)),
        grid=(n_steps,),
        in_specs=in_specs,
        out_specs=out_specs,
        out_shape=out_shape,
        compiler_params=pltpu.CompilerParams(vmem_limit_bytes=VMEM_LIMIT),
        name="attention_tk%d" % tk,
    )(*args)


def _gelu_tanh(x):
    return 0.5 * x * (1.0 + jnp.tanh(math.sqrt(2.0 / math.pi) * (x + 0.044715 * (x * x * x))))


def _block_diag_lanes(blocks):
    pair = jnp.concatenate([blocks, blocks], axis=-1)
    tiled = jnp.concatenate([pair] * (SSM_LANES // 128), axis=-1)
    r = jnp.right_shift(lax.broadcasted_iota(jnp.int32, tiled.shape, 0), 4)
    c = jnp.right_shift(lax.broadcasted_iota(jnp.int32, tiled.shape, 1), 6)
    return jnp.where(r == c, tiled, 0.0)


def _ssm_kernel(*refs, n_batch, emit_state):
    uf_ref, ub_ref, h0_ref, par_ref, bre_ref, bim_ref, cre_ref, cim_ref, yf_ref, yb_ref = refs[:10]
    pos = 10
    if emit_state:
        hfin_ref = refs[pos]
        pos += 1
    bbar_scr, cblk_scr, hst_scr, bu_scr, hb_scr = refs[pos:]

    step = pl.program_id(0)
    t_chunk = uf_ref.shape[1]
    rows = n_batch * t_chunk
    abar = []
    for d in range(2):
        a_re = par_ref[d, 0:1, :]
        a_im = par_ref[d, 1:2, :]
        dt = jnp.exp(par_ref[d, 2:3, :])
        mag = jnp.exp(a_re * dt)
        abar.append((mag * jnp.cos(a_im * dt), mag * jnp.sin(a_im * dt)))

    @pl.when(step == 0)
    def _():
        for d in range(2):
            a_re = par_ref[d, 0:1, :]
            a_im = par_ref[d, 1:2, :]
            abr, abi = abar[d]
            den = a_re * a_re + a_im * a_im
            xr = abr - 1.0
            cr = (xr * a_re + abi * a_im) / den
            ci = (abi * a_re - xr * a_im) / den
            bre = _block_diag_lanes(bre_ref[d])
            bim = _block_diag_lanes(bim_ref[d])
            bbar_scr[d, :, 0:SSM_LANES] = (cr * bre - ci * bim).astype(BF16)
            bbar_scr[d, :, SSM_LANES:2 * SSM_LANES] = (cr * bim + ci * bre).astype(BF16)
            cblk_scr[d, :, 0:SSM_LANES] = _block_diag_lanes(cre_ref[d]).astype(BF16)
            cblk_scr[d, :, SSM_LANES:2 * SSM_LANES] = (-_block_diag_lanes(cim_ref[d])).astype(BF16)
        hst_scr[...] = h0_ref[...]

    for d, u_ref in enumerate((uf_ref, ub_ref)):
        u_tm = jnp.swapaxes(u_ref[...], 0, 1).reshape(rows, SSM_WIDTH)
        bu_scr[d] = _dot(u_tm.astype(BF16), bbar_scr[d])

    lw = 128
    if n_batch % 8 == 0:
        for j in range(SSM_LANES // lw):
            sre = slice(j * lw, (j + 1) * lw)
            sim = slice(SSM_LANES + j * lw, SSM_LANES + (j + 1) * lw)
            coef = [(jnp.broadcast_to(abar[d][0][:, sre], (n_batch, lw)),
                     jnp.broadcast_to(abar[d][1][:, sre], (n_batch, lw))) for d in range(2)]

            def body(i, carry, sre=sre, sim=sim, coef=coef):
                new = []
                for d in range(2):
                    hr, hi = carry[2 * d], carry[2 * d + 1]
                    ar, ai = coef[d]
                    t = i if d == 0 else t_chunk - 1 - i
                    r0 = pl.multiple_of(t * n_batch, n_batch)
                    nr = ar * hr - ai * hi + bu_scr[d, pl.ds(r0, n_batch), sre]
                    ni = ar * hi + ai * hr + bu_scr[d, pl.ds(r0, n_batch), sim]
                    hb_scr[d, pl.ds(r0, n_batch), sre] = nr.astype(BF16)
                    hb_scr[d, pl.ds(r0, n_batch), sim] = ni.astype(BF16)
                    new += [nr, ni]
                return tuple(new)

            init = (hst_scr[0, :, sre], hst_scr[0, :, sim], hst_scr[1, :, sre], hst_scr[1, :, sim])
            fin = lax.fori_loop(0, t_chunk, body, init, unroll=2)
            hst_scr[0, :, sre] = fin[0]
            hst_scr[0, :, sim] = fin[1]
            hst_scr[1, :, sre] = fin[2]
            hst_scr[1, :, sim] = fin[3]
    else:
        assert n_batch == 4
        n_blk = rows // 8
        low = lax.broadcasted_iota(jnp.int32, (8, lw), 0) < 4
        firsts = (low, jnp.logical_not(low))
        for j in range(SSM_LANES // lw):
            sre = slice(j * lw, (j + 1) * lw)
            sim = slice(SSM_LANES + j * lw, SSM_LANES + (j + 1) * lw)
            coef = []
            for d in range(2):
                ar = jnp.broadcast_to(abar[d][0][:, sre], (8, lw))
                ai = jnp.broadcast_to(abar[d][1][:, sre], (8, lw))
                coef.append((ar, ai, jnp.where(firsts[d], ar, ar * ar - ai * ai),
                             jnp.where(firsts[d], ai, 2.0 * ar * ai)))

            def body(i, carry, sre=sre, sim=sim, coef=coef):
                new = []
                for d in range(2):
                    pr, pi = carry[2 * d], carry[2 * d + 1]
                    ar, ai, cr, ci = coef[d]
                    first = firsts[d]
                    r16 = pl.multiple_of((2 * i if d == 0 else n_blk - 2 - 2 * i) * 8, 16)
                    tile_r, tile_i = [None, None], [None, None]
                    for half in ((0, 1) if d == 0 else (1, 0)):
                        vr = bu_scr[d, pl.ds(r16 + 8 * half, 8), sre]
                        vi = bu_scr[d, pl.ds(r16 + 8 * half, 8), sim]
                        sr = jnp.where(first, 0.0, pltpu.roll(vr, 4, axis=0))
                        si = jnp.where(first, 0.0, pltpu.roll(vi, 4, axis=0))
                        nr = cr * pr - ci * pi + (vr + (ar * sr - ai * si))
                        ni = cr * pi + ci * pr + (vi + (ar * si + ai * sr))
                        tile_r[half], tile_i[half] = nr, ni
                        pr = jnp.where(first, pltpu.roll(nr, 4, axis=0), nr)
                        pi = jnp.where(first, pltpu.roll(ni, 4, axis=0), ni)
                    hb_scr[d, pl.ds(r16, 16), sre] = jnp.concatenate(tile_r, axis=0).astype(BF16)
                    hb_scr[d, pl.ds(r16, 16), sim] = jnp.concatenate(tile_i, axis=0).astype(BF16)
                    new += [pr, pi]
                return tuple(new)

            init = (hst_scr[0, :, sre], hst_scr[0, :, sim], hst_scr[1, :, sre], hst_scr[1, :, sim])
            fin = lax.fori_loop(0, n_blk // 2, body, init, unroll=2)
            hst_scr[0, :, sre] = fin[0]
            hst_scr[0, :, sim] = fin[1]
            hst_scr[1, :, sre] = fin[2]
            hst_scr[1, :, sim] = fin[3]

    for d, y_ref in enumerate((yf_ref, yb_ref)):
        y = _dot_nt(hb_scr[d], cblk_scr[d])
        y_ref[...] = jnp.swapaxes(y.reshape(t_chunk, n_batch, SSM_WIDTH), 0, 1)

    if emit_state:
        @pl.when(step == pl.num_programs(0) - 1)
        def _():
            hfin_ref[...] = hst_scr[...]


def _ssm_scan(u_tok, h0, h0_spec, wts, layer, n_batch, seq, t_chunk, emit_state):
    n_steps = seq // t_chunk
    rows = n_batch * t_chunk
    h_rows = h0.shape[-2]
    fwd_blk = lambda k: (0, k, 0)
    bwd_blk = lambda k: (0, n_steps - 1 - k, 0)
    chunk = (n_batch, t_chunk, SSM_WIDTH)

    def par(shape):
        return _layer_spec((2,) + shape, layer)

    out_shape = [jax.ShapeDtypeStruct((n_batch, seq, SSM_WIDTH), F32)] * 2
    out_specs = [pl.BlockSpec(chunk, fwd_blk), pl.BlockSpec(chunk, bwd_blk)]
    if emit_state:
        out_shape.append(jax.ShapeDtypeStruct((2, h_rows, 2 * SSM_LANES), F32))
        out_specs.append(_const_spec((2, h_rows, 2 * SSM_LANES)))
    u3 = u_tok.reshape(n_batch, seq, SSM_WIDTH)
    return pl.pallas_call(
        functools.partial(_ssm_kernel, n_batch=n_batch, emit_state=emit_state),
        grid=(n_steps,),
        in_specs=[pl.BlockSpec(chunk, fwd_blk), pl.BlockSpec(chunk, bwd_blk), h0_spec, par((3, SSM_LANES)),
                  par((SSM_WIDTH, SSM_STATE)), par((SSM_WIDTH, SSM_STATE)),
                  par((SSM_WIDTH, SSM_STATE)), par((SSM_WIDTH, SSM_STATE))],
        out_specs=out_specs,
        out_shape=out_shape,
        scratch_shapes=[pltpu.VMEM((2, SSM_WIDTH, 2 * SSM_LANES), BF16),
                        pltpu.VMEM((2, SSM_WIDTH, 2 * SSM_LANES), BF16),
                        pltpu.VMEM((2, h_rows, 2 * SSM_LANES), F32),
                        pltpu.VMEM((2, rows, 2 * SSM_LANES), F32),
                        pltpu.VMEM((2, rows, 2 * SSM_LANES), BF16)],
        compiler_params=pltpu.CompilerParams(vmem_limit_bytes=VMEM_LIMIT),
        name="ssm_b%d" % n_batch,
    )(u3, u3, h0, wts["ssm_par"], wts["ssm_b_re"], wts["ssm_b_im"], wts["ssm_c_re"], wts["ssm_c_im"])


def _out_mlp_kernel(x_ref, oa_ref, yf_ref, yb_ref, u_ref, mod_ref, vec_ref, wout_ref, wglu_ref, w1_ref, w2_ref,
                    y_ref, *, final):
    g1 = mod_ref[:, 2 * D_MODEL:3 * D_MODEL]
    sh2 = mod_ref[:, 3 * D_MODEL:4 * D_MODEL]
    sc2 = mod_ref[:, 4 * D_MODEL:5 * D_MODEL]
    g2 = mod_ref[:, 5 * D_MODEL:6 * D_MODEL]
    ys = yf_ref[...] + yb_ref[...] + u_ref[...] * vec_ref[:, VEC_SSMD:VEC_SSMD + SSM_WIDTH]
    z = _dot(_gelu_tanh(ys).astype(BF16), wglu_ref[...])
    o_ssm = (z[:, 0:SSM_WIDTH] * _sigmoid(z[:, SSM_WIDTH:2 * SSM_WIDTH])).astype(BF16)
    mix = (_dot(oa_ref[:, 0:512], wout_ref[0:512, :]) + _dot(o_ssm, wout_ref[512:768, :])
           + _dot(oa_ref[:, 512:O_COLS], wout_ref[768:1024, :]))
    x1 = x_ref[...] + g1 * mix
    ms = jnp.mean(x1 * x1, axis=-1, keepdims=True)
    h = x1 * lax.rsqrt(ms + EPS) * vec_ref[:, VEC_N2G:VEC_N2G + D_MODEL]
    h = h * (1.0 + sc2) + sh2
    hb = h.astype(BF16)
    mlp = jnp.zeros_like(x1)
    for c0 in range(0, D_FF, FF_CHUNK):
        a = jnp.maximum(_dot(hb, w1_ref[:, c0:c0 + FF_CHUNK]), 0.0)
        mlp = mlp + _dot((a * a).astype(BF16), w2_ref[c0:c0 + FF_CHUNK, :])
    x2 = x1 + g2 * mlp
    if final:
        ms2 = jnp.mean(x2 * x2, axis=-1, keepdims=True)
        x2 = x2 * lax.rsqrt(ms2 + EPS) * vec_ref[:, VEC_FNG:VEC_FNG + D_MODEL]
    y_ref[...] = x2


def _out_mlp(x2d, o_attn, y_fwd, y_bwd, u, mod3, layer, wts, mlp_weights, n_batch, seq, latent, final):
    tm = MLP_TILE
    n_tok = n_batch * seq
    w_out, w1, w2 = mlp_weights
    resident = lambda shape: pl.BlockSpec(shape, lambda i: (0, 0), pipeline_mode=pl.Buffered(1))
    if latent:
        tpb = seq // tm
        mod_row = lambda i: (layer * MOD_ROWS + 1 + i // tpb, 0, 0)
    else:
        mod_row = lambda i: (layer * MOD_ROWS, 0, 0)
    tile = lambda i: (i, 0)
    return pl.pallas_call(
        functools.partial(_out_mlp_kernel, final=final),
        grid=(n_tok // tm,),
        in_specs=[
            pl.BlockSpec((tm, D_MODEL), tile),
            pl.BlockSpec((tm, O_COLS), tile),
            pl.BlockSpec((tm, SSM_WIDTH), tile),
            pl.BlockSpec((tm, SSM_WIDTH), tile),
            pl.BlockSpec((tm, SSM_WIDTH), tile),
            pl.BlockSpec((None, 1, N_MOD * D_MODEL), mod_row),
            _layer_spec((1, VEC_END), layer),
            resident((D_MODEL, D_MODEL)),
            _layer_spec((SSM_WIDTH, 2 * SSM_WIDTH), layer, single_buffer=True),
            resident((D_MODEL, D_FF)),
            resident((D_FF, D_MODEL)),
        ],
        out_specs=pl.BlockSpec((tm, D_MODEL), tile),
        out_shape=jax.ShapeDtypeStruct((n_tok, D_MODEL), F32),
        compiler_params=pltpu.CompilerParams(vmem_limit_bytes=VMEM_LIMIT),
        name="out_mlp_%s%s" % ("lat" if latent else "ctx", "_final" if final else ""),
    )(x2d, o_attn, y_fwd, y_bwd, u, mod3, wts["vecs"], w_out, wts["w_glu"], w1, w2)


def _rope_tables(t_len, chunk, n_chunks):
    n = chunk // 4
    rows = t_len // GRID_W
    row = np.repeat(np.arange(rows), GRID_W).astype(np.float32)
    col = np.tile(np.arange(GRID_W), rows).astype(np.float32)
    freq = (np.float32(ROPE_THETA) ** (-np.arange(n, dtype=np.float32) / np.float32(n))).astype(np.float32)
    ang_r = (row[:, None] * freq).astype(np.float32)
    ang_c = (col[:, None] * freq).astype(np.float32)
    cr, sr, cc, sc = np.cos(ang_r), np.sin(ang_r), np.cos(ang_c), np.sin(ang_c)
    z = np.zeros_like(cr)
    tabs = np.stack([np.concatenate([cr, cr, cc, cc], axis=-1),
                     np.concatenate([-sr, z, -sc, z], axis=-1),
                     np.concatenate([z, sr, z, sc], axis=-1)]).astype(np.float32)
    return np.tile(tabs, (1, 1, n_chunks))


def _mla_rope_tables(t_len):
    tabs = _rope_tables(t_len, MLA_ROPE, 1)
    ident = np.stack([np.ones((t_len, MLA_NOPE), np.float32), np.zeros((t_len, MLA_NOPE), np.float32),
                      np.zeros((t_len, MLA_NOPE), np.float32)])
    tail = ident[:, :, :MLA_BLK - MLA_QK]
    return np.concatenate([ident, tabs, tail], axis=-1)


def _pack_vectors(p, n_layers):
    def rows(a):
        return a.reshape(n_layers, 1, -1)

    fng = jnp.broadcast_to(p["final_norm_g"].reshape(1, 1, D_MODEL), (n_layers, 1, D_MODEL))
    return jnp.concatenate([
        rows(p["norm1_g"]), rows(p["norm2_g"]), fng,
        jnp.tile(rows(p["gqa_qn_g"]), (1, 1, GQA_HEADS)),
        jnp.tile(rows(p["gqa_kn_g"]), (1, 1, GQA_KV_HEADS)),
        rows(p["mla_qn_g"]), jnp.zeros((n_layers, 1, CQ_PAD - MLA_Q_RANK), F32),
        rows(p["mla_kvn_g"]),
        jnp.tile(rows(p["diff_subln_g"]), (1, 1, DIFF_HEADS)),
        rows(p["ssm_d"]),
        rows(p["diff_lq1"]), rows(p["diff_lk1"]), rows(p["diff_lq2"]), rows(p["diff_lk2"]),
    ], axis=-1)


def kernel(x_prompt, x_sample, cache_diff_k, cache_diff_v, cache_gqa_k, cache_gqa_v, cache_mla_ckv, cache_mla_krope, state_ssm_re, state_ssm_im, c, c_ctx, norm1_g, norm2_g, w_ada, b_ada, w_in, w_out, diff_lq1, diff_lk1, diff_lq2, diff_lk2, diff_subln_g, gqa_qn_g, gqa_kn_g, ssm_a_re, ssm_a_im, ssm_log_dt, ssm_b_re, ssm_b_im, ssm_c_re, ssm_c_im, ssm_d, ssm_w_glu, mla_qn_g, mla_kvn_g, mla_w_uq, mla_w_ukv, mlp_w1, mlp_w2, final_norm_g):
    n_layers = w_in.shape[0]
    n_ctx, seq, _ = x_prompt.shape
    n_lat, t_lat, _ = x_sample.shape
    past = cache_diff_k.shape[2]
    assert n_lat + 1 <= MOD_ROWS and n_lat == 4 and n_ctx % 8 == 0
    assert seq == TOKEN_TILE and t_lat % TOKEN_TILE == 0
    assert (n_ctx * seq) % MLP_TILE == 0 and t_lat % MLP_TILE == 0
    assert PROJ_TILE % seq == 0 and n_ctx % (PROJ_TILE // seq) == 0 and t_lat % PROJ_TILE == 0

    p = dict(norm1_g=norm1_g, norm2_g=norm2_g, final_norm_g=final_norm_g, gqa_qn_g=gqa_qn_g, gqa_kn_g=gqa_kn_g,
             mla_qn_g=mla_qn_g, mla_kvn_g=mla_kvn_g, diff_subln_g=diff_subln_g, ssm_d=ssm_d, diff_lq1=diff_lq1,
             diff_lk1=diff_lk1, diff_lq2=diff_lq2, diff_lk2=diff_lk2)
    ssm_rows = SSM_GROUPS * SSM_GROUP
    wts = {
        "vecs": _pack_vectors(p, n_layers),
        "w_in_t": jnp.swapaxes(w_in, 1, 2),
        "w_uq": mla_w_uq, "w_ukv": mla_w_ukv,
        "w_glu": ssm_w_glu.astype(BF16),
        "ssm_par": jnp.stack([ssm_a_re.reshape(n_layers, 2, SSM_LANES), ssm_a_im.reshape(n_layers, 2, SSM_LANES),
                              jnp.repeat(ssm_log_dt, SSM_STATE, axis=-1)], axis=2),
        "ssm_b_re": jnp.swapaxes(ssm_b_re, 3, 4).reshape(n_layers, 2, ssm_rows, SSM_STATE),
        "ssm_b_im": jnp.swapaxes(ssm_b_im, 3, 4).reshape(n_layers, 2, ssm_rows, SSM_STATE),
        "ssm_c_re": ssm_c_re.reshape(n_layers, 2, ssm_rows, SSM_STATE),
        "ssm_c_im": ssm_c_im.reshape(n_layers, 2, ssm_rows, SSM_STATE),
    }

    cond = jnp.concatenate([c_ctx[None], c, jnp.zeros((MOD_ROWS - 1 - n_lat, D_MODEL), F32)], axis=0)
    mod3 = _adaln(cond, w_ada, b_ada).reshape(n_layers * MOD_ROWS, 1, N_MOD * D_MODEL)

    rope_tabs = (jnp.asarray(_rope_tables(t_lat, DIFF_QK, 256 // DIFF_QK)),
                 jnp.asarray(_rope_tables(t_lat, GQA_HEAD_DIM, 256 // GQA_HEAD_DIM)),
                 jnp.asarray(_mla_rope_tables(t_lat)))
    caches = (jnp.transpose(cache_diff_k, (0, 1, 3, 4, 2)).reshape(n_lat, n_layers, 256, past),
              jnp.transpose(cache_diff_v, (0, 1, 3, 4, 2)).reshape(n_lat, n_layers, 256, past),
              jnp.transpose(cache_gqa_k, (0, 1, 3, 4, 2)).reshape(n_lat, n_layers, 128, past),
              jnp.transpose(cache_gqa_v, (0, 1, 3, 4, 2)).reshape(n_lat, n_layers, 128, past),
              cache_mla_ckv,
              jnp.swapaxes(cache_mla_krope, 2, 3))

    lam_inits = [0.8 - 0.6 * math.exp(-0.3 * l) for l in range(n_layers)]

    x = x_prompt.reshape(n_ctx * seq, D_MODEL)
    zero_state = jnp.zeros((2, n_ctx, 2 * SSM_LANES), F32)
    zero_spec = _const_spec((2, n_ctx, 2 * SSM_LANES))
    kept = None
    states = []
    mlp_weights = []
    for l in range(n_layers):
        res = _in_proj(x, mod3, l, wts, n_ctx, seq, latent=False, prev_caches=kept)
        q, k, v, u = res[:4]
        kept = res[4:]
        o_attn, *mlp_w = _attention(q, k, v, wts["vecs"], l, lam_inits[l], n_ctx, seq,
                                    to_bf16=(w_out, mlp_w1, mlp_w2))
        mlp_weights.append(mlp_w)
        y_fwd, y_bwd, h_fin = _ssm_scan(u, zero_state, zero_spec, wts, l, n_ctx, seq, CTX_SSM_CHUNK,
                                        emit_state=True)
        x = _out_mlp(x, o_attn, y_fwd.reshape(n_ctx * seq, SSM_WIDTH), y_bwd.reshape(n_ctx * seq, SSM_WIDTH), u,
                     mod3, l, wts, mlp_w, n_ctx, seq, latent=False, final=(l == n_layers - 1))
        states.append(h_fin)
    y_prompt = x.reshape(n_ctx, seq, D_MODEL)

    x = x_sample.reshape(n_lat * t_lat, D_MODEL)
    h0 = jnp.concatenate([state_ssm_re.reshape(n_lat, n_layers, 2, SSM_LANES),
                          state_ssm_im.reshape(n_lat, n_layers, 2, SSM_LANES)], axis=-1)
    h0 = jnp.transpose(h0, (1, 2, 0, 3))
    h0 = jnp.concatenate([h0, h0], axis=2)
    for l in range(n_layers):
        h0_spec = _layer_spec((2, 2 * n_lat, 2 * SSM_LANES), l)
        q, k, v, u = _in_proj(x, mod3, l, wts, n_lat, t_lat, latent=True, rope_tabs=rope_tabs)
        kc, vc = _cache_prep(caches, l, wts["w_ukv"])
        o_attn, = _attention(q, k, v, wts["vecs"], l, lam_inits[l], n_lat, t_lat, k_cache=kc, v_cache=vc)
        y_fwd, y_bwd = _ssm_scan(u, h0, h0_spec, wts, l, n_lat, t_lat, LAT_SSM_CHUNK, emit_state=False)
        x = _out_mlp(x, o_attn, y_fwd.reshape(n_lat * t_lat, SSM_WIDTH), y_bwd.reshape(n_lat * t_lat, SSM_WIDTH), u,
                     mod3, l, wts, mlp_weights[l], n_lat, t_lat, latent=True, final=(l == n_layers - 1))
    y_sample = x.reshape(n_lat, t_lat, D_MODEL)

    cdk, cdv, cgk, cgv, cckv, ckr = kept

    def token_major(a, heads, dim):
        return jnp.transpose(a.reshape(n_ctx, n_layers, heads, dim, seq), (0, 1, 4, 2, 3))

    st = jnp.transpose(jnp.stack(states), (2, 0, 1, 3))
    return (y_prompt, y_sample,
            token_major(cdk, DIFF_HEADS, 2 * DIFF_QK), token_major(cdv, DIFF_HEADS, DIFF_V),
            token_major(cgk, GQA_KV_HEADS, GQA_HEAD_DIM), token_major(cgv, GQA_KV_HEADS, GQA_HEAD_DIM),
            cckv, jnp.swapaxes(ckr, 2, 3),
            st[..., :SSM_LANES].reshape(n_ctx, n_layers, 2, SSM_GROUPS, SSM_STATE),
            st[..., SSM_LANES:].reshape(n_ctx, n_layers, 2, SSM_GROUPS, SSM_STATE))
```

```python
import functools
import math

import numpy as np
import jax
import jax.numpy as jnp
from jax import lax
from jax.experimental import pallas as pl
from jax.experimental.pallas import tpu as pltpu

F32 = jnp.float32
BF16 = jnp.bfloat16

D_MODEL = 1024
GRID_W = 64
ROPE_THETA = 10000.0
EPS = 1e-6
DIFF_HEADS, DIFF_QK, DIFF_V = 4, 32, 64
GQA_HEADS, GQA_KV_HEADS, GQA_HEAD_DIM = 4, 2, 64
SSM_WIDTH, SSM_GROUP, SSM_STATE = 256, 16, 64
SSM_GROUPS = SSM_WIDTH // SSM_GROUP
SSM_LANES = SSM_GROUPS * SSM_STATE
MLA_HEADS, MLA_Q_RANK, MLA_KV_RANK, MLA_NOPE, MLA_ROPE, MLA_V = 4, 192, 128, 64, 32, 64
MLA_QK = MLA_NOPE + MLA_ROPE
D_FF = 4 * D_MODEL
N_MOD = 6
LOG2E = math.log2(math.e)
IN_COLS = 1888
MOD_ROWS = 8

P_DQ, P_DK, P_DV, P_GQ, P_GK, P_GV, P_U, P_CQ, P_CKV, P_KR, P_END = (
    0, 256, 512, 768, 1024, 1152, 1280, 1536, 1792, 1920, 2048)
S_CQ_END, S_CKV, S_KR = 1728, 1728, 1856
CQ_PAD = P_CKV - P_CQ
MLA_BLK = 128
Q_DIFF, Q_GQA, Q_MLA, Q_COLS = 0, 256, 512, 1024
K_DIFF, K_GQA, K_MLA, K_COLS = 0, 256, 512, 1024
V_DIFF, V_GQA, V_MLA, V_COLS = 0, 256, 512, 1024
O_COLS = 768

VEC_N1G, VEC_N2G, VEC_FNG, VEC_QNG, VEC_KNG, VEC_MQNG, VEC_KVNG, VEC_SUBG, VEC_SSMD, VEC_LAM, VEC_END = (
    0, 1024, 2048, 3072, 3328, 3456, 3712, 3840, 4096, 4352, 4480)

TOKEN_TILE = 256
LAT_QUERY_TILE = 512
CTX_SEQ_PER_STEP = 2
PROJ_TILE = 512
MLP_TILE = 512
FF_CHUNK = 1024
CTX_SSM_CHUNK = 32
LAT_SSM_CHUNK = 256
STACK_MAX_KEYS = 512
SHORT_WAVE_BYTES = 4 * 1024 * 1024
LONG_WAVE_BYTES = 7 * 1024 * 1024
VMEM_LIMIT = 48 * 1024 * 1024


def _dot(a, b):
    return jnp.dot(a, b, preferred_element_type=F32)


def _dot_nt(a, b):
    return lax.dot_general(a, b, (((1,), (1,)), ((), ())), preferred_element_type=F32)


def _block_ones(width, seg):
    shift = seg.bit_length() - 1
    r = jnp.right_shift(lax.broadcasted_iota(jnp.int32, (width, width), 0), shift)
    c = jnp.right_shift(lax.broadcasted_iota(jnp.int32, (width, width), 1), shift)
    return jnp.where(r == c, 1.0, 0.0).astype(BF16)


def _seg_sum_sq(y, seg):
    ones = _block_ones(y.shape[-1], seg)
    sq = y * y
    hi = sq.astype(BF16)
    lo = (sq - hi.astype(F32)).astype(BF16)
    return _dot(hi, ones) + _dot(lo, ones)


def _rope(x, tab_ref, quarter):
    outs = []
    for j in range(x.shape[-1] // 128):
        sl = slice(j * 128, (j + 1) * 128)
        xs = x[:, sl]
        up = pltpu.roll(xs, 128 - quarter, axis=1)
        dn = pltpu.roll(xs, quarter, axis=1)
        outs.append(xs * tab_ref[0, :, sl] + up * tab_ref[1, :, sl] + dn * tab_ref[2, :, sl])
    return outs[0] if len(outs) == 1 else jnp.concatenate(outs, axis=-1)


def _sigmoid(x):
    return 1.0 / (1.0 + jnp.exp(-x))


def _lane_lt(width, bound):
    return lax.broadcasted_iota(jnp.int32, (1, width), 1) < bound


def _dup_halves(x):
    swapped = pltpu.roll(x, 64, axis=1)
    left = _lane_lt(128, 64)
    return jnp.where(left, x, swapped), jnp.where(left, swapped, x)


def _const_spec(shape):
    nd = len(shape)
    return pl.BlockSpec(shape, lambda *_: (0,) * nd)


def _layer_spec(shape, layer, single_buffer=False):
    nd = len(shape)
    kw = {"pipeline_mode": pl.Buffered(1)} if single_buffer else {}
    return pl.BlockSpec((None,) + tuple(shape), lambda *_: (layer,) + (0,) * nd, **kw)


def _adaln_kernel(cond_ref, w_ref, b_ref, o_ref):
    c = cond_ref[...]
    s = c * _sigmoid(c)
    o_ref[...] = _dot(s.astype(BF16), w_ref[...].astype(BF16)) + b_ref[...]


def _adaln(cond, w_ada, b_ada):
    n_layers = w_ada.shape[0]
    tn = 1024
    return pl.pallas_call(
        _adaln_kernel,
        grid=(n_layers, N_MOD * D_MODEL // tn),
        in_specs=[
            pl.BlockSpec((MOD_ROWS, D_MODEL), lambda l, j: (0, 0)),
            pl.BlockSpec((None, D_MODEL, tn), lambda l, j: (l, 0, j)),
            pl.BlockSpec((None, 1, tn), lambda l, j: (l, 0, j)),
        ],
        out_specs=pl.BlockSpec((None, MOD_ROWS, tn), lambda l, j: (l, 0, j)),
        out_shape=jax.ShapeDtypeStruct((n_layers, MOD_ROWS, N_MOD * D_MODEL), F32),
        compiler_params=pltpu.CompilerParams(vmem_limit_bytes=VMEM_LIMIT),
        name="adaln",
    )(cond, w_ada, b_ada.reshape(n_layers, 1, N_MOD * D_MODEL))


N_CACHE = 6


def _in_proj_kernel(*refs, latent, n_alias, layer):
    x_ref, mod_ref, vec_ref, wint_ref, wuq_ref, wukv_ref = refs[:6]
    pos = 6
    if latent:
        rope_d_ref, rope_g_ref, rope_m_ref = refs[pos:pos + 3]
        pos += 3
    pos += n_alias
    q_ref, k_ref, v_ref, u_ref = refs[pos:pos + 4]
    pos += 4
    if not latent:
        cdk_ref, cdv_ref, cgk_ref, cgv_ref, cckv_ref, ckr_ref = refs[pos:pos + N_CACHE]
        pos += N_CACHE
    win_scr, wuq_scr, wukv_scr = refs[pos:]

    @pl.when(pl.program_id(0) == 0)
    def _():
        aligned = (S_CQ_END // 128) * 128
        for r0 in range(0, aligned, 256):
            r1 = min(r0 + 256, aligned)
            win_scr[:, r0:r1] = wint_ref[r0:r1, :].T.astype(BF16)
        tail = wint_ref[aligned:aligned + 128, :].T
        win_scr[:, aligned:aligned + 128] = jnp.where(_lane_lt(128, S_CQ_END - aligned), tail, 0.0).astype(BF16)
        win_scr[:, P_CKV:P_KR] = wint_ref[S_CKV:S_KR, :].T.astype(BF16)
        kr_rows = jnp.concatenate([jnp.zeros((MLA_NOPE, D_MODEL), F32), wint_ref[S_KR:IN_COLS, :],
                                   jnp.zeros((MLA_BLK - MLA_QK, D_MODEL), F32)], axis=0)
        win_scr[:, P_KR:P_END] = kr_rows.T.astype(BF16)
        wuq_scr[...] = jnp.zeros(wuq_scr.shape, BF16)
        wuq = wuq_ref[...]
        for hd in range(MLA_HEADS):
            wuq_scr[0:MLA_Q_RANK, hd * MLA_BLK:hd * MLA_BLK + MLA_QK] = (
                wuq[:, hd * MLA_QK:(hd + 1) * MLA_QK].astype(BF16))
        wukv_scr[...] = wukv_ref[...].astype(BF16)

    x = x_ref[...]
    sh1 = mod_ref[:, 0:D_MODEL]
    sc1 = mod_ref[:, D_MODEL:2 * D_MODEL]
    ms = jnp.mean(x * x, axis=-1, keepdims=True)
    h = x * lax.rsqrt(ms + EPS) * vec_ref[:, VEC_N1G:VEC_N1G + D_MODEL]
    h = h * (1.0 + sc1) + sh1
    proj = _dot(h.astype(BF16), win_scr[...])

    dq = proj[:, P_DQ:P_DK]
    dk = proj[:, P_DK:P_DV]
    dv = proj[:, P_DV:P_GQ]
    gq = proj[:, P_GQ:P_GK]
    gk = proj[:, P_GK:P_GV]
    gv = proj[:, P_GV:P_U]
    u_ref[...] = proj[:, P_U:P_CQ]
    cq = proj[:, P_CQ:P_CKV]
    ckv = proj[:, P_CKV:P_KR]
    krb = proj[:, P_KR:P_END]

    qng = vec_ref[:, VEC_QNG:VEC_QNG + 256]
    kng = vec_ref[:, VEC_KNG:VEC_KNG + 128]
    gq = gq * lax.rsqrt(_seg_sum_sq(gq, GQA_HEAD_DIM) * (1.0 / GQA_HEAD_DIM) + EPS) * qng
    gk = gk * lax.rsqrt(_seg_sum_sq(gk, GQA_HEAD_DIM) * (1.0 / GQA_HEAD_DIM) + EPS) * kng
    ckv_n = (ckv * lax.rsqrt(jnp.mean(ckv * ckv, axis=-1, keepdims=True) + EPS)
             * vec_ref[:, VEC_KVNG:VEC_KVNG + MLA_KV_RANK])
    cq_ms = jnp.sum(cq * cq, axis=-1, keepdims=True) * (1.0 / MLA_Q_RANK)
    cq_n = cq * lax.rsqrt(cq_ms + EPS) * vec_ref[:, VEC_MQNG:VEC_MQNG + CQ_PAD]
    mq = _dot(cq_n.astype(BF16), wuq_scr[...])
    kv = _dot(ckv_n.astype(BF16), wukv_scr[...])

    if not latent:
        seq = cckv_ref.shape[2]

        def keep(ref, val, feature_major=True, rows=None):
            for s in range(ref.shape[0]):
                piece = val[s * seq:(s + 1) * seq]
                if feature_major:
                    piece = piece.T
                if rows is not None:
                    piece = piece[rows]
                for l in range(ref.shape[1]):
                    ref[s, l] = piece if ref.shape[1] == 1 or l == layer else jnp.zeros_like(piece)

        keep(cdk_ref, dk)
        keep(cdv_ref, dv)
        keep(cgk_ref, gk)
        keep(cgv_ref, gv)
        keep(cckv_ref, ckv_n, feature_major=False)
        keep(ckr_ref, krb, rows=slice(MLA_NOPE, MLA_QK))
    else:
        dq = _rope(dq, rope_d_ref, DIFF_QK // 4)
        dk = _rope(dk, rope_d_ref, DIFF_QK // 4)
        gq = _rope(gq, rope_g_ref, GQA_HEAD_DIM // 4)
        gk = _rope(gk, rope_g_ref, GQA_HEAD_DIM // 4)
        mq = jnp.concatenate(
            [_rope(mq[:, hd * MLA_BLK:(hd + 1) * MLA_BLK], rope_m_ref, MLA_ROPE // 4)
             for hd in range(MLA_HEADS)], axis=-1)
        krb = _rope(krb, rope_m_ref, MLA_ROPE // 4)

    q_ref[:, Q_DIFF:Q_GQA] = (dq * (LOG2E * DIFF_QK ** -0.5)).astype(BF16)
    q_ref[:, Q_GQA:Q_MLA] = (gq * (LOG2E * GQA_HEAD_DIM ** -0.5)).astype(BF16)
    q_ref[:, Q_MLA:Q_COLS] = (mq * (LOG2E * MLA_QK ** -0.5)).astype(BF16)
    k_ref[:, K_DIFF:K_GQA] = dk.astype(BF16)
    gk0, gk1 = _dup_halves(gk)
    k_ref[:, K_GQA:K_GQA + 128] = gk0.astype(BF16)
    k_ref[:, K_GQA + 128:K_MLA] = gk1.astype(BF16)
    nope = _lane_lt(MLA_BLK, MLA_NOPE)
    for hd in range(MLA_HEADS):
        blk = slice(hd * MLA_BLK, (hd + 1) * MLA_BLK)
        k_ref[:, K_MLA + hd * MLA_BLK:K_MLA + (hd + 1) * MLA_BLK] = jnp.where(nope, kv[:, blk], krb).astype(BF16)
    v_ref[:, V_DIFF:V_GQA] = dv.astype(BF16)
    gv0, gv1 = _dup_halves(gv)
    v_ref[:, V_GQA:V_GQA + 128] = gv0.astype(BF16)
    v_ref[:, V_GQA + 128:V_MLA] = gv1.astype(BF16)
    v_ref[:, V_MLA:V_COLS] = kv.astype(BF16)


def _in_proj(x2d, mod3, layer, wts, n_batch, seq, latent, rope_tabs=None, prev_caches=None):
    tm = PROJ_TILE
    n_tok = n_batch * seq
    n_layers = wts["vecs"].shape[0]
    if latent:
        tpb = seq // tm
        mod_row = lambda i: (layer * MOD_ROWS + 1 + i // tpb, 0, 0)
    else:
        mod_row = lambda i: (layer * MOD_ROWS, 0, 0)
    tile = lambda i: (i, 0)
    in_specs = [
        pl.BlockSpec((tm, D_MODEL), tile),
        pl.BlockSpec((None, 1, N_MOD * D_MODEL), mod_row),
        _layer_spec((1, VEC_END), layer),
        _layer_spec((IN_COLS, D_MODEL), layer, single_buffer=True),
        _layer_spec((MLA_Q_RANK, MLA_HEADS * MLA_QK), layer, single_buffer=True),
        _layer_spec((MLA_KV_RANK, MLA_HEADS * MLA_BLK), layer, single_buffer=True),
    ]
    args = [x2d, mod3, wts["vecs"], wts["w_in_t"], wts["w_uq"], wts["w_ukv"]]
    out_shape = [
        jax.ShapeDtypeStruct((n_tok, Q_COLS), BF16),
        jax.ShapeDtypeStruct((n_tok, K_COLS), BF16),
        jax.ShapeDtypeStruct((n_tok, V_COLS), BF16),
        jax.ShapeDtypeStruct((n_tok, SSM_WIDTH), F32),
    ]
    out_specs = [
        pl.BlockSpec((tm, Q_COLS), tile),
        pl.BlockSpec((tm, K_COLS), tile),
        pl.BlockSpec((tm, V_COLS), tile),
        pl.BlockSpec((tm, SSM_WIDTH), tile),
    ]
    aliases = {}
    n_alias = 0
    if latent:
        pos = lambda i: (0, i % tpb, 0)
        in_specs += [pl.BlockSpec((3, tm, 256), pos), pl.BlockSpec((3, tm, 256), pos),
                     pl.BlockSpec((3, tm, 128), pos)]
        args += list(rope_tabs)
    else:
        spt = tm // seq
        cache_dims = [(256, seq), (256, seq), (128, seq), (128, seq), (seq, MLA_KV_RANK), (MLA_ROPE, seq)]
        for dims in cache_dims:
            out_shape.append(jax.ShapeDtypeStruct((n_batch, n_layers) + dims, F32))
            if prev_caches is None:
                out_specs.append(pl.BlockSpec((spt, n_layers) + dims, lambda i: (i, 0, 0, 0)))
            else:
                out_specs.append(pl.BlockSpec((spt, 1) + dims, lambda i: (i, layer, 0, 0)))
        if prev_caches is not None:
            n_alias = N_CACHE
            for j, arr in enumerate(prev_caches):
                aliases[len(args)] = 4 + j
                in_specs.append(pl.BlockSpec(memory_space=pl.ANY))
                args.append(arr)
    return pl.pallas_call(
        functools.partial(_in_proj_kernel, latent=latent, n_alias=n_alias, layer=layer),
        grid=(n_tok // tm,),
        in_specs=in_specs,
        out_specs=out_specs,
        out_shape=out_shape,
        input_output_aliases=aliases,
        scratch_shapes=[pltpu.VMEM((D_MODEL, P_END), BF16),
                        pltpu.VMEM((CQ_PAD, MLA_HEADS * MLA_BLK), BF16),
                        pltpu.VMEM((MLA_KV_RANK, MLA_HEADS * MLA_BLK), BF16)],
        compiler_params=pltpu.CompilerParams(vmem_limit_bytes=VMEM_LIMIT),
        name="in_proj_lat" if latent else "in_proj_ctx",
    )(*args)


def _cache_prep_kernel(cdk_ref, cdv_ref, cgk_ref, cgv_ref, cckv_ref, ckr_ref, wukv_ref, k_ref, v_ref):
    past = k_ref.shape[0]
    kv = _dot(cckv_ref[...].astype(BF16), wukv_ref[...].astype(BF16))
    krb = jnp.concatenate([jnp.zeros((MLA_NOPE, past), F32), ckr_ref[...],
                           jnp.zeros((MLA_BLK - MLA_QK, past), F32)], axis=0).T
    k_ref[:, K_DIFF:K_GQA] = cdk_ref[...].T.astype(BF16)
    gk0, gk1 = _dup_halves(cgk_ref[...].T)
    k_ref[:, K_GQA:K_GQA + 128] = gk0.astype(BF16)
    k_ref[:, K_GQA + 128:K_MLA] = gk1.astype(BF16)
    nope = _lane_lt(MLA_BLK, MLA_NOPE)
    for hd in range(MLA_HEADS):
        blk = slice(hd * MLA_BLK, (hd + 1) * MLA_BLK)
        k_ref[:, K_MLA + hd * MLA_BLK:K_MLA + (hd + 1) * MLA_BLK] = jnp.where(nope, kv[:, blk], krb).astype(BF16)
    v_ref[:, V_DIFF:V_GQA] = cdv_ref[...].T.astype(BF16)
    gv0, gv1 = _dup_halves(cgv_ref[...].T)
    v_ref[:, V_GQA:V_GQA + 128] = gv0.astype(BF16)
    v_ref[:, V_GQA + 128:V_MLA] = gv1.astype(BF16)
    v_ref[:, V_MLA:V_COLS] = kv.astype(BF16)


def _cache_prep(caches, layer, w_ukv):
    cdk, cdv, cgk, cgv, cckv, ckr = caches
    n_batch, _, _, past = cdk.shape

    def spec(rows, cols):
        return pl.BlockSpec((None, None, rows, cols), lambda b: (b, layer, 0, 0))

    return pl.pallas_call(
        _cache_prep_kernel,
        grid=(n_batch,),
        in_specs=[spec(256, past), spec(256, past), spec(128, past), spec(128, past),
                  spec(past, MLA_KV_RANK), spec(MLA_ROPE, past),
                  _layer_spec((MLA_KV_RANK, MLA_HEADS * MLA_BLK), layer)],
        out_specs=[pl.BlockSpec((None, past, K_COLS), lambda b: (b, 0, 0)),
                   pl.BlockSpec((None, past, V_COLS), lambda b: (b, 0, 0))],
        out_shape=[jax.ShapeDtypeStruct((n_batch, past, K_COLS), BF16),
                   jax.ShapeDtypeStruct((n_batch, past, V_COLS), BF16)],
        name="cache_prep",
    )(cdk, cdv, cgk, cgv, cckv, ckr, w_ukv)


def _attend_all(jobs, wave_bytes):
    size = lambda job: 4 * job[0].shape[0] * sum(k.shape[0] for k in job[1])
    if len(jobs) > 1 and sum(size(j) for j in jobs) > wave_bytes:
        n = 1
        while n < len(jobs) and sum(size(j) for j in jobs[:n + 1]) <= wave_bytes:
            n += 1
        return _attend_all(jobs[:n], wave_bytes) + _attend_all(jobs[n:], wave_bytes)
    scores = [[_dot_nt(qm, k) for k in ks] for qm, ks, _ in jobs]
    weights, denoms = [], []
    for ss in scores:
        m = functools.reduce(jnp.maximum, [jnp.max(s, axis=-1, keepdims=True) for s in ss])
        es = [jnp.exp2(s - m) for s in ss]
        denoms.append(functools.reduce(jnp.add, [jnp.sum(e, axis=-1, keepdims=True) for e in es]))
        weights.append([e.astype(BF16) for e in es])
    outs = [functools.reduce(jnp.add, [_dot(e, v) for e, v in zip(es, vs)])
            for es, (_, _, vs) in zip(weights, jobs)]
    return [o * (1.0 / d) for o, d in zip(outs, denoms)]


def _lane_mask(width, lo, hi):
    lane = lax.broadcasted_iota(jnp.int32, (1, width), 1)
    return jnp.where((lane >= lo) & (lane < hi), 1.0, 0.0).astype(BF16)


def _attn_kernel(*refs, lam_init, past, n_cast, seqs):
    if past:
        q_ref, kc_ref, ko_ref, vc_ref, vo_ref, vec_ref, o_ref = refs
        k_refs, v_refs = (kc_ref, ko_ref), (vc_ref, vo_ref)
    else:
        q_ref, k_ref, v_ref, vec_ref = refs[:4]
        k_refs, v_refs = (k_ref,), (v_ref,)
        o_ref = refs[4 + n_cast]
        for src, dst in zip(refs[4:4 + n_cast], refs[5 + n_cast:]):
            dst[...] = src[...].astype(BF16)
    tq = q_ref.shape[0] // seqs
    stack_rows = sum(r.shape[0] for r in k_refs) // seqs <= STACK_MAX_KEYS
    lq1 = vec_ref[:, VEC_LAM:VEC_LAM + 32]
    lk1 = vec_ref[:, VEC_LAM + 32:VEC_LAM + 64]
    lq2 = vec_ref[:, VEC_LAM + 64:VEC_LAM + 96]
    lk2 = vec_ref[:, VEC_LAM + 96:VEC_LAM + 128]
    lam = (jnp.exp(jnp.sum(lq1 * lk1, axis=-1, keepdims=True))
           - jnp.exp(jnp.sum(lq2 * lk2, axis=-1, keepdims=True)) + lam_init)
    subg = vec_ref[:, VEC_SUBG:VEC_SUBG + 256]
    lane = lax.broadcasted_iota(jnp.int32, (1, 256), 1)
    left = _lane_lt(128, GQA_HEAD_DIM)
    upper = _lane_mask(MLA_BLK, MLA_NOPE, MLA_BLK)

    jobs = []

    def submit(qm, ks, vs):
        jobs.append((qm, ks, vs))
        return len(jobs) - 1

    def masked_heads(qv, ks, vs, masks):
        if stack_rows:
            j = submit(jnp.concatenate([qv * m for m in masks], axis=0), ks, vs)
            return lambda res: [res[j][i * tq:(i + 1) * tq] for i in range(len(masks))]
        js = [submit(qv * m, ks, vs) for m in masks]
        return lambda res: [res[j] for j in js]

    pending = []

    def flush():
        res = _attend_all(jobs, SHORT_WAVE_BYTES if stack_rows else LONG_WAVE_BYTES)
        for finish in pending:
            finish(res)
        jobs.clear()
        pending.clear()

    def finish_diff(res, rows, diff):
        parts = diff(res)
        acc = jnp.zeros((tq, 256), F32)
        for hd in range(DIFF_HEADS):
            head = (lane >= hd * DIFF_V) & (lane < (hd + 1) * DIFF_V)
            acc = jnp.where(head, parts[2 * hd] - lam * parts[2 * hd + 1], acc)
        ms = _seg_sum_sq(acc, DIFF_V) * (1.0 / DIFF_V)
        o_ref[rows, 0:256] = (acc * lax.rsqrt(ms + EPS) * subg * (1.0 - lam_init)).astype(o_ref.dtype)

    def finish_gqa(res, rows, gqa):
        for blk in range(GQA_KV_HEADS):
            o_l, o_r = gqa[blk](res)
            o_ref[rows, 256 + blk * 128:256 + (blk + 1) * 128] = jnp.where(left, o_l, o_r).astype(o_ref.dtype)

    def finish_mla(res, rows, mla):
        for pair in range(MLA_HEADS // 2):
            both = pltpu.roll(res[mla[2 * pair]], MLA_V, axis=1) + res[mla[2 * pair + 1]]
            o_ref[rows, 512 + pair * 128:512 + (pair + 1) * 128] = both.astype(o_ref.dtype)

    for sq in range(seqs):
        def keys(lo, hi, sq=sq):
            return [r[sq * (r.shape[0] // seqs):(sq + 1) * (r.shape[0] // seqs), lo:hi] for r in k_refs]

        def values(lo, hi, mask=None, sq=sq):
            vs = [r[sq * (r.shape[0] // seqs):(sq + 1) * (r.shape[0] // seqs), lo:hi] for r in v_refs]
            return vs if mask is None else [v * mask for v in vs]

        rows = slice(sq * tq, (sq + 1) * tq)
        diff = masked_heads(q_ref[rows, Q_DIFF:Q_GQA], keys(K_DIFF, K_GQA), values(V_DIFF, V_GQA),
                            [_lane_mask(256, j * DIFF_QK, (j + 1) * DIFF_QK) for j in range(2 * DIFF_HEADS)])
        pending.append(functools.partial(finish_diff, rows=rows, diff=diff))
        if not stack_rows:
            flush()
        gqa = [masked_heads(q_ref[rows, Q_GQA + blk * 128:Q_GQA + (blk + 1) * 128],
                            keys(K_GQA + blk * 128, K_GQA + (blk + 1) * 128),
                            values(V_GQA + blk * 128, V_GQA + (blk + 1) * 128),
                            [_lane_mask(128, 0, GQA_HEAD_DIM), _lane_mask(128, GQA_HEAD_DIM, 2 * GQA_HEAD_DIM)])
               for blk in range(GQA_KV_HEADS)]
        pending.append(functools.partial(finish_gqa, rows=rows, gqa=gqa))
        if not stack_rows:
            flush()
        mla = [submit(q_ref[rows, Q_MLA + hd * MLA_BLK:Q_MLA + (hd + 1) * MLA_BLK],
                      keys(K_MLA + hd * MLA_BLK, K_MLA + (hd + 1) * MLA_BLK),
                      values(V_MLA + hd * MLA_BLK, V_MLA + (hd + 1) * MLA_BLK, upper)) for hd in range(MLA_HEADS)]
        pending.append(functools.partial(finish_mla, rows=rows, mla=mla))
        if not stack_rows:
            flush()
    if jobs:
        flush()


def _attention(q, k_own, v_own, vecs, layer, lam_init, n_batch, tq_total, k_cache=None, v_cache=None, to_bf16=()):
    past = 0 if k_cache is None else k_cache.shape[1]
    tk = past + tq_total
    if past:
        seqs, tq = 1, min(tq_total, LAT_QUERY_TILE)
    else:
        seqs, tq = CTX_SEQ_PER_STEP, CTX_SEQ_PER_STEP * tq_total
        assert n_batch % seqs == 0
    nq = max(tq_total // tq, 1)
    n_steps = n_batch * tq_total // tq
    own = lambda cols: pl.BlockSpec((seqs * tq_total, cols), lambda i: (i // nq, 0))
    in_specs = [pl.BlockSpec((tq, Q_COLS), lambda i: (i, 0))]
    if past:
        assert not to_bf16
        cached = lambda cols: pl.BlockSpec((None, past, cols), lambda i: (i // nq, 0, 0))
        in_specs += [cached(K_COLS), own(K_COLS), cached(V_COLS), own(V_COLS)]
        args = [q, k_cache, k_own, v_cache, v_own]
    else:
        in_specs += [own(K_COLS), own(V_COLS)]
        args = [q, k_own, v_own]
    in_specs.append(_layer_spec((1, VEC_END), layer))
    args.append(vecs)
    out_specs = [pl.BlockSpec((tq, O_COLS), lambda i: (i, 0))]
    out_shape = [jax.ShapeDtypeStruct((n_batch * tq_total, O_COLS), BF16)]
    for w in to_bf16:
        _, rows, cols = w.shape
        slab = rows // n_steps
        assert slab * n_steps == rows and slab % 16 == 0
        in_specs.append(pl.BlockSpec((None, slab, cols), lambda i: (layer, i, 0)))
        args.append(w)
        out_specs.append(pl.BlockSpec((slab, cols), lambda i: (i, 0)))
        out_shape.append(jax.ShapeDtypeStruct((rows, cols), BF16))
    return pl.pallas_call(
        functools.partial(_attn_kernel, lam_init=lam_init, past=past, n_cast=len(to_bf16), seqs=seqs),
        grid=(n_steps,),
        in_specs=in_specs,
        out_specs=out_specs,
        out_shape=out_shape,
        compiler_params=pltpu.CompilerParams(vmem_limit_bytes=VMEM_LIMIT),
        name="attention_tk%d" % tk,
    )(*args)


def _gelu_tanh(x):
    return 0.5 * x * (1.0 + jnp.tanh(math.sqrt(2.0 / math.pi) * (x + 0.044715 * (x * x * x))))


def _block_diag_lanes(blocks):
    pair = jnp.concatenate([blocks, blocks], axis=-1)
    tiled = jnp.concatenate([pair] * (SSM_LANES // 128), axis=-1)
    r = jnp.right_shift(lax.broadcasted_iota(jnp.int32, tiled.shape, 0), 4)
    c = jnp.right_shift(lax.broadcasted_iota(jnp.int32, tiled.shape, 1), 6)
    return jnp.where(r == c, tiled, 0.0)


def _ssm_kernel(*refs, n_batch, emit_state):
    uf_ref, ub_ref, h0_ref, par_ref, bre_ref, bim_ref, cre_ref, cim_ref, yf_ref, yb_ref = refs[:10]
    pos = 10
    if emit_state:
        hfin_ref = refs[pos]
        pos += 1
    bbar_scr, cblk_scr, hst_scr, bu_scr, hb_scr = refs[pos:]

    step = pl.program_id(0)
    t_chunk = uf_ref.shape[1]
    rows = n_batch * t_chunk
    abar = []
    for d in range(2):
        a_re = par_ref[d, 0:1, :]
        a_im = par_ref[d, 1:2, :]
        dt = jnp.exp(par_ref[d, 2:3, :])
        mag = jnp.exp(a_re * dt)
        abar.append((mag * jnp.cos(a_im * dt), mag * jnp.sin(a_im * dt)))

    @pl.when(step == 0)
    def _():
        for d in range(2):
            a_re = par_ref[d, 0:1, :]
            a_im = par_ref[d, 1:2, :]
            abr, abi = abar[d]
            den = a_re * a_re + a_im * a_im
            xr = abr - 1.0
            cr = (xr * a_re + abi * a_im) / den
            ci = (abi * a_re - xr * a_im) / den
            bre = _block_diag_lanes(bre_ref[d])
            bim = _block_diag_lanes(bim_ref[d])
            bbar_scr[d, :, 0:SSM_LANES] = (cr * bre - ci * bim).astype(BF16)
            bbar_scr[d, :, SSM_LANES:2 * SSM_LANES] = (cr * bim + ci * bre).astype(BF16)
            cblk_scr[d, :, 0:SSM_LANES] = _block_diag_lanes(cre_ref[d]).astype(BF16)
            cblk_scr[d, :, SSM_LANES:2 * SSM_LANES] = (-_block_diag_lanes(cim_ref[d])).astype(BF16)
        hst_scr[...] = h0_ref[...]

    for d, u_ref in enumerate((uf_ref, ub_ref)):
        u_tm = jnp.swapaxes(u_ref[...], 0, 1).reshape(rows, SSM_WIDTH)
        bu_scr[d] = _dot(u_tm.astype(BF16), bbar_scr[d])

    lw = 128
    if n_batch % 8 == 0:
        for j in range(SSM_LANES // lw):
            sre = slice(j * lw, (j + 1) * lw)
            sim = slice(SSM_LANES + j * lw, SSM_LANES + (j + 1) * lw)
            coef = [(jnp.broadcast_to(abar[d][0][:, sre], (n_batch, lw)),
                     jnp.broadcast_to(abar[d][1][:, sre], (n_batch, lw))) for d in range(2)]

            def body(i, carry, sre=sre, sim=sim, coef=coef):
                new = []
                for d in range(2):
                    hr, hi = carry[2 * d], carry[2 * d + 1]
                    ar, ai = coef[d]
                    t = i if d == 0 else t_chunk - 1 - i
                    r0 = pl.multiple_of(t * n_batch, n_batch)
                    nr = ar * hr - ai * hi + bu_scr[d, pl.ds(r0, n_batch), sre]
                    ni = ar * hi + ai * hr + bu_scr[d, pl.ds(r0, n_batch), sim]
                    hb_scr[d, pl.ds(r0, n_batch), sre] = nr.astype(BF16)
                    hb_scr[d, pl.ds(r0, n_batch), sim] = ni.astype(BF16)
                    new += [nr, ni]
                return tuple(new)

            init = (hst_scr[0, :, sre], hst_scr[0, :, sim], hst_scr[1, :, sre], hst_scr[1, :, sim])
            fin = lax.fori_loop(0, t_chunk, body, init, unroll=2)
            hst_scr[0, :, sre] = fin[0]
            hst_scr[0, :, sim] = fin[1]
            hst_scr[1, :, sre] = fin[2]
            hst_scr[1, :, sim] = fin[3]
    else:
        assert n_batch == 4
        n_blk = rows // 8
        low = lax.broadcasted_iota(jnp.int32, (8, lw), 0) < 4
        firsts = (low, jnp.logical_not(low))
        for j in range(SSM_LANES // lw):
            sre = slice(j * lw, (j + 1) * lw)
            sim = slice(SSM_LANES + j * lw, SSM_LANES + (j + 1) * lw)
            coef = []
            for d in range(2):
                ar = jnp.broadcast_to(abar[d][0][:, sre], (8, lw))
                ai = jnp.broadcast_to(abar[d][1][:, sre], (8, lw))
                coef.append((ar, ai, jnp.where(firsts[d], ar, ar * ar - ai * ai),
                             jnp.where(firsts[d], ai, 2.0 * ar * ai)))

            def body(i, carry, sre=sre, sim=sim, coef=coef):
                new = []
                for d in range(2):
                    pr, pi = carry[2 * d], carry[2 * d + 1]
                    ar, ai, cr, ci = coef[d]
                    first = firsts[d]
                    r16 = pl.multiple_of((2 * i if d == 0 else n_blk - 2 - 2 * i) * 8, 16)
                    tile_r, tile_i = [None, None], [None, None]
                    for half in ((0, 1) if d == 0 else (1, 0)):
                        vr = bu_scr[d, pl.ds(r16 + 8 * half, 8), sre]
                        vi = bu_scr[d, pl.ds(r16 + 8 * half, 8), sim]
                        sr = jnp.where(first, 0.0, pltpu.roll(vr, 4, axis=0))
                        si = jnp.where(first, 0.0, pltpu.roll(vi, 4, axis=0))
                        nr = cr * pr - ci * pi + (vr + (ar * sr - ai * si))
                        ni = cr * pi + ci * pr + (vi + (ar * si + ai * sr))
                        tile_r[half], tile_i[half] = nr, ni
                        pr = jnp.where(first, pltpu.roll(nr, 4, axis=0), nr)
                        pi = jnp.where(first, pltpu.roll(ni, 4, axis=0), ni)
                    hb_scr[d, pl.ds(r16, 16), sre] = jnp.concatenate(tile_r, axis=0).astype(BF16)
                    hb_scr[d, pl.ds(r16, 16), sim] = jnp.concatenate(tile_i, axis=0).astype(BF16)
                    new += [pr, pi]
                return tuple(new)

            init = (hst_scr[0, :, sre], hst_scr[0, :, sim], hst_scr[1, :, sre], hst_scr[1, :, sim])
            fin = lax.fori_loop(0, n_blk // 2, body, init, unroll=2)
            hst_scr[0, :, sre] = fin[0]
            hst_scr[0, :, sim] = fin[1]
            hst_scr[1, :, sre] = fin[2]
            hst_scr[1, :, sim] = fin[3]

    for d, y_ref in enumerate((yf_ref, yb_ref)):
        y = _dot_nt(hb_scr[d], cblk_scr[d])
        y_ref[...] = jnp.swapaxes(y.reshape(t_chunk, n_batch, SSM_WIDTH), 0, 1)

    if emit_state:
        @pl.when(step == pl.num_programs(0) - 1)
        def _():
            hfin_ref[...] = hst_scr[...]


def _ssm_scan(u_tok, h0, h0_spec, wts, layer, n_batch, seq, t_chunk, emit_state):
    n_steps = seq // t_chunk
    rows = n_batch * t_chunk
    h_rows = h0.shape[-2]
    fwd_blk = lambda k: (0, k, 0)
    bwd_blk = lambda k: (0, n_steps - 1 - k, 0)
    chunk = (n_batch, t_chunk, SSM_WIDTH)

    def par(shape):
        return _layer_spec((2,) + shape, layer)

    out_shape = [jax.ShapeDtypeStruct((n_batch, seq, SSM_WIDTH), F32)] * 2
    out_specs = [pl.BlockSpec(chunk, fwd_blk), pl.BlockSpec(chunk, bwd_blk)]
    if emit_state:
        out_shape.append(jax.ShapeDtypeStruct((2, h_rows, 2 * SSM_LANES), F32))
        out_specs.append(_const_spec((2, h_rows, 2 * SSM_LANES)))
    u3 = u_tok.reshape(n_batch, seq, SSM_WIDTH)
    return pl.pallas_call(
        functools.partial(_ssm_kernel, n_batch=n_batch, emit_state=emit_state),
        grid=(n_steps,),
        in_specs=[pl.BlockSpec(chunk, fwd_blk), pl.BlockSpec(chunk, bwd_blk), h0_spec, par((3, SSM_LANES)),
                  par((SSM_WIDTH, SSM_STATE)), par((SSM_WIDTH, SSM_STATE)),
                  par((SSM_WIDTH, SSM_STATE)), par((SSM_WIDTH, SSM_STATE))],
        out_specs=out_specs,
        out_shape=out_shape,
        scratch_shapes=[pltpu.VMEM((2, SSM_WIDTH, 2 * SSM_LANES), BF16),
                        pltpu.VMEM((2, SSM_WIDTH, 2 * SSM_LANES), BF16),
                        pltpu.VMEM((2, h_rows, 2 * SSM_LANES), F32),
                        pltpu.VMEM((2, rows, 2 * SSM_LANES), F32),
                        pltpu.VMEM((2, rows, 2 * SSM_LANES), BF16)],
        compiler_params=pltpu.CompilerParams(vmem_limit_bytes=VMEM_LIMIT),
        name="ssm_b%d" % n_batch,
    )(u3, u3, h0, wts["ssm_par"], wts["ssm_b_re"], wts["ssm_b_im"], wts["ssm_c_re"], wts["ssm_c_im"])


def _out_mlp_kernel(x_ref, oa_ref, yf_ref, yb_ref, u_ref, mod_ref, vec_ref, wout_ref, wglu_ref, w1_ref, w2_ref,
                    y_ref, *, final):
    g1 = mod_ref[:, 2 * D_MODEL:3 * D_MODEL]
    sh2 = mod_ref[:, 3 * D_MODEL:4 * D_MODEL]
    sc2 = mod_ref[:, 4 * D_MODEL:5 * D_MODEL]
    g2 = mod_ref[:, 5 * D_MODEL:6 * D_MODEL]
    ys = yf_ref[...] + yb_ref[...] + u_ref[...] * vec_ref[:, VEC_SSMD:VEC_SSMD + SSM_WIDTH]
    z = _dot(_gelu_tanh(ys).astype(BF16), wglu_ref[...])
    o_ssm = (z[:, 0:SSM_WIDTH] * _sigmoid(z[:, SSM_WIDTH:2 * SSM_WIDTH])).astype(BF16)
    mix = (_dot(oa_ref[:, 0:512], wout_ref[0:512, :]) + _dot(o_ssm, wout_ref[512:768, :])
           + _dot(oa_ref[:, 512:O_COLS], wout_ref[768:1024, :]))
    x1 = x_ref[...] + g1 * mix
    ms = jnp.mean(x1 * x1, axis=-1, keepdims=True)
    h = x1 * lax.rsqrt(ms + EPS) * vec_ref[:, VEC_N2G:VEC_N2G + D_MODEL]
    h = h * (1.0 + sc2) + sh2
    hb = h.astype(BF16)
    mlp = jnp.zeros_like(x1)
    for c0 in range(0, D_FF, FF_CHUNK):
        a = jnp.maximum(_dot(hb, w1_ref[:, c0:c0 + FF_CHUNK]), 0.0)
        mlp = mlp + _dot((a * a).astype(BF16), w2_ref[c0:c0 + FF_CHUNK, :])
    x2 = x1 + g2 * mlp
    if final:
        ms2 = jnp.mean(x2 * x2, axis=-1, keepdims=True)
        x2 = x2 * lax.rsqrt(ms2 + EPS) * vec_ref[:, VEC_FNG:VEC_FNG + D_MODEL]
    y_ref[...] = x2


def _out_mlp(x2d, o_attn, y_fwd, y_bwd, u, mod3, layer, wts, mlp_weights, n_batch, seq, latent, final):
    tm = MLP_TILE
    n_tok = n_batch * seq
    w_out, w1, w2 = mlp_weights
    resident = lambda shape: pl.BlockSpec(shape, lambda i: (0, 0), pipeline_mode=pl.Buffered(1))
    if latent:
        tpb = seq // tm
        mod_row = lambda i: (layer * MOD_ROWS + 1 + i // tpb, 0, 0)
    else:
        mod_row = lambda i: (layer * MOD_ROWS, 0, 0)
    tile = lambda i: (i, 0)
    return pl.pallas_call(
        functools.partial(_out_mlp_kernel, final=final),
        grid=(n_tok // tm,),
        in_specs=[
            pl.BlockSpec((tm, D_MODEL), tile),
            pl.BlockSpec((tm, O_COLS), tile),
            pl.BlockSpec((tm, SSM_WIDTH), tile),
            pl.BlockSpec((tm, SSM_WIDTH), tile),
            pl.BlockSpec((tm, SSM_WIDTH), tile),
            pl.BlockSpec((None, 1, N_MOD * D_MODEL), mod_row),
            _layer_spec((1, VEC_END), layer),
            resident((D_MODEL, D_MODEL)),
            _layer_spec((SSM_WIDTH, 2 * SSM_WIDTH), layer, single_buffer=True),
            resident((D_MODEL, D_FF)),
            resident((D_FF, D_MODEL)),
        ],
        out_specs=pl.BlockSpec((tm, D_MODEL), tile),
        out_shape=jax.ShapeDtypeStruct((n_tok, D_MODEL), F32),
        compiler_params=pltpu.CompilerParams(vmem_limit_bytes=VMEM_LIMIT),
        name="out_mlp_%s%s" % ("lat" if latent else "ctx", "_final" if final else ""),
    )(x2d, o_attn, y_fwd, y_bwd, u, mod3, wts["vecs"], w_out, wts["w_glu"], w1, w2)


def _rope_tables(t_len, chunk, n_chunks):
    n = chunk // 4
    rows = t_len // GRID_W
    row = np.repeat(np.arange(rows), GRID_W).astype(np.float32)
    col = np.tile(np.arange(GRID_W), rows).astype(np.float32)
    freq = (np.float32(ROPE_THETA) ** (-np.arange(n, dtype=np.float32) / np.float32(n))).astype(np.float32)
    ang_r = (row[:, None] * freq).astype(np.float32)
    ang_c = (col[:, None] * freq).astype(np.float32)
    cr, sr, cc, sc = np.cos(ang_r), np.sin(ang_r), np.cos(ang_c), np.sin(ang_c)
    z = np.zeros_like(cr)
    tabs = np.stack([np.concatenate([cr, cr, cc, cc], axis=-1),
                     np.concatenate([-sr, z, -sc, z], axis=-1),
                     np.concatenate([z, sr, z, sc], axis=-1)]).astype(np.float32)
    return np.tile(tabs, (1, 1, n_chunks))


def _mla_rope_tables(t_len):
    tabs = _rope_tables(t_len, MLA_ROPE, 1)
    ident = np.stack([np.ones((t_len, MLA_NOPE), np.float32), np.zeros((t_len, MLA_NOPE), np.float32),
                      np.zeros((t_len, MLA_NOPE), np.float32)])
    tail = ident[:, :, :MLA_BLK - MLA_QK]
    return np.concatenate([ident, tabs, tail], axis=-1)


def _pack_vectors(p, n_layers):
    def rows(a):
        return a.reshape(n_layers, 1, -1)

    fng = jnp.broadcast_to(p["final_norm_g"].reshape(1, 1, D_MODEL), (n_layers, 1, D_MODEL))
    return jnp.concatenate([
        rows(p["norm1_g"]), rows(p["norm2_g"]), fng,
        jnp.tile(rows(p["gqa_qn_g"]), (1, 1, GQA_HEADS)),
        jnp.tile(rows(p["gqa_kn_g"]), (1, 1, GQA_KV_HEADS)),
        rows(p["mla_qn_g"]), jnp.zeros((n_layers, 1, CQ_PAD - MLA_Q_RANK), F32),
        rows(p["mla_kvn_g"]),
        jnp.tile(rows(p["diff_subln_g"]), (1, 1, DIFF_HEADS)),
        rows(p["ssm_d"]),
        rows(p["diff_lq1"]), rows(p["diff_lk1"]), rows(p["diff_lq2"]), rows(p["diff_lk2"]),
    ], axis=-1)


def kernel(x_prompt, x_sample, cache_diff_k, cache_diff_v, cache_gqa_k, cache_gqa_v, cache_mla_ckv, cache_mla_krope, state_ssm_re, state_ssm_im, c, c_ctx, norm1_g, norm2_g, w_ada, b_ada, w_in, w_out, diff_lq1, diff_lk1, diff_lq2, diff_lk2, diff_subln_g, gqa_qn_g, gqa_kn_g, ssm_a_re, ssm_a_im, ssm_log_dt, ssm_b_re, ssm_b_im, ssm_c_re, ssm_c_im, ssm_d, ssm_w_glu, mla_qn_g, mla_kvn_g, mla_w_uq, mla_w_ukv, mlp_w1, mlp_w2, final_norm_g):
    n_layers = w_in.shape[0]
    n_ctx, seq, _ = x_prompt.shape
    n_lat, t_lat, _ = x_sample.shape
    past = cache_diff_k.shape[2]
    assert n_lat + 1 <= MOD_ROWS and n_lat == 4 and n_ctx % 8 == 0
    assert seq == TOKEN_TILE and t_lat % TOKEN_TILE == 0
    assert (n_ctx * seq) % MLP_TILE == 0 and t_lat % MLP_TILE == 0
    assert PROJ_TILE % seq == 0 and n_ctx % (PROJ_TILE // seq) == 0 and t_lat % PROJ_TILE == 0

    p = dict(norm1_g=norm1_g, norm2_g=norm2_g, final_norm_g=final_norm_g, gqa_qn_g=gqa_qn_g, gqa_kn_g=gqa_kn_g,
             mla_qn_g=mla_qn_g, mla_kvn_g=mla_kvn_g, diff_subln_g=diff_subln_g, ssm_d=ssm_d, diff_lq1=diff_lq1,
             diff_lk1=diff_lk1, diff_lq2=diff_lq2, diff_lk2=diff_lk2)
    ssm_rows = SSM_GROUPS * SSM_GROUP
    wts = {
        "vecs": _pack_vectors(p, n_layers),
        "w_in_t": jnp.swapaxes(w_in, 1, 2),
        "w_uq": mla_w_uq, "w_ukv": mla_w_ukv,
        "w_glu": ssm_w_glu.astype(BF16),
        "ssm_par": jnp.stack([ssm_a_re.reshape(n_layers, 2, SSM_LANES), ssm_a_im.reshape(n_layers, 2, SSM_LANES),
                              jnp.repeat(ssm_log_dt, SSM_STATE, axis=-1)], axis=2),
        "ssm_b_re": jnp.swapaxes(ssm_b_re, 3, 4).reshape(n_layers, 2, ssm_rows, SSM_STATE),
        "ssm_b_im": jnp.swapaxes(ssm_b_im, 3, 4).reshape(n_layers, 2, ssm_rows, SSM_STATE),
        "ssm_c_re": ssm_c_re.reshape(n_layers, 2, ssm_rows, SSM_STATE),
        "ssm_c_im": ssm_c_im.reshape(n_layers, 2, ssm_rows, SSM_STATE),
    }

    cond = jnp.concatenate([c_ctx[None], c, jnp.zeros((MOD_ROWS - 1 - n_lat, D_MODEL), F32)], axis=0)
    mod3 = _adaln(cond, w_ada, b_ada).reshape(n_layers * MOD_ROWS, 1, N_MOD * D_MODEL)

    rope_tabs = (jnp.asarray(_rope_tables(t_lat, DIFF_QK, 256 // DIFF_QK)),
                 jnp.asarray(_rope_tables(t_lat, GQA_HEAD_DIM, 256 // GQA_HEAD_DIM)),
                 jnp.asarray(_mla_rope_tables(t_lat)))
    caches = (jnp.transpose(cache_diff_k, (0, 1, 3, 4, 2)).reshape(n_lat, n_layers, 256, past),
              jnp.transpose(cache_diff_v, (0, 1, 3, 4, 2)).reshape(n_lat, n_layers, 256, past),
              jnp.transpose(cache_gqa_k, (0, 1, 3, 4, 2)).reshape(n_lat, n_layers, 128, past),
              jnp.transpose(cache_gqa_v, (0, 1, 3, 4, 2)).reshape(n_lat, n_layers, 128, past),
              cache_mla_ckv,
              jnp.swapaxes(cache_mla_krope, 2, 3))

    lam_inits = [0.8 - 0.6 * math.exp(-0.3 * l) for l in range(n_layers)]

    x = x_prompt.reshape(n_ctx * seq, D_MODEL)
    zero_state = jnp.zeros((2, n_ctx, 2 * SSM_LANES), F32)
    zero_spec = _const_spec((2, n_ctx, 2 * SSM_LANES))
    kept = None
    states = []
    mlp_weights = []
    for l in range(n_layers):
        res = _in_proj(x, mod3, l, wts, n_ctx, seq, latent=False, prev_caches=kept)
        q, k, v, u = res[:4]
        kept = res[4:]
        o_attn, *mlp_w = _attention(q, k, v, wts["vecs"], l, lam_inits[l], n_ctx, seq,
                                    to_bf16=(w_out, mlp_w1, mlp_w2))
        mlp_weights.append(mlp_w)
        y_fwd, y_bwd, h_fin = _ssm_scan(u, zero_state, zero_spec, wts, l, n_ctx, seq, CTX_SSM_CHUNK,
                                        emit_state=True)
        x = _out_mlp(x, o_attn, y_fwd.reshape(n_ctx * seq, SSM_WIDTH), y_bwd.reshape(n_ctx * seq, SSM_WIDTH), u,
                     mod3, l, wts, mlp_w, n_ctx, seq, latent=False, final=(l == n_layers - 1))
        states.append(h_fin)
    y_prompt = x.reshape(n_ctx, seq, D_MODEL)

    x = x_sample.reshape(n_lat * t_lat, D_MODEL)
    h0 = jnp.concatenate([state_ssm_re.reshape(n_lat, n_layers, 2, SSM_LANES),
                          state_ssm_im.reshape(n_lat, n_layers, 2, SSM_LANES)], axis=-1)
    h0 = jnp.transpose(h0, (1, 2, 0, 3))
    h0 = jnp.concatenate([h0, h0], axis=2)
    for l in range(n_layers):
        h0_spec = _layer_spec((2, 2 * n_lat, 2 * SSM_LANES), l)
        q, k, v, u = _in_proj(x, mod3, l, wts, n_lat, t_lat, latent=True, rope_tabs=rope_tabs)
        kc, vc = _cache_prep(caches, l, wts["w_ukv"])
        o_attn, = _attention(q, k, v, wts["vecs"], l, lam_inits[l], n_lat, t_lat, k_cache=kc, v_cache=vc)
        y_fwd, y_bwd = _ssm_scan(u, h0, h0_spec, wts, l, n_lat, t_lat, LAT_SSM_CHUNK, emit_state=False)
        x = _out_mlp(x, o_attn, y_fwd.reshape(n_lat * t_lat, SSM_WIDTH), y_bwd.reshape(n_lat * t_lat, SSM_WIDTH), u,
                     mod3, l, wts, mlp_weights[l], n_lat, t_lat, latent=True, final=(l == n_layers - 1))
    y_sample = x.reshape(n_lat, t_lat, D_MODEL)

    cdk, cdv, cgk, cgv, cckv, ckr = kept

    def token_major(a, heads, dim):
        return jnp.transpose(a.reshape(n_ctx, n_layers, heads, dim, seq), (0, 1, 4, 2, 3))

    st = jnp.transpose(jnp.stack(states), (2, 0, 1, 3))
    return (y_prompt, y_sample,
            token_major(cdk, DIFF_HEADS, 2 * DIFF_QK), token_major(cdv, DIFF_HEADS, DIFF_V),
            token_major(cgk, GQA_KV_HEADS, GQA_HEAD_DIM), token_major(cgv, GQA_KV_HEADS, GQA_HEAD_DIM),
            cckv, jnp.swapaxes(ckr, 2, 3),
            st[..., :SSM_LANES].reshape(n_ctx, n_layers, 2, SSM_GROUPS, SSM_STATE),
            st[..., SSM_LANES:].reshape(n_ctx, n_layers, 2, SSM_GROUPS, SSM_STATE))
```
